```python
import jax, jax.numpy as jnp
from jax import lax
import numpy as np

D_MODEL = 2048
BATCH = 8
SEQ = 2048
DEPTH = 2

GRID_W = 64
CTX_LEN = 256
HEAD_DIM = 128
N_Q_HEADS = 8
N_KV_HEADS = 2
GQA_GROUP = N_Q_HEADS // N_KV_HEADS
ATTN_W = N_Q_HEADS * HEAD_DIM
KV_W = N_KV_HEADS * HEAD_DIM
Q_BLOCK = 128
ROPE_THETA = 10000.0
ATTN_SCALE = HEAD_DIM ** -0.5
SCONV_W = 512
SCONV_K = 3
CONF_W = 512
CONF_K = 31
MIX_W = ATTN_W + SCONV_W + CONF_W
IN_SIZES = (ATTN_W, KV_W, KV_W, SCONV_W, SCONV_W, SCONV_W, CONF_W, CONF_W)
IN_W = sum(IN_SIZES)
IN_SPLITS = tuple(int(i) for i in np.cumsum(IN_SIZES)[:-1])
FFN_DIM = 5632
N_EXPERTS = 8
TOP_K = 2
MOE_BLOCK = 256
EPS = 1e-6

kernel_name = "hymba_style_hybrid_dit_block"


def rmsnorm(x, g):
    xf = x.astype(jnp.float32)
    y = xf * lax.rsqrt(jnp.mean(xf * xf, axis=-1, keepdims=True) + EPS)
    return (y * g.astype(jnp.float32)).astype(x.dtype)


def unit_rms(x):
    xf = x.astype(jnp.float32)
    return (xf * lax.rsqrt(jnp.mean(xf * xf, axis=-1, keepdims=True) + EPS)).astype(x.dtype)


def layernorm(x, g, b):
    xf = x.astype(jnp.float32)
    mu = jnp.mean(xf, axis=-1, keepdims=True)
    xc = xf - mu
    var = jnp.mean(xc * xc, axis=-1, keepdims=True)
    y = xc * lax.rsqrt(var + EPS) * g.astype(jnp.float32) + b.astype(jnp.float32)
    return y.astype(x.dtype)


def modulate(h, shift, scale):
    return h * (1.0 + scale) + shift


def heads(z, n):
    return z.reshape(z.shape[0], z.shape[1], n, HEAD_DIM)


def depthwise_conv(x, w):
    k = w.shape[0]
    return lax.conv_general_dilated(
        x, w[:, None, :].astype(x.dtype), window_strides=(1,), padding=[(k // 2, k // 2)],
        dimension_numbers=('NWC', 'WIO', 'NWC'), feature_group_count=x.shape[-1])


def axial_rope_tables(n_tokens):
    rows = n_tokens // GRID_W
    row = jnp.repeat(jnp.arange(rows), GRID_W).astype(jnp.float32)
    col = jnp.tile(jnp.arange(GRID_W), rows).astype(jnp.float32)
    axis_dim = HEAD_DIM // 2
    inv_freq = ROPE_THETA ** (-jnp.arange(0, axis_dim, 2, dtype=jnp.float32) / axis_dim)
    ang_r = row[:, None] * inv_freq
    ang_c = col[:, None] * inv_freq
    return (jnp.cos(ang_r)[:, None, :], jnp.sin(ang_r)[:, None, :],
            jnp.cos(ang_c)[:, None, :], jnp.sin(ang_c)[:, None, :])


def _rotate(z, cos, sin):
    half = z.shape[-1] // 2
    z1, z2 = z[..., :half], z[..., half:]
    return jnp.concatenate([z1 * cos - z2 * sin, z1 * sin + z2 * cos], axis=-1)


def apply_axial_rope(x, cos_r, sin_r, cos_c, sin_c):
    xf = x.astype(jnp.float32)
    axis_dim = HEAD_DIM // 2
    out = jnp.concatenate([_rotate(xf[..., :axis_dim], cos_r, sin_r),
                           _rotate(xf[..., axis_dim:], cos_c, sin_c)], axis=-1)
    return out.astype(x.dtype)


def attend(qg, k, v):
    s = jnp.einsum('bqhgd,bkhd->bhgqk', qg, k, preferred_element_type=jnp.float32) * ATTN_SCALE
    p = jax.nn.softmax(s, axis=-1).astype(v.dtype)
    return jnp.einsum('bhgqk,bkhd->bqhgd', p, v)


def latent_attention(q, k_all, v_all):
    b, s = q.shape[0], q.shape[1]
    nb = s // Q_BLOCK
    qb = q.reshape(b, nb, Q_BLOCK, N_KV_HEADS, GQA_GROUP, HEAD_DIM).swapaxes(0, 1)
    o = lax.map(lambda t: attend(t, k_all, v_all), qb)
    return o.swapaxes(0, 1).reshape(b, s, ATTN_W)


def context_attention(qc, kc, vc):
    b, n = qc.shape[0], qc.shape[1]
    qg = qc.reshape(b, n, N_KV_HEADS, GQA_GROUP, HEAD_DIM)
    return attend(qg, kc, vc).reshape(b, n, ATTN_W)


def local_mixers(sb, sg, sx, ga, gb, sconv_w, conf_dw, conf_db, conf_ln_g, conf_ln_b, conf_pw, conf_pb):
    y_s = sb * depthwise_conv(sg * sx, sconv_w)
    u = ga * jax.nn.sigmoid(gb)
    u = depthwise_conv(u, conf_dw) + conf_db
    u = jax.nn.silu(layernorm(u, conf_ln_g, conf_ln_b))
    y_c = u @ conf_pw + conf_pb
    return y_s, y_c


def mixer_out(o_att, y_s, y_c, g_group, w_o):
    merged = jnp.concatenate([unit_rms(o_att), unit_rms(y_s), unit_rms(y_c)], axis=-1) * g_group
    return merged @ w_o


def swiglu(t, wg, wu, wd):
    return (jax.nn.silu(t @ wg) * (t @ wu)) @ wd


def moe_swiglu(t, router_w, wg, wu, wd):
    n, d = t.shape
    logits = jnp.dot(t.astype(jnp.float32), router_w.astype(jnp.float32))
    top_logit, top_e = lax.top_k(logits, TOP_K)
    gates = jax.nn.softmax(top_logit, axis=-1)
    nk = n * TOP_K
    flat_e = top_e.reshape(nk)
    flat_tok = jnp.repeat(jnp.arange(n), TOP_K)
    order = jnp.argsort(flat_e)
    e_s, tok_s = flat_e[order], flat_tok[order]
    gate_s = gates.reshape(nk)[order]
    counts = jnp.bincount(flat_e, length=N_EXPERTS)
    starts = jnp.cumsum(counts) - counts
    padded = (counts + MOE_BLOCK - 1) // MOE_BLOCK * MOE_BLOCK
    pends = jnp.cumsum(padded)
    pstarts = pends - padded
    dest = pstarts[e_s] + jnp.arange(nk) - starts[e_s]
    n_blocks = (nk + N_EXPERTS * (MOE_BLOCK - 1) + MOE_BLOCK - 1) // MOE_BLOCK
    buf = jnp.zeros((n_blocks * MOE_BLOCK, d), t.dtype).at[dest].set(t[tok_s])
    blk_e = jnp.minimum(jnp.searchsorted(pends, jnp.arange(n_blocks) * MOE_BLOCK, side='right'),
                        N_EXPERTS - 1)

    def expert_block(args):
        xb, e = args
        return swiglu(xb, wg[e], wu[e], wd[e])

    out_buf = lax.map(expert_block, (buf.reshape(n_blocks, MOE_BLOCK, d), blk_e))
    y = out_buf.reshape(n_blocks * MOE_BLOCK, d)[dest] * gate_s[:, None].astype(t.dtype)
    return jax.ops.segment_sum(y, tok_s, num_segments=n)


def setup_inputs(seed: int = 0) -> dict:
    key = jax.random.key(seed)
    ks = iter(jax.random.split(key, 40))
    D = D_MODEL
    n_dense = (DEPTH + 1) // 2
    n_moe = DEPTH // 2

    def nrm(shape, scale):
        return jax.random.normal(next(ks), shape, jnp.float32) * scale

    def gain(shape):
        return 1.0 + nrm(shape, 0.05)

    return {
        'x': nrm((BATCH, SEQ, D), 1.0),
        'c': nrm((BATCH, D), 1.0),
        'ctx': nrm((BATCH, CTX_LEN, D), 1.0),
        'c_ctx': nrm((D,), 1.0),
        'w_ada': nrm((DEPTH, D, 6 * D), 0.5 * D ** -0.5),
        'b_ada': nrm((DEPTH, 6 * D), 0.02),
        'g_mix': gain((DEPTH, D)),
        'w_in': nrm((DEPTH, D, IN_W), D ** -0.5),
        'g_q': gain((DEPTH, HEAD_DIM)),
        'g_k': gain((DEPTH, HEAD_DIM)),
        'sconv_w': nrm((DEPTH, SCONV_K, SCONV_W), SCONV_K ** -0.5),
        'conf_dw': nrm((DEPTH, CONF_K, CONF_W), CONF_K ** -0.5),
        'conf_db': nrm((DEPTH, CONF_W), 0.02),
        'conf_ln_g': gain((DEPTH, CONF_W)),
        'conf_ln_b': nrm((DEPTH, CONF_W), 0.02),
        'conf_pw': nrm((DEPTH, CONF_W, CONF_W), CONF_W ** -0.5),
        'conf_pb': nrm((DEPTH, CONF_W), 0.02),
        'g_group': gain((DEPTH, MIX_W)),
        'w_o': nrm((DEPTH, MIX_W, D), MIX_W ** -0.5),
        'g_ffn': gain((DEPTH, D)),
        'dense_wg': nrm((n_dense, D, FFN_DIM), D ** -0.5),
        'dense_wu': nrm((n_dense, D, FFN_DIM), D ** -0.5),
        'dense_wd': nrm((n_dense, FFN_DIM, D), FFN_DIM ** -0.5),
        'router_w': nrm((n_moe, D, N_EXPERTS), D ** -0.5),
        'moe_wg': nrm((n_moe, N_EXPERTS, D, FFN_DIM), D ** -0.5),
        'moe_wu': nrm((n_moe, N_EXPERTS, D, FFN_DIM), D ** -0.5),
        'moe_wd': nrm((n_moe, N_EXPERTS, FFN_DIM, D), FFN_DIM ** -0.5),
        'g_final': gain((D,)),
    }


def reference(x, c, ctx, c_ctx, w_ada, b_ada, g_mix, w_in, g_q, g_k, sconv_w, conf_dw, conf_db,
              conf_ln_g, conf_ln_b, conf_pw, conf_pb, g_group, w_o, g_ffn, dense_wg, dense_wu,
              dense_wd, router_w, moe_wg, moe_wu, moe_wd, g_final):
    b, s, d = x.shape
    n_ctx = ctx.shape[1]
    rope = axial_rope_tables(s)
    xc = ctx
    c_act = jax.nn.silu(c)
    cc_act = jax.nn.silu(c_ctx)
    for l in range(DEPTH):
        last = l == DEPTH - 1
        mod = jnp.split((c_act @ w_ada[l] + b_ada[l])[:, None, :], 6, axis=-1)
        modc = jnp.split(cc_act @ w_ada[l] + b_ada[l], 6, axis=-1)

        h = modulate(rmsnorm(x, g_mix[l]), mod[0], mod[1])
        hc = modulate(rmsnorm(xc, g_mix[l]), modc[0], modc[1])
        q, k, v, sb, sg, sx, ga, gb = jnp.split(h @ w_in[l], IN_SPLITS, axis=-1)
        qc, kc, vc, sbc, sgc, sxc, gac, gbc = jnp.split(hc @ w_in[l], IN_SPLITS, axis=-1)
        q = apply_axial_rope(rmsnorm(heads(q, N_Q_HEADS), g_q[l]), *rope)
        k = apply_axial_rope(rmsnorm(heads(k, N_KV_HEADS), g_k[l]), *rope)
        v = heads(v, N_KV_HEADS)
        kc = rmsnorm(heads(kc, N_KV_HEADS), g_k[l])
        vc = heads(vc, N_KV_HEADS)
        o_att = latent_attention(q, jnp.concatenate([kc, k], axis=1), jnp.concatenate([vc, v], axis=1))
        y_s, y_c = local_mixers(sb, sg, sx, ga, gb, sconv_w[l], conf_dw[l], conf_db[l],
                                conf_ln_g[l], conf_ln_b[l], conf_pw[l], conf_pb[l])
        x = x + mod[2] * mixer_out(o_att, y_s, y_c, g_group[l], w_o[l])
        if not last:
            qc = rmsnorm(heads(qc, N_Q_HEADS), g_q[l])
            o_att_c = context_attention(qc, kc, vc)
            y_sc, y_cc = local_mixers(sbc, sgc, sxc, gac, gbc, sconv_w[l], conf_dw[l], conf_db[l],
                                      conf_ln_g[l], conf_ln_b[l], conf_pw[l], conf_pb[l])
            xc = xc + modc[2] * mixer_out(o_att_c, y_sc, y_cc, g_group[l], w_o[l])

        h2 = modulate(rmsnorm(x, g_ffn[l]), mod[3], mod[4]).reshape(b * s, d)
        if last:
            tokens = h2
        else:
            hc2 = modulate(rmsnorm(xc, g_ffn[l]), modc[3], modc[4]).reshape(b * n_ctx, d)
            tokens = jnp.concatenate([h2, hc2], axis=0)
        if l % 2 == 0:
            f = swiglu(tokens, dense_wg[l // 2], dense_wu[l // 2], dense_wd[l // 2])
        else:
            f = moe_swiglu(tokens, router_w[l // 2], moe_wg[l // 2], moe_wu[l // 2], moe_wd[l // 2])
        x = x + mod[5] * f[:b * s].reshape(b, s, d)
        if not last:
            xc = xc + modc[5] * f[b * s:].reshape(b, n_ctx, d)
    return rmsnorm(x, g_final)
```

```python
import functools

import jax
import jax.numpy as jnp
from jax import lax
from jax.experimental import pallas as pl
from jax.experimental.pallas import tpu as pltpu

F32 = jnp.float32
BF16 = jnp.bfloat16

GRID_W = 64
HEAD_DIM = 128
N_Q_HEADS = 8
N_KV_HEADS = 2
GQA_GROUP = N_Q_HEADS // N_KV_HEADS
ATTN_W = N_Q_HEADS * HEAD_DIM
KV_W = N_KV_HEADS * HEAD_DIM
ROPE_THETA = 10000.0
SCONV_W = 512
SCONV_K = 3
CONF_W = 512
CONF_K = 31
N_EXPERTS = 8
EPS = 1e-6
LOG2E = 1.4426950408889634

V7X_VMEM_LIMIT_BYTES = 56 * 1024 * 1024
MOD_ROWS = 16
IN_TN = 512
EXPERT_TM = 512


def _cparams(n_axes):
    return pltpu.CompilerParams(dimension_semantics=("arbitrary",) * n_axes,
                                vmem_limit_bytes=V7X_VMEM_LIMIT_BYTES)


def _sigmoid(x):
    return 1.0 / (1.0 + jnp.exp(-x))


def _norm_modulate(x, g, shift, scale):
    ms = jnp.mean(x * x, axis=-1, keepdims=True)
    y = x * lax.rsqrt(ms + EPS) * g
    return y * (1.0 + scale) + shift


def _norm_modulate_rows(x_ref, h_ref, g, shift, scale, chunk):
    rows = x_ref.shape[0]

    def body(c, carry):
        r = pl.multiple_of(c * chunk, chunk)
        h_ref[pl.ds(r, chunk), :] = _norm_modulate(
            x_ref[pl.ds(r, chunk), :], g, shift, scale).astype(h_ref.dtype)
        return carry

    lax.fori_loop(0, rows // chunk, body, 0)


def _ada_kernel(c_ref, w_ref, b_ref, o_ref):
    a = c_ref[...]
    a = (a * _sigmoid(a)).astype(BF16)
    o_ref[0] = jnp.dot(a, w_ref[0].astype(BF16), preferred_element_type=F32) + b_ref[0]


def _ada(cin, w_ada, b_ada):
    depth, d, n = w_ada.shape
    tn = 1536
    return pl.pallas_call(
        _ada_kernel,
        grid=(depth, n // tn),
        in_specs=[pl.BlockSpec((MOD_ROWS, d), lambda l, j: (0, 0)),
                  pl.BlockSpec((1, d, tn), lambda l, j: (l, 0, j)),
                  pl.BlockSpec((1, 1, tn), lambda l, j: (l, 0, j))],
        out_specs=pl.BlockSpec((1, MOD_ROWS, tn), lambda l, j: (l, 0, j)),
        out_shape=jax.ShapeDtypeStruct((depth, MOD_ROWS, n), F32),
        compiler_params=_cparams(2),
        name="ada",
    )(cin, w_ada, b_ada.reshape(depth, 1, n))


def _head_norm(seg, gain):
    ms = jnp.mean(seg * seg, axis=-1, keepdims=True)
    return seg * lax.rsqrt(ms + EPS) * gain


def _rope(n, cos_t, sin_t):
    lane = lax.broadcasted_iota(jnp.int32, n.shape, 1)
    fwd = pltpu.roll(n, 32, 1)
    bwd = pltpu.roll(n, 96, 1)
    partner = jnp.where((lane // 32) % 2 == 0, bwd, fwd)
    return n * cos_t + partner * sin_t


def _in_kernel(x_ref, shift_ref, scale_ref, g_ref, w_ref, gq_ref, gk_ref, cos_ref, sin_ref,
               o_ref, h_ref, *, rope, j0, q_scale):
    @pl.when(pl.program_id(1) == 0)
    def _():
        _norm_modulate_rows(x_ref, h_ref, g_ref[...], shift_ref[0], scale_ref[0], 256)

    j = pl.program_id(1) + j0
    acc = jnp.dot(h_ref[...], w_ref[...], preferred_element_type=F32)

    def head(seg, gain, scale):
        n = _head_norm(seg, gain)
        if rope:
            n = _rope(n, cos_ref[...], sin_ref[...])
        if scale != 1.0:
            n = n * scale
        return n

    @pl.when(j < 2)
    def _():
        for hd in range(IN_TN // HEAD_DIM):
            sl = slice(hd * HEAD_DIM, (hd + 1) * HEAD_DIM)
            o_ref[:, sl] = head(acc[:, sl], gq_ref[...], q_scale).astype(o_ref.dtype)

    @pl.when(j == 2)
    def _():
        for hd in range(N_KV_HEADS):
            sl = slice(hd * HEAD_DIM, (hd + 1) * HEAD_DIM)
            o_ref[:, sl] = head(acc[:, sl], gk_ref[...], 1.0).astype(o_ref.dtype)
        o_ref[:, KV_W:] = acc[:, KV_W:].astype(o_ref.dtype)

    @pl.when(j > 2)
    def _():
        o_ref[...] = acc.astype(o_ref.dtype)


def _in_proj(x2, mod3, mod_row, g, w, gq, gk, cos_t, sin_t, *, tm, rope, j0, nj, seq):
    m, d = x2.shape
    tiles_per_seq = seq // tm
    q_scale = HEAD_DIM ** -0.5 * LOG2E
    kern = functools.partial(_in_kernel, rope=rope, j0=j0, q_scale=q_scale)
    return pl.pallas_call(
        kern,
        grid=(m // tm, nj),
        in_specs=[pl.BlockSpec((tm, d), lambda i, j: (i, 0)),
                  pl.BlockSpec((1, 1, d), lambda i, j: (mod_row(i), 0, 0)),
                  pl.BlockSpec((1, 1, d), lambda i, j: (mod_row(i), 0, 1)),
                  pl.BlockSpec((1, d), lambda i, j: (0, 0)),
                  pl.BlockSpec((d, IN_TN), lambda i, j: (0, j + j0)),
                  pl.BlockSpec((1, HEAD_DIM), lambda i, j: (0, 0)),
                  pl.BlockSpec((1, HEAD_DIM), lambda i, j: (0, 0)),
                  pl.BlockSpec((tm, HEAD_DIM), lambda i, j: (i % tiles_per_seq, 0)),
                  pl.BlockSpec((tm, HEAD_DIM), lambda i, j: (i % tiles_per_seq, 0))],
        out_specs=pl.BlockSpec((tm, IN_TN), lambda i, j: (i, j)),
        out_shape=jax.ShapeDtypeStruct((m, nj * IN_TN), BF16),
        scratch_shapes=[pltpu.VMEM((tm, d), BF16)],
        compiler_params=_cparams(2),
        name="in_proj",
    )(x2, mod3, mod3, g, w, gq, gk, cos_t, sin_t)


def _attn_kernel(q_ref, kc_ref, vc_ref, kl_ref, vl_ref, gg_ref, o_ref, acc_ref, *, has_latent):
    tq = q_ref.shape[0]
    dn = (((1,), (1,)), ((), ()))
    ssq = jnp.zeros((tq, 1), F32)
    for hd in range(N_Q_HEADS):
        kv = hd // GQA_GROUP
        ksl = slice(kv * HEAD_DIM, (kv + 1) * HEAD_DIM)
        q = q_ref[:, hd * HEAD_DIM:(hd + 1) * HEAD_DIM]
        s_c = lax.dot_general(q, kc_ref[:, ksl], dn, preferred_element_type=F32)
        mx = jnp.max(s_c, axis=-1, keepdims=True)
        if has_latent:
            s_l = lax.dot_general(q, kl_ref[:, ksl], dn, preferred_element_type=F32)
            mx = jnp.maximum(mx, jnp.max(s_l, axis=-1, keepdims=True))
        p_c = jnp.exp2(s_c - mx)
        den = jnp.sum(p_c, axis=-1, keepdims=True)
        o = jnp.dot(p_c.astype(BF16), vc_ref[:, ksl], preferred_element_type=F32)
        if has_latent:
            p_l = jnp.exp2(s_l - mx)
            den = den + jnp.sum(p_l, axis=-1, keepdims=True)
            o = o + jnp.dot(p_l.astype(BF16), vl_ref[:, ksl], preferred_element_type=F32)
        o = o * (1.0 / den)
        ssq = ssq + jnp.sum(o * o, axis=-1, keepdims=True)
        acc_ref[:, hd * HEAD_DIM:(hd + 1) * HEAD_DIM] = o
    inv = lax.rsqrt(ssq * (1.0 / ATTN_W) + EPS)
    o_ref[...] = (acc_ref[...] * inv * gg_ref[...]).astype(o_ref.dtype)


def _attention(qsrc, csrc, lsrc, gg, *, n_batch, q_len, tq, kc_blk, has_latent, n_ctx, seq):
    m = qsrc.shape[0]
    tiles = q_len // tq
    kern = functools.partial(_attn_kernel, has_latent=has_latent)
    return pl.pallas_call(
        kern,
        grid=(n_batch, tiles),
        in_specs=[pl.BlockSpec((tq, ATTN_W), lambda b, i: (b * tiles + i, 0)),
                  pl.BlockSpec((n_ctx, KV_W), lambda b, i: (b, kc_blk)),
                  pl.BlockSpec((n_ctx, KV_W), lambda b, i: (b, kc_blk + 1)),
                  pl.BlockSpec((seq, KV_W), lambda b, i: (b, ATTN_W // KV_W)),
                  pl.BlockSpec((seq, KV_W), lambda b, i: (b, ATTN_W // KV_W + 1)),
                  pl.BlockSpec((1, ATTN_W), lambda b, i: (0, 0))],
        out_specs=pl.BlockSpec((tq, ATTN_W), lambda b, i: (b * tiles + i, 0)),
        out_shape=jax.ShapeDtypeStruct((m, ATTN_W), BF16),
        scratch_shapes=[pltpu.VMEM((tq, ATTN_W), F32)],
        compiler_params=_cparams(2),
        name="attention",
    )(qsrc, csrc, csrc, lsrc, lsrc, gg)


MIX_CHUNK = 64
SCONV_PAD = 8
CONF_PAD = 16


def _mix_kernel(sb_ref, sg_ref, sx_ref, ga_ref, gb_ref, sw_ref, dw_ref, db_ref, lng_ref, lnb_ref,
                pw_ref, pb_ref, ggs_ref, ggc_ref, o_ref, p1_ref, p2_ref, a_ref):
    seq = sb_ref.shape[0]
    ch = MIX_CHUNK
    n_chunks = seq // ch
    hs, hc = SCONV_PAD, CONF_PAD
    p1_ref[1, 0:hs, :] = jnp.zeros((hs, SCONV_W), F32)
    p1_ref[n_chunks, ch + hs:ch + 2 * hs, :] = jnp.zeros((hs, SCONV_W), F32)
    p2_ref[1, 0:hc, :] = jnp.zeros((hc, CONF_W), F32)
    p2_ref[n_chunks, ch + hc:ch + 2 * hc, :] = jnp.zeros((hc, CONF_W), F32)

    def fill(c, carry):
        r = pl.multiple_of(c * ch, ch)
        rows = pl.ds(r, ch)
        v = sg_ref[rows, :].astype(F32) * sx_ref[rows, :].astype(F32)
        p1_ref[c + 1, hs:hs + ch, :] = v
        p1_ref[c, ch + hs:ch + 2 * hs, :] = v[0:hs]
        p1_ref[c + 2, 0:hs, :] = v[ch - hs:ch]
        u = ga_ref[rows, :].astype(F32) * _sigmoid(gb_ref[rows, :].astype(F32))
        p2_ref[c + 1, hc:hc + ch, :] = u
        p2_ref[c, ch + hc:ch + 2 * hc, :] = u[0:hc]
        p2_ref[c + 2, 0:hc, :] = u[ch - hc:ch]
        return carry

    lax.fori_loop(0, n_chunks, fill, 0)

    def conv(c, carry):
        r = pl.multiple_of(c * ch, ch)
        rows = pl.ds(r, ch)
        o1 = hs - SCONV_K // 2
        acc = sw_ref[0:1, :] * p1_ref[c + 1, o1:o1 + ch, :]
        for k in range(1, SCONV_K):
            acc = acc + sw_ref[k:k + 1, :] * p1_ref[c + 1, o1 + k:o1 + k + ch, :]
        ys = sb_ref[rows, :].astype(F32) * acc
        ms = jnp.mean(ys * ys, axis=-1, keepdims=True)
        o_ref[rows, 0:SCONV_W] = (ys * lax.rsqrt(ms + EPS) * ggs_ref[...]).astype(o_ref.dtype)

        o2 = hc - CONF_K // 2
        u = db_ref[...] + dw_ref[0:1, :] * p2_ref[c + 1, o2:o2 + ch, :]
        for k in range(1, CONF_K):
            u = u + dw_ref[k:k + 1, :] * p2_ref[c + 1, o2 + k:o2 + k + ch, :]
        mu = jnp.mean(u, axis=-1, keepdims=True)
        uc = u - mu
        var = jnp.mean(uc * uc, axis=-1, keepdims=True)
        v = uc * lax.rsqrt(var + EPS) * lng_ref[...] + lnb_ref[...]
        a_ref[rows, :] = (v * _sigmoid(v)).astype(a_ref.dtype)
        return carry

    lax.fori_loop(0, n_chunks, conv, 0)

    yc = jnp.dot(a_ref[...], pw_ref[...], preferred_element_type=F32) + pb_ref[...]
    ms = jnp.mean(yc * yc, axis=-1, keepdims=True)
    o_ref[:, SCONV_W:] = (yc * lax.rsqrt(ms + EPS) * ggc_ref[...]).astype(o_ref.dtype)


def _local_mixers(src, sconv_w, conf_dw, conf_db, ln_g, ln_b, pw, pb, ggs, ggc, *, n_seq, seq):
    w = SCONV_W
    col = lambda k: pl.BlockSpec((seq, w), lambda b: (b, k))
    vec = lambda n: pl.BlockSpec((1, n), lambda b: (0, 0))
    return pl.pallas_call(
        _mix_kernel,
        grid=(n_seq,),
        in_specs=[col(3), col(4), col(5), col(6), col(7),
                  pl.BlockSpec((SCONV_K, w), lambda b: (0, 0)),
                  pl.BlockSpec((CONF_K, w), lambda b: (0, 0)),
                  vec(w), vec(w), vec(w),
                  pl.BlockSpec((w, w), lambda b: (0, 0)),
                  vec(w), vec(w), vec(w)],
        out_specs=pl.BlockSpec((seq, 2 * w), lambda b: (b, 0)),
        out_shape=jax.ShapeDtypeStruct((src.shape[0], 2 * w), BF16),
        scratch_shapes=[pltpu.VMEM((seq // MIX_CHUNK + 2, MIX_CHUNK + 2 * SCONV_PAD, w), F32),
                        pltpu.VMEM((seq // MIX_CHUNK + 2, MIX_CHUNK + 2 * CONF_PAD, w), F32),
                        pltpu.VMEM((seq, w), BF16)],
        compiler_params=_cparams(1),
        name="local_mixers",
    )(src, src, src, src, src, sconv_w, conf_dw, conf_db, ln_g, ln_b, pw, pb, ggs, ggc)


def _out_kernel(a_ref, m_ref, wa_ref, wm_ref, x_ref, gate_ref, o_ref):
    acc = (jnp.dot(a_ref[...], wa_ref[...], preferred_element_type=F32)
           + jnp.dot(m_ref[...], wm_ref[...], preferred_element_type=F32))
    o_ref[...] = x_ref[...] + gate_ref[0] * acc


def _out_proj(att, mix, w, x2, mod3, mod_row, *, tm, tn):
    m, d = x2.shape
    half = att.shape[1]
    nt = d // tn
    return pl.pallas_call(
        _out_kernel,
        grid=(m // tm, nt),
        in_specs=[pl.BlockSpec((tm, half), lambda i, j: (i, 0)),
                  pl.BlockSpec((tm, half), lambda i, j: (i, 0)),
                  pl.BlockSpec((half, tn), lambda i, j: (0, j)),
                  pl.BlockSpec((half, tn), lambda i, j: (1, j)),
                  pl.BlockSpec((tm, tn), lambda i, j: (i, j)),
                  pl.BlockSpec((1, 1, tn), lambda i, j: (mod_row(i), 0, 2 * nt + j))],
        out_specs=pl.BlockSpec((tm, tn), lambda i, j: (i, j)),
        out_shape=jax.ShapeDtypeStruct((m, d), F32),
        compiler_params=_cparams(2),
        name="out_proj",
    )(att, mix, w, w, x2, mod3)


def _ffn_kernel(x_ref, shift_ref, scale_ref, gate_ref, g_ref, wg_ref, wu_ref, wd_ref, o_ref, h_ref):
    j = pl.program_id(1)

    @pl.when(j == 0)
    def _():
        _norm_modulate_rows(x_ref, h_ref, g_ref[...], shift_ref[0], scale_ref[0], 256)
        o_ref[...] = jnp.zeros(o_ref.shape, o_ref.dtype)

    h = h_ref[...]
    gv = jnp.dot(h, wg_ref[...], preferred_element_type=F32)
    uv = jnp.dot(h, wu_ref[...], preferred_element_type=F32)
    a = (gv * _sigmoid(gv) * uv).astype(BF16)
    o_ref[...] += jnp.dot(a, wd_ref[...], preferred_element_type=F32)

    @pl.when(j == pl.num_programs(1) - 1)
    def _():
        o_ref[...] = x_ref[...] + gate_ref[0] * o_ref[...]


def _dense_ffn(x2, mod3, mod_row, g, wg, wu, wd, *, tm, tf):
    m, d = x2.shape
    f = wg.shape[1]
    modspec = lambda k: pl.BlockSpec((1, 1, d), lambda i, j: (mod_row(i), 0, k))
    return pl.pallas_call(
        _ffn_kernel,
        grid=(m // tm, f // tf),
        in_specs=[pl.BlockSpec((tm, d), lambda i, j: (i, 0)),
                  modspec(3), modspec(4), modspec(5),
                  pl.BlockSpec((1, d), lambda i, j: (0, 0)),
                  pl.BlockSpec((d, tf), lambda i, j: (0, j)),
                  pl.BlockSpec((d, tf), lambda i, j: (0, j)),
                  pl.BlockSpec((tf, d), lambda i, j: (j, 0))],
        out_specs=pl.BlockSpec((tm, d), lambda i, j: (i, 0)),
        out_shape=jax.ShapeDtypeStruct((m, d), F32),
        scratch_shapes=[pltpu.VMEM((tm, d), BF16)],
        compiler_params=_cparams(2),
        name="dense_ffn",
    )(x2, mod3, mod3, mod3, g, wg, wu, wd)


ROUTER_ROWS = 16


def _router_kernel(x_ref, shift_ref, scale_ref, g_ref, rw_ref, t_ref, idx_ref, gate_ref, cnt_ref,
                   tri_ref, base_ref):
    tm = x_ref.shape[0]

    @pl.when(pl.program_id(0) == 0)
    def _():
        r = lax.broadcasted_iota(jnp.int32, (tm, tm), 0)
        c = lax.broadcasted_iota(jnp.int32, (tm, tm), 1)
        tri_ref[...] = jnp.where(r < c, 1.0, 0.0).astype(BF16)
        base_ref[...] = jnp.zeros(base_ref.shape, F32)

    _norm_modulate_rows(x_ref, t_ref, g_ref[...], shift_ref[0], scale_ref[0], 256)

    t = t_ref[...]
    t_hi = t.astype(BF16)
    t_lo = (t - t_hi.astype(F32)).astype(BF16)
    w = rw_ref[...]
    w_hi = w.astype(BF16)
    w_lo = (w - w_hi.astype(F32)).astype(BF16)
    dn = (((1,), (1,)), ((), ()))
    logits = (lax.dot_general(w_hi, t_hi, dn, preferred_element_type=F32)
              + lax.dot_general(w_lo, t_hi, dn, preferred_element_type=F32)
              + lax.dot_general(w_hi, t_lo, dn, preferred_element_type=F32))

    e = lax.broadcasted_iota(jnp.int32, (ROUTER_ROWS, tm), 0).astype(F32)
    neg = jnp.float32(-jnp.inf)
    lg = jnp.where(e < N_EXPERTS, logits, neg)
    m1 = jnp.max(lg, axis=0, keepdims=True)
    i1 = jnp.min(jnp.where(lg == m1, e, float(ROUTER_ROWS)), axis=0, keepdims=True)
    lg2 = jnp.where(e == i1, neg, lg)
    m2 = jnp.max(lg2, axis=0, keepdims=True)
    i2 = jnp.min(jnp.where(lg2 == m2, e, float(ROUTER_ROWS)), axis=0, keepdims=True)
    ex = jnp.exp(m2 - m1)
    den = 1.0 + ex
    gate_ref[0:1, :] = 1.0 / den
    gate_ref[1:2, :] = ex / den

    hit1 = e == i1
    hit2 = e == i2
    onehot = jnp.where(hit1 | hit2, 1.0, 0.0)
    prefix = jnp.dot(onehot.astype(BF16), tri_ref[...], preferred_element_type=F32) + base_ref[:, 0:1]
    r1 = jnp.sum(jnp.where(hit1, prefix, 0.0), axis=0, keepdims=True)
    r2 = jnp.sum(jnp.where(hit2, prefix, 0.0), axis=0, keepdims=True)
    idx_ref[0:1, :] = i1.astype(jnp.int32)
    idx_ref[1:2, :] = i2.astype(jnp.int32)
    idx_ref[2:3, :] = r1.astype(jnp.int32)
    idx_ref[3:4, :] = r2.astype(jnp.int32)
    base_ref[...] = base_ref[...] + jnp.sum(onehot, axis=1, keepdims=True)
    cnt_ref[...] = base_ref[...].astype(jnp.int32)


def _router(x2, mod3, mod_row, g, rw16, *, tm):
    m, d = x2.shape
    modspec = lambda k: pl.BlockSpec((1, 1, d), lambda i: (mod_row(i), 0, k))
    return pl.pallas_call(
        _router_kernel,
        grid=(m // tm,),
        in_specs=[pl.BlockSpec((tm, d), lambda i: (i, 0)),
                  modspec(3), modspec(4),
                  pl.BlockSpec((1, d), lambda i: (0, 0)),
                  pl.BlockSpec((ROUTER_ROWS, d), lambda i: (0, 0))],
        out_specs=[pl.BlockSpec((tm, d), lambda i: (i, 0)),
                   pl.BlockSpec((4, tm), lambda i: (0, i)),
                   pl.BlockSpec((2, tm), lambda i: (0, i)),
                   pl.BlockSpec((ROUTER_ROWS, 128), lambda i: (0, 0))],
        out_shape=[jax.ShapeDtypeStruct((m, d), F32),
                   jax.ShapeDtypeStruct((4, m), jnp.int32),
                   jax.ShapeDtypeStruct((2, m), F32),
                   jax.ShapeDtypeStruct((ROUTER_ROWS, 128), jnp.int32)],
        scratch_shapes=[pltpu.VMEM((tm, tm), BF16), pltpu.VMEM((ROUTER_ROWS, 128), F32)],
        compiler_params=_cparams(1),
        name="router",
    )(x2, mod3, mod3, g, rw16)


def _row_copy(src_ref, src_row, dst_ref, dst_row, sem):
    return pltpu.make_async_copy(src_ref.at[pl.ds(src_row, 1), :], dst_ref.at[pl.ds(dst_row, 1), :], sem)


def _scatter_kernel(dest_ref, t_ref, buf_in_ref, buf_ref, sem, *, n_tok):
    del buf_in_ref
    tm = t_ref.shape[0]
    base = pl.program_id(0) * tm

    def start(r, carry):
        for k in range(2):
            _row_copy(t_ref, r, buf_ref, dest_ref[k * n_tok + base + r], sem).start()
        return carry

    lax.fori_loop(0, tm, start, 0)

    def wait(r, carry):
        for k in range(2):
            _row_copy(t_ref, 0, buf_ref, 0, sem).wait()
        return carry

    lax.fori_loop(0, tm, wait, 0)


def _scatter_rows(dest, t, buf, *, tm):
    m, d = t.shape
    grid_spec = pltpu.PrefetchScalarGridSpec(
        num_scalar_prefetch=1,
        grid=(m // tm,),
        in_specs=[pl.BlockSpec((tm, d), lambda i, dest: (i, 0)),
                  pl.BlockSpec(memory_space=pl.ANY)],
        out_specs=pl.BlockSpec(memory_space=pl.ANY),
        scratch_shapes=[pltpu.SemaphoreType.DMA(())],
    )
    return pl.pallas_call(
        functools.partial(_scatter_kernel, n_tok=m),
        grid_spec=grid_spec,
        out_shape=jax.ShapeDtypeStruct(buf.shape, buf.dtype),
        input_output_aliases={2: 0},
        compiler_params=_cparams(1),
        name="moe_scatter",
    )(dest, t, buf)


def _expert_kernel(te_ref, tv_ref, x_ref, wg_ref, wu_ref, wd_ref, o_ref, xb_ref):
    i = pl.program_id(0)
    j = pl.program_id(1)
    valid = tv_ref[i] == 1

    @pl.when(j == 0)
    def _():
        o_ref[...] = jnp.zeros(o_ref.shape, o_ref.dtype)

    @pl.when(valid)
    def _():
        @pl.when(j == 0)
        def _():
            xb_ref[...] = x_ref[...].astype(BF16)

        xb = xb_ref[...]
        gv = jnp.dot(xb, wg_ref[...], preferred_element_type=F32)
        uv = jnp.dot(xb, wu_ref[...], preferred_element_type=F32)
        a = (gv * _sigmoid(gv) * uv).astype(BF16)
        o_ref[...] += jnp.dot(a, wd_ref[...], preferred_element_type=F32)


def _experts(tile_e, tile_v, tile_x, buf, wg, wu, wd, *, tf):
    rows, d = buf.shape
    f = wg.shape[2]
    nf = f // tf
    fcol = lambda i, j, tv: jnp.where(tv[i] == 1, j, nf - 1)
    grid_spec = pltpu.PrefetchScalarGridSpec(
        num_scalar_prefetch=3,
        grid=(rows // EXPERT_TM, nf),
        in_specs=[pl.BlockSpec((EXPERT_TM, d), lambda i, j, te, tv, tx: (tx[i], 0)),
                  pl.BlockSpec((None, d, tf), lambda i, j, te, tv, tx: (te[i], 0, fcol(i, j, tv))),
                  pl.BlockSpec((None, d, tf), lambda i, j, te, tv, tx: (te[i], 0, fcol(i, j, tv))),
                  pl.BlockSpec((None, tf, d), lambda i, j, te, tv, tx: (te[i], fcol(i, j, tv), 0))],
        out_specs=pl.BlockSpec((EXPERT_TM, d), lambda i, j, te, tv, tx: (i, 0)),
        scratch_shapes=[pltpu.VMEM((EXPERT_TM, d), BF16)],
    )

    def kern(te_ref, tv_ref, tx_ref, x_ref, wg_ref, wu_ref, wd_ref, o_ref, xb_ref):
        del tx_ref
        _expert_kernel(te_ref, tv_ref, x_ref, wg_ref, wu_ref, wd_ref, o_ref, xb_ref)

    return pl.pallas_call(
        kern,
        grid_spec=grid_spec,
        out_shape=jax.ShapeDtypeStruct((rows, d), F32),
        compiler_params=_cparams(2),
        name="moe_experts",
    )(tile_e, tile_v, tile_x, buf, wg, wu, wd)


def _combine_kernel(dest_ref, x_ref, gate5_ref, gates_ref, gf_ref, ybuf_ref, o_ref, rows_ref, sem, *, n_tok):
    tm = x_ref.shape[0]
    base = pl.program_id(0) * tm

    def start(r, carry):
        for k in range(2):
            _row_copy(ybuf_ref, dest_ref[k * n_tok + base + r], rows_ref.at[k], r, sem).start()
        return carry

    lax.fori_loop(0, tm, start, 0)

    def wait(r, carry):
        for k in range(2):
            _row_copy(ybuf_ref, 0, rows_ref.at[k], 0, sem).wait()
        return carry

    lax.fori_loop(0, tm, wait, 0)

    eye = (lax.broadcasted_iota(jnp.int32, (tm, tm), 0) == lax.broadcasted_iota(jnp.int32, (tm, tm), 1))
    g0 = jnp.sum(jnp.where(eye, gates_ref[0:1, :], 0.0), axis=1, keepdims=True)
    g1 = jnp.sum(jnp.where(eye, gates_ref[1:2, :], 0.0), axis=1, keepdims=True)
    y = g0 * rows_ref[0] + g1 * rows_ref[1]
    xn = x_ref[...] + gate5_ref[0] * y
    ms = jnp.mean(xn * xn, axis=-1, keepdims=True)
    o_ref[...] = xn * lax.rsqrt(ms + EPS) * gf_ref[...]


def _combine(dest, x2, mod3, mod_row, gates, g_final, ybuf, *, tm):
    m, d = x2.shape
    grid_spec = pltpu.PrefetchScalarGridSpec(
        num_scalar_prefetch=1,
        grid=(m // tm,),
        in_specs=[pl.BlockSpec((tm, d), lambda i, dest: (i, 0)),
                  pl.BlockSpec((1, 1, d), lambda i, dest: (mod_row(i), 0, 5)),
                  pl.BlockSpec((2, tm), lambda i, dest: (0, i)),
                  pl.BlockSpec((1, d), lambda i, dest: (0, 0)),
                  pl.BlockSpec(memory_space=pl.ANY)],
        out_specs=pl.BlockSpec((tm, d), lambda i, dest: (i, 0)),
        scratch_shapes=[pltpu.VMEM((2, tm, d), F32), pltpu.SemaphoreType.DMA(())],
    )
    return pl.pallas_call(
        functools.partial(_combine_kernel, n_tok=m),
        grid_spec=grid_spec,
        out_shape=jax.ShapeDtypeStruct((m, d), F32),
        compiler_params=_cparams(1),
        name="moe_combine",
    )(dest, x2, mod3, gates, g_final, ybuf)


def _rope_tables(seq):
    rows = seq // GRID_W
    row = jnp.repeat(jnp.arange(rows), GRID_W).astype(F32)
    col = jnp.tile(jnp.arange(GRID_W), rows).astype(F32)
    axis_dim = HEAD_DIM // 2
    inv_freq = ROPE_THETA ** (-jnp.arange(0, axis_dim, 2, dtype=F32) / axis_dim)
    ang_r = row[:, None] * inv_freq
    ang_c = col[:, None] * inv_freq
    cr, sr, cc, sc = jnp.cos(ang_r), jnp.sin(ang_r), jnp.cos(ang_c), jnp.sin(ang_c)
    return (jnp.concatenate([cr, cr, cc, cc], axis=-1),
            jnp.concatenate([-sr, sr, -sc, sc], axis=-1))


def kernel(x, c, ctx, c_ctx, w_ada, b_ada, g_mix, w_in, g_q, g_k, sconv_w, conf_dw, conf_db, conf_ln_g,
           conf_ln_b, conf_pw, conf_pb, g_group, w_o, g_ffn, dense_wg, dense_wu, dense_wd, router_w,
           moe_wg, moe_wu, moe_wd, g_final):
    b, s, d = x.shape
    n_ctx = ctx.shape[1]
    depth = w_ada.shape[0]
    assert b + 1 <= MOD_ROWS and depth == 2
    m_lat, m_ctx = b * s, b * n_ctx
    ctx_row = b

    cin = jnp.concatenate([c, c_ctx[None, :], jnp.zeros((MOD_ROWS - b - 1, d), F32)], axis=0)
    mod = _ada(cin, w_ada, b_ada)
    cos_t, sin_t = _rope_tables(s)
    row2 = lambda v: v.reshape(1, -1)

    lat_tm = 512
    lat_row_for = lambda tm: (lambda i: i // (s // tm))
    lat_row = lat_row_for(lat_tm)
    ctx_tm = 512
    ctx_mod_row = lambda i: ctx_row

    x2 = x.reshape(m_lat, d)
    xc2 = ctx.reshape(m_ctx, d)
    out = None
    for l in range(depth):
        last = l == depth - 1
        mod3 = mod[l].reshape(MOD_ROWS, 1, 6 * d)
        w_in_b = w_in[l].astype(BF16)
        w_o_b = w_o[l].astype(BF16)
        pw_b = conf_pw[l].astype(BF16)
        gg = g_group[l]
        gga, ggs, ggc = row2(gg[:ATTN_W]), row2(gg[ATTN_W:ATTN_W + SCONV_W]), row2(gg[ATTN_W + SCONV_W:])
        mixer_args = (sconv_w[l], conf_dw[l], row2(conf_db[l]), row2(conf_ln_g[l]), row2(conf_ln_b[l]),
                      pw_b, row2(conf_pb[l]), ggs, ggc)
        in_args = (row2(g_mix[l]), w_in_b, row2(g_q[l]), row2(g_k[l]), cos_t, sin_t)

        p_lat = _in_proj(x2, mod3, lat_row, *in_args, tm=lat_tm, rope=True, j0=0, nj=8, seq=s)
        if last:
            p_ctx = _in_proj(xc2, mod3, ctx_mod_row, *in_args, tm=ctx_tm, rope=False, j0=2, nj=1, seq=ctx_tm)
            kc_blk = 0
        else:
            p_ctx = _in_proj(xc2, mod3, ctx_mod_row, *in_args, tm=ctx_tm, rope=False, j0=0, nj=8, seq=ctx_tm)
            kc_blk = ATTN_W // KV_W
        att = _attention(p_lat, p_ctx, p_lat, gga, n_batch=b, q_len=s, tq=512, kc_blk=kc_blk,
                         has_latent=True, n_ctx=n_ctx, seq=s)
        mix = _local_mixers(p_lat, *mixer_args, n_seq=b, seq=s)
        x2 = _out_proj(att, mix, w_o_b, x2, mod3, lat_row, tm=lat_tm, tn=1024)
        if not last:
            att_c = _attention(p_ctx, p_ctx, p_ctx, gga, n_batch=b, q_len=n_ctx, tq=n_ctx, kc_blk=kc_blk,
                               has_latent=False, n_ctx=n_ctx, seq=n_ctx)
            mix_c = _local_mixers(p_ctx, *mixer_args, n_seq=b, seq=n_ctx)
            xc2 = _out_proj(att_c, mix_c, w_o_b, xc2, mod3, ctx_mod_row, tm=ctx_tm, tn=1024)

        if l % 2 == 0:
            wg_b, wu_b, wd_b = (dense_wg[l // 2].astype(BF16), dense_wu[l // 2].astype(BF16),
                                dense_wd[l // 2].astype(BF16))
            ffn = functools.partial(_dense_ffn, g=row2(g_ffn[l]), wg=wg_b, wu=wu_b, wd=wd_b, tf=512)
            x2 = ffn(x2, mod3, lat_row, tm=lat_tm)
            if not last:
                xc2 = ffn(xc2, mod3, ctx_mod_row, tm=ctx_tm)
        else:
            assert last, "routed layer is implemented for the final layer (latent tokens only)"
            rw16 = jnp.zeros((ROUTER_ROWS, d), F32).at[:N_EXPERTS].set(router_w[l // 2].T)
            t, idx, gates, cnt = _router(x2, mod3, lat_row, row2(g_ffn[l]), rw16, tm=lat_tm)
            counts = cnt[:N_EXPERTS, 0]
            padded = (counts + EXPERT_TM - 1) // EXPERT_TM * EXPERT_TM
            pends = jnp.cumsum(padded)
            pstarts = pends - padded
            dest = (pstarts[idx[0:2]] + idx[2:4]).reshape(-1).astype(jnp.int32)
            n_tiles = (2 * m_lat) // EXPERT_TM + N_EXPERTS
            tile_start = jnp.arange(n_tiles, dtype=jnp.int32) * EXPERT_TM
            tile_v = (tile_start < pends[-1]).astype(jnp.int32)
            last_tile = pends[-1] // EXPERT_TM - 1
            tile_x = jnp.minimum(jnp.arange(n_tiles, dtype=jnp.int32), last_tile).astype(jnp.int32)
            tile_e = jnp.minimum(jnp.searchsorted(pends, tile_x * EXPERT_TM, side='right'),
                                 N_EXPERTS - 1).astype(jnp.int32)
            buf = jnp.zeros((n_tiles * EXPERT_TM, d), F32)
            buf = _scatter_rows(dest, t, buf, tm=lat_tm)
            ybuf = _experts(tile_e, tile_v, tile_x, buf, moe_wg[l // 2].astype(BF16),
                            moe_wu[l // 2].astype(BF16), moe_wd[l // 2].astype(BF16), tf=512)
            out = _combine(dest, x2, mod3, lat_row_for(256), gates, row2(g_final), ybuf, tm=256)
    return out.reshape(b, s, d)
```

```python
import functools

import jax
import jax.numpy as jnp
from jax import lax
from jax.experimental import pallas as pl
from jax.experimental.pallas import tpu as pltpu

F32 = jnp.float32
BF16 = jnp.bfloat16

GRID_W = 64
HEAD_DIM = 128
N_Q_HEADS = 8
N_KV_HEADS = 2
GQA_GROUP = N_Q_HEADS // N_KV_HEADS
ATTN_W = N_Q_HEADS * HEAD_DIM
KV_W = N_KV_HEADS * HEAD_DIM
ROPE_THETA = 10000.0
SCONV_W = 512
SCONV_K = 3
CONF_W = 512
CONF_K = 31
N_EXPERTS = 8
EPS = 1e-6
LOG2E = 1.4426950408889634

V7X_VMEM_LIMIT_BYTES = 56 * 1024 * 1024
MOD_ROWS = 16
IN_TN = 512
EXPERT_TM = 512


def _cparams(n_axes):
    return pltpu.CompilerParams(dimension_semantics=("arbitrary",) * n_axes,
                                vmem_limit_bytes=V7X_VMEM_LIMIT_BYTES)


def _sigmoid(x):
    return 1.0 / (1.0 + jnp.exp(-x))


NORM_CHUNK = 16


def _norm_modulate_rows(x_ref, h_ref, g, shift, scale):
    rows = x_ref.shape[0]
    gain = g * (1.0 + scale)

    def body(c, carry):
        r = pl.multiple_of(c * NORM_CHUNK, NORM_CHUNK)
        x = x_ref[pl.ds(r, NORM_CHUNK), :]
        inv = lax.rsqrt(jnp.mean(x * x, axis=-1, keepdims=True) + EPS)
        h_ref[pl.ds(r, NORM_CHUNK), :] = (x * inv * gain + shift).astype(h_ref.dtype)
        return carry

    lax.fori_loop(0, rows // NORM_CHUNK, body, 0)


def _ada_kernel(c_ref, w_ref, b_ref, o_ref):
    a = c_ref[...]
    a = (a * _sigmoid(a)).astype(BF16)
    o_ref[0] = jnp.dot(a, w_ref[0].astype(BF16), preferred_element_type=F32) + b_ref[0]


def _ada(cin, w_ada, b_ada):
    depth, d, n = w_ada.shape
    tn = 1536
    return pl.pallas_call(
        _ada_kernel,
        grid=(depth, n // tn),
        in_specs=[pl.BlockSpec((MOD_ROWS, d), lambda l, j: (0, 0)),
                  pl.BlockSpec((1, d, tn), lambda l, j: (l, 0, j)),
                  pl.BlockSpec((1, 1, tn), lambda l, j: (l, 0, j))],
        out_specs=pl.BlockSpec((1, MOD_ROWS, tn), lambda l, j: (l, 0, j)),
        out_shape=jax.ShapeDtypeStruct((depth, MOD_ROWS, n), F32),
        compiler_params=_cparams(2),
        name="ada",
    )(cin, w_ada, b_ada.reshape(depth, 1, n))


def _head_norm(seg, gain):
    ms = jnp.mean(seg * seg, axis=-1, keepdims=True)
    return seg * lax.rsqrt(ms + EPS) * gain


def _rope(n, cos_t, sin_t):
    lane = lax.broadcasted_iota(jnp.int32, n.shape, 1)
    fwd = pltpu.roll(n, 32, 1)
    bwd = pltpu.roll(n, 96, 1)
    partner = jnp.where((lane // 32) % 2 == 0, bwd, fwd)
    return n * cos_t + partner * sin_t


def _in_kernel(x_ref, shift_ref, scale_ref, g_ref, w_ref, gq_ref, gk_ref, cos_ref, sin_ref,
               o_ref, h_ref, *, rope, j0, nj, q_scale):
    _norm_modulate_rows(x_ref, h_ref, g_ref[...], shift_ref[0], scale_ref[0])
    h = h_ref[...]

    def head(seg, gain, scale):
        n = _head_norm(seg, gain)
        if rope:
            n = _rope(n, cos_ref[...], sin_ref[...])
        if scale != 1.0:
            n = n * scale
        return n

    for t in range(nj):
        j = t + j0
        c0 = t * IN_TN
        acc = jnp.dot(h, w_ref[:, c0:c0 + IN_TN], preferred_element_type=F32)
        if j < 2:
            for hd in range(IN_TN // HEAD_DIM):
                sl = slice(hd * HEAD_DIM, (hd + 1) * HEAD_DIM)
                osl = slice(c0 + hd * HEAD_DIM, c0 + (hd + 1) * HEAD_DIM)
                o_ref[:, osl] = head(acc[:, sl], gq_ref[...], q_scale).astype(o_ref.dtype)
        elif j == 2:
            for hd in range(N_KV_HEADS):
                sl = slice(hd * HEAD_DIM, (hd + 1) * HEAD_DIM)
                osl = slice(c0 + hd * HEAD_DIM, c0 + (hd + 1) * HEAD_DIM)
                o_ref[:, osl] = head(acc[:, sl], gk_ref[...], 1.0).astype(o_ref.dtype)
            o_ref[:, c0 + KV_W:c0 + IN_TN] = acc[:, KV_W:].astype(o_ref.dtype)
        else:
            o_ref[:, c0:c0 + IN_TN] = acc.astype(o_ref.dtype)


def _const_spec(shape, idx):
    return pl.BlockSpec(shape, lambda i: idx, pipeline_mode=pl.Buffered(1))


def _in_proj(x2, mod3, mod_row, g, w, gq, gk, cos_t, sin_t, *, tm, rope, j0, nj, seq):
    m, d = x2.shape
    tiles_per_seq = seq // tm
    ncols = nj * IN_TN
    assert j0 % nj == 0
    q_scale = HEAD_DIM ** -0.5 * LOG2E
    kern = functools.partial(_in_kernel, rope=rope, j0=j0, nj=nj, q_scale=q_scale)
    return pl.pallas_call(
        kern,
        grid=(m // tm,),
        in_specs=[pl.BlockSpec((tm, d), lambda i: (i, 0)),
                  pl.BlockSpec((1, 1, d), lambda i: (mod_row(i), 0, 0)),
                  pl.BlockSpec((1, 1, d), lambda i: (mod_row(i), 0, 1)),
                  _const_spec((1, d), (0, 0)),
                  _const_spec((d, ncols), (0, j0 // nj)),
                  _const_spec((1, HEAD_DIM), (0, 0)),
                  _const_spec((1, HEAD_DIM), (0, 0)),
                  pl.BlockSpec((tm, HEAD_DIM), lambda i: (i % tiles_per_seq, 0)),
                  pl.BlockSpec((tm, HEAD_DIM), lambda i: (i % tiles_per_seq, 0))],
        out_specs=pl.BlockSpec((tm, ncols), lambda i: (i, 0)),
        out_shape=jax.ShapeDtypeStruct((m, ncols), BF16),
        scratch_shapes=[pltpu.VMEM((tm, d), BF16)],
        compiler_params=_cparams(1),
        name="in_proj",
    )(x2, mod3, mod3, g, w, gq, gk, cos_t, sin_t)


def _attn_kernel(q_ref, kc_ref, vc_ref, kl_ref, vl_ref, gg_ref, o_ref, acc_ref, *, has_latent):
    tq = q_ref.shape[0]
    dn = (((1,), (1,)), ((), ()))
    ssq = jnp.zeros((tq, 1), F32)
    for hd in range(N_Q_HEADS):
        kv = hd // GQA_GROUP
        ksl = slice(kv * HEAD_DIM, (kv + 1) * HEAD_DIM)
        q = q_ref[:, hd * HEAD_DIM:(hd + 1) * HEAD_DIM]
        s_c = lax.dot_general(q, kc_ref[:, ksl], dn, preferred_element_type=F32)
        mx = jnp.max(s_c, axis=-1, keepdims=True)
        if has_latent:
            s_l = lax.dot_general(q, kl_ref[:, ksl], dn, preferred_element_type=F32)
            mx = jnp.maximum(mx, jnp.max(s_l, axis=-1, keepdims=True))
        p_c = jnp.exp2(s_c - mx)
        den = jnp.sum(p_c, axis=-1, keepdims=True)
        o = jnp.dot(p_c.astype(BF16), vc_ref[:, ksl], preferred_element_type=F32)
        if has_latent:
            p_l = jnp.exp2(s_l - mx)
            den = den + jnp.sum(p_l, axis=-1, keepdims=True)
            o = o + jnp.dot(p_l.astype(BF16), vl_ref[:, ksl], preferred_element_type=F32)
        o = o * (1.0 / den)
        ssq = ssq + jnp.sum(o * o, axis=-1, keepdims=True)
        acc_ref[:, hd * HEAD_DIM:(hd + 1) * HEAD_DIM] = o
    inv = lax.rsqrt(ssq * (1.0 / ATTN_W) + EPS)
    o_ref[...] = (acc_ref[...] * inv * gg_ref[...]).astype(o_ref.dtype)


def _attention(qsrc, csrc, lsrc, gg, *, n_batch, q_len, tq, kc_blk, has_latent, n_ctx, seq):
    m = qsrc.shape[0]
    tiles = q_len // tq
    kern = functools.partial(_attn_kernel, has_latent=has_latent)
    return pl.pallas_call(
        kern,
        grid=(n_batch, tiles),
        in_specs=[pl.BlockSpec((tq, ATTN_W), lambda b, i: (b * tiles + i, 0)),
                  pl.BlockSpec((n_ctx, KV_W), lambda b, i: (b, kc_blk)),
                  pl.BlockSpec((n_ctx, KV_W), lambda b, i: (b, kc_blk + 1)),
                  pl.BlockSpec((seq, KV_W), lambda b, i: (b, ATTN_W // KV_W)),
                  pl.BlockSpec((seq, KV_W), lambda b, i: (b, ATTN_W // KV_W + 1)),
                  pl.BlockSpec((1, ATTN_W), lambda b, i: (0, 0))],
        out_specs=pl.BlockSpec((tq, ATTN_W), lambda b, i: (b * tiles + i, 0)),
        out_shape=jax.ShapeDtypeStruct((m, ATTN_W), BF16),
        scratch_shapes=[pltpu.VMEM((tq, ATTN_W), F32)],
        compiler_params=_cparams(2),
        name="attention",
    )(qsrc, csrc, csrc, lsrc, lsrc, gg)


MIX_CHUNK = 64
SCONV_PAD = 8
CONF_PAD = 16


LANES = 128
SUBLANES = 8
N_LANE_GROUPS = SCONV_W // LANES


def _window_conv(p_ref, slot, w_ref, group, first, n_taps, ch):
    lanes = slice(group * LANES, (group + 1) * LANES)
    win = p_ref[slot, :, lanes]
    n_rows = win.shape[0]
    acc = None
    for phase in range(SUBLANES):
        taps = [k for k in range(n_taps) if (first + k) % SUBLANES == phase]
        if not taps:
            continue
        shifted = win if phase == 0 else pltpu.roll(win, n_rows - phase, 0)
        for k in taps:
            base = first + k - phase
            term = w_ref[k:k + 1, lanes] * shifted[base:base + ch]
            acc = term if acc is None else acc + term
    return acc


def _mix_kernel(sb_ref, sg_ref, sx_ref, ga_ref, gb_ref, sw_ref, dw_ref, db_ref, lng_ref, lnb_ref,
                pw_ref, pb_ref, ggs_ref, ggc_ref, o_ref, p1_ref, p2_ref, a_ref):
    seq = sb_ref.shape[0]
    ch = MIX_CHUNK
    n_chunks = seq // ch
    hs, hc = SCONV_PAD, CONF_PAD
    p1_ref[1, 0:hs, :] = jnp.zeros((hs, SCONV_W), F32)
    p1_ref[n_chunks, ch + hs:ch + 2 * hs, :] = jnp.zeros((hs, SCONV_W), F32)
    p2_ref[1, 0:hc, :] = jnp.zeros((hc, CONF_W), F32)
    p2_ref[n_chunks, ch + hc:ch + 2 * hc, :] = jnp.zeros((hc, CONF_W), F32)

    def fill(c, carry):
        r = pl.multiple_of(c * ch, ch)
        rows = pl.ds(r, ch)
        v = sg_ref[rows, :].astype(F32) * sx_ref[rows, :].astype(F32)
        p1_ref[c + 1, hs:hs + ch, :] = v
        p1_ref[c, ch + hs:ch + 2 * hs, :] = v[0:hs]
        p1_ref[c + 2, 0:hs, :] = v[ch - hs:ch]
        u = ga_ref[rows, :].astype(F32) * _sigmoid(gb_ref[rows, :].astype(F32))
        p2_ref[c + 1, hc:hc + ch, :] = u
        p2_ref[c, ch + hc:ch + 2 * hc, :] = u[0:hc]
        p2_ref[c + 2, 0:hc, :] = u[ch - hc:ch]
        return carry

    lax.fori_loop(0, n_chunks, fill, 0)

    def conv(c, carry):
        r = pl.multiple_of(c * ch, ch)
        rows = pl.ds(r, ch)
        acc = jnp.concatenate(
            [_window_conv(p1_ref, c + 1, sw_ref, g, hs - SCONV_K // 2, SCONV_K, ch) for g in range(N_LANE_GROUPS)],
            axis=1)
        ys = sb_ref[rows, :].astype(F32) * acc
        ms = jnp.mean(ys * ys, axis=-1, keepdims=True)
        o_ref[rows, 0:SCONV_W] = (ys * lax.rsqrt(ms + EPS) * ggs_ref[...]).astype(o_ref.dtype)

        u = db_ref[...] + jnp.concatenate(
            [_window_conv(p2_ref, c + 1, dw_ref, g, hc - CONF_K // 2, CONF_K, ch) for g in range(N_LANE_GROUPS)],
            axis=1)
        mu = jnp.mean(u, axis=-1, keepdims=True)
        uc = u - mu
        var = jnp.mean(uc * uc, axis=-1, keepdims=True)
        v = uc * lax.rsqrt(var + EPS) * lng_ref[...] + lnb_ref[...]
        a_ref[rows, :] = (v * _sigmoid(v)).astype(a_ref.dtype)
        return carry

    lax.fori_loop(0, n_chunks, conv, 0)

    yc = jnp.dot(a_ref[...], pw_ref[...], preferred_element_type=F32) + pb_ref[...]
    ms = jnp.mean(yc * yc, axis=-1, keepdims=True)
    o_ref[:, SCONV_W:] = (yc * lax.rsqrt(ms + EPS) * ggc_ref[...]).astype(o_ref.dtype)


def _local_mixers(src, sconv_w, conf_dw, conf_db, ln_g, ln_b, pw, pb, ggs, ggc, *, n_seq, seq):
    w = SCONV_W
    col = lambda k: pl.BlockSpec((seq, w), lambda b: (b, k))
    vec = lambda n: pl.BlockSpec((1, n), lambda b: (0, 0))
    return pl.pallas_call(
        _mix_kernel,
        grid=(n_seq,),
        in_specs=[col(3), col(4), col(5), col(6), col(7),
                  pl.BlockSpec((SCONV_K, w), lambda b: (0, 0)),
                  pl.BlockSpec((CONF_K, w), lambda b: (0, 0)),
                  vec(w), vec(w), vec(w),
                  pl.BlockSpec((w, w), lambda b: (0, 0)),
                  vec(w), vec(w), vec(w)],
        out_specs=pl.BlockSpec((seq, 2 * w), lambda b: (b, 0)),
        out_shape=jax.ShapeDtypeStruct((src.shape[0], 2 * w), BF16),
        scratch_shapes=[pltpu.VMEM((seq // MIX_CHUNK + 2, MIX_CHUNK + 2 * SCONV_PAD, w), F32),
                        pltpu.VMEM((seq // MIX_CHUNK + 2, MIX_CHUNK + 2 * CONF_PAD, w), F32),
                        pltpu.VMEM((seq, w), BF16)],
        compiler_params=_cparams(1),
        name="local_mixers",
    )(src, src, src, src, src, sconv_w, conf_dw, conf_db, ln_g, ln_b, pw, pb, ggs, ggc)


OUT_TN = 512


def _out_kernel(a_ref, m_ref, wa_ref, wm_ref, x_ref, gate_ref, o_ref):
    a = a_ref[...]
    mx = m_ref[...]
    for t in range(o_ref.shape[1] // OUT_TN):
        sl = slice(t * OUT_TN, (t + 1) * OUT_TN)
        acc = (jnp.dot(a, wa_ref[:, sl], preferred_element_type=F32)
               + jnp.dot(mx, wm_ref[:, sl], preferred_element_type=F32))
        o_ref[:, sl] = x_ref[:, sl] + gate_ref[0, :, sl] * acc


def _out_proj(att, mix, w, x2, mod3, mod_row, *, tm):
    m, d = x2.shape
    half = att.shape[1]
    return pl.pallas_call(
        _out_kernel,
        grid=(m // tm,),
        in_specs=[pl.BlockSpec((tm, half), lambda i: (i, 0)),
                  pl.BlockSpec((tm, half), lambda i: (i, 0)),
                  _const_spec((half, d), (0, 0)),
                  _const_spec((half, d), (1, 0)),
                  pl.BlockSpec((tm, d), lambda i: (i, 0)),
                  pl.BlockSpec((1, 1, d), lambda i: (mod_row(i), 0, 2))],
        out_specs=pl.BlockSpec((tm, d), lambda i: (i, 0)),
        out_shape=jax.ShapeDtypeStruct((m, d), F32),
        compiler_params=_cparams(1),
        name="out_proj",
    )(att, mix, w, w, x2, mod3)


def _ffn_kernel(x_ref, shift_ref, scale_ref, gate_ref, g_ref, wg_ref, wu_ref, wd_ref, o_ref, h_ref):
    j = pl.program_id(1)

    @pl.when(j == 0)
    def _():
        _norm_modulate_rows(x_ref, h_ref, g_ref[...], shift_ref[0], scale_ref[0])
        o_ref[...] = jnp.zeros(o_ref.shape, o_ref.dtype)

    h = h_ref[...]
    gv = jnp.dot(h, wg_ref[...], preferred_element_type=F32)
    uv = jnp.dot(h, wu_ref[...], preferred_element_type=F32)
    a = (gv * _sigmoid(gv) * uv).astype(BF16)
    o_ref[...] += jnp.dot(a, wd_ref[...], preferred_element_type=F32)

    @pl.when(j == pl.num_programs(1) - 1)
    def _():
        o_ref[...] = x_ref[...] + gate_ref[0] * o_ref[...]


def _dense_ffn(x2, mod3, mod_row, g, wg, wu, wd, *, tm, tf):
    m, d = x2.shape
    f = wg.shape[1]
    modspec = lambda k: pl.BlockSpec((1, 1, d), lambda i, j: (mod_row(i), 0, k))
    return pl.pallas_call(
        _ffn_kernel,
        grid=(m // tm, f // tf),
        in_specs=[pl.BlockSpec((tm, d), lambda i, j: (i, 0)),
                  modspec(3), modspec(4), modspec(5),
                  pl.BlockSpec((1, d), lambda i, j: (0, 0)),
                  pl.BlockSpec((d, tf), lambda i, j: (0, j)),
                  pl.BlockSpec((d, tf), lambda i, j: (0, j)),
                  pl.BlockSpec((tf, d), lambda i, j: (j, 0))],
        out_specs=pl.BlockSpec((tm, d), lambda i, j: (i, 0)),
        out_shape=jax.ShapeDtypeStruct((m, d), F32),
        scratch_shapes=[pltpu.VMEM((tm, d), BF16)],
        compiler_params=_cparams(2),
        name="dense_ffn",
    )(x2, mod3, mod3, mod3, g, wg, wu, wd)


ROUTER_ROWS = 16


def _router_kernel(x_ref, shift_ref, scale_ref, g_ref, rw_ref, t_ref, idx_ref, gate_ref, cnt_ref,
                   tri_ref, base_ref):
    tm = x_ref.shape[0]

    @pl.when(pl.program_id(0) == 0)
    def _():
        r = lax.broadcasted_iota(jnp.int32, (tm, tm), 0)
        c = lax.broadcasted_iota(jnp.int32, (tm, tm), 1)
        tri_ref[...] = jnp.where(r < c, 1.0, 0.0).astype(BF16)
        base_ref[...] = jnp.zeros(base_ref.shape, F32)

    _norm_modulate_rows(x_ref, t_ref, g_ref[...], shift_ref[0], scale_ref[0])

    t = t_ref[...]
    t_hi = t.astype(BF16)
    t_lo = (t - t_hi.astype(F32)).astype(BF16)
    w = rw_ref[...]
    w_hi = w.astype(BF16)
    w_lo = (w - w_hi.astype(F32)).astype(BF16)
    dn = (((1,), (1,)), ((), ()))
    logits = (lax.dot_general(w_hi, t_hi, dn, preferred_element_type=F32)
              + lax.dot_general(w_lo, t_hi, dn, preferred_element_type=F32)
              + lax.dot_general(w_hi, t_lo, dn, preferred_element_type=F32))

    e = lax.broadcasted_iota(jnp.int32, (ROUTER_ROWS, tm), 0).astype(F32)
    neg = jnp.float32(-jnp.inf)
    lg = jnp.where(e < N_EXPERTS, logits, neg)
    m1 = jnp.max(lg, axis=0, keepdims=True)
    i1 = jnp.min(jnp.where(lg == m1, e, float(ROUTER_ROWS)), axis=0, keepdims=True)
    lg2 = jnp.where(e == i1, neg, lg)
    m2 = jnp.max(lg2, axis=0, keepdims=True)
    i2 = jnp.min(jnp.where(lg2 == m2, e, float(ROUTER_ROWS)), axis=0, keepdims=True)
    ex = jnp.exp(m2 - m1)
    den = 1.0 + ex
    gate_ref[0:1, :] = 1.0 / den
    gate_ref[1:2, :] = ex / den

    hit1 = e == i1
    hit2 = e == i2
    onehot = jnp.where(hit1 | hit2, 1.0, 0.0)
    prefix = jnp.dot(onehot.astype(BF16), tri_ref[...], preferred_element_type=F32) + base_ref[:, 0:1]
    r1 = jnp.sum(jnp.where(hit1, prefix, 0.0), axis=0, keepdims=True)
    r2 = jnp.sum(jnp.where(hit2, prefix, 0.0), axis=0, keepdims=True)
    idx_ref[0:1, :] = i1.astype(jnp.int32)
    idx_ref[1:2, :] = i2.astype(jnp.int32)
    idx_ref[2:3, :] = r1.astype(jnp.int32)
    idx_ref[3:4, :] = r2.astype(jnp.int32)
    base_ref[...] = base_ref[...] + jnp.sum(onehot, axis=1, keepdims=True)
    cnt_ref[...] = base_ref[...].astype(jnp.int32)


def _router(x2, mod3, mod_row, g, rw16, *, tm):
    m, d = x2.shape
    modspec = lambda k: pl.BlockSpec((1, 1, d), lambda i: (mod_row(i), 0, k))
    return pl.pallas_call(
        _router_kernel,
        grid=(m // tm,),
        in_specs=[pl.BlockSpec((tm, d), lambda i: (i, 0)),
                  modspec(3), modspec(4),
                  pl.BlockSpec((1, d), lambda i: (0, 0)),
                  pl.BlockSpec((ROUTER_ROWS, d), lambda i: (0, 0))],
        out_specs=[pl.BlockSpec((tm, d), lambda i: (i, 0)),
                   pl.BlockSpec((4, tm), lambda i: (0, i)),
                   pl.BlockSpec((2, tm), lambda i: (0, i)),
                   pl.BlockSpec((ROUTER_ROWS, 128), lambda i: (0, 0))],
        out_shape=[jax.ShapeDtypeStruct((m, d), F32),
                   jax.ShapeDtypeStruct((4, m), jnp.int32),
                   jax.ShapeDtypeStruct((2, m), F32),
                   jax.ShapeDtypeStruct((ROUTER_ROWS, 128), jnp.int32)],
        scratch_shapes=[pltpu.VMEM((tm, tm), BF16), pltpu.VMEM((ROUTER_ROWS, 128), F32)],
        compiler_params=_cparams(1),
        name="router",
    )(x2, mod3, mod3, g, rw16)


def _row_copy(src_ref, src_row, dst_ref, dst_row, sem):
    return pltpu.make_async_copy(src_ref.at[pl.ds(src_row, 1), :], dst_ref.at[pl.ds(dst_row, 1), :], sem)


def _scatter_kernel(dest_ref, t_ref, buf_in_ref, buf_ref, sem, *, n_tok):
    del buf_in_ref
    tm = t_ref.shape[0]
    base = pl.program_id(0) * tm

    def start(r, carry):
        for k in range(2):
            _row_copy(t_ref, r, buf_ref, dest_ref[k * n_tok + base + r], sem).start()
        return carry

    lax.fori_loop(0, tm, start, 0)

    def wait(r, carry):
        for k in range(2):
            _row_copy(t_ref, 0, buf_ref, 0, sem).wait()
        return carry

    lax.fori_loop(0, tm, wait, 0)


def _scatter_rows(dest, t, buf, *, tm):
    m, d = t.shape
    grid_spec = pltpu.PrefetchScalarGridSpec(
        num_scalar_prefetch=1,
        grid=(m // tm,),
        in_specs=[pl.BlockSpec((tm, d), lambda i, dest: (i, 0)),
                  pl.BlockSpec(memory_space=pl.ANY)],
        out_specs=pl.BlockSpec(memory_space=pl.ANY),
        scratch_shapes=[pltpu.SemaphoreType.DMA(())],
    )
    return pl.pallas_call(
        functools.partial(_scatter_kernel, n_tok=m),
        grid_spec=grid_spec,
        out_shape=jax.ShapeDtypeStruct(buf.shape, buf.dtype),
        input_output_aliases={2: 0},
        compiler_params=_cparams(1),
        name="moe_scatter",
    )(dest, t, buf)


def _expert_kernel(te_ref, tv_ref, x_ref, wg_ref, wu_ref, wd_ref, o_ref, xb_ref):
    i = pl.program_id(0)
    j = pl.program_id(1)
    valid = tv_ref[i] == 1

    @pl.when(j == 0)
    def _():
        o_ref[...] = jnp.zeros(o_ref.shape, o_ref.dtype)

    @pl.when(valid)
    def _():
        @pl.when(j == 0)
        def _():
            xb_ref[...] = x_ref[...].astype(BF16)

        xb = xb_ref[...]
        gv = jnp.dot(xb, wg_ref[...], preferred_element_type=F32)
        uv = jnp.dot(xb, wu_ref[...], preferred_element_type=F32)
        a = (gv * _sigmoid(gv) * uv).astype(BF16)
        o_ref[...] += jnp.dot(a, wd_ref[...], preferred_element_type=F32)


def _experts(tile_e, tile_v, tile_x, buf, wg, wu, wd, *, tf):
    rows, d = buf.shape
    f = wg.shape[2]
    nf = f // tf
    fcol = lambda i, j, tv: jnp.where(tv[i] == 1, j, nf - 1)
    grid_spec = pltpu.PrefetchScalarGridSpec(
        num_scalar_prefetch=3,
        grid=(rows // EXPERT_TM, nf),
        in_specs=[pl.BlockSpec((EXPERT_TM, d), lambda i, j, te, tv, tx: (tx[i], 0)),
                  pl.BlockSpec((None, d, tf), lambda i, j, te, tv, tx: (te[i], 0, fcol(i, j, tv))),
                  pl.BlockSpec((None, d, tf), lambda i, j, te, tv, tx: (te[i], 0, fcol(i, j, tv))),
                  pl.BlockSpec((None, tf, d), lambda i, j, te, tv, tx: (te[i], fcol(i, j, tv), 0))],
        out_specs=pl.BlockSpec((EXPERT_TM, d), lambda i, j, te, tv, tx: (i, 0)),
        scratch_shapes=[pltpu.VMEM((EXPERT_TM, d), BF16)],
    )

    def kern(te_ref, tv_ref, tx_ref, x_ref, wg_ref, wu_ref, wd_ref, o_ref, xb_ref):
        del tx_ref
        _expert_kernel(te_ref, tv_ref, x_ref, wg_ref, wu_ref, wd_ref, o_ref, xb_ref)

    return pl.pallas_call(
        kern,
        grid_spec=grid_spec,
        out_shape=jax.ShapeDtypeStruct((rows, d), F32),
        compiler_params=_cparams(2),
        name="moe_experts",
    )(tile_e, tile_v, tile_x, buf, wg, wu, wd)


def _combine_kernel(dest_ref, x_ref, gate5_ref, gates_ref, gf_ref, ybuf_ref, o_ref, rows_ref, sem, *, n_tok):
    tm = x_ref.shape[0]
    base = pl.program_id(0) * tm

    def start(r, carry):
        for k in range(2):
            _row_copy(ybuf_ref, dest_ref[k * n_tok + base + r], rows_ref.at[k], r, sem).start()
        return carry

    lax.fori_loop(0, tm, start, 0)

    def wait(r, carry):
        for k in range(2):
            _row_copy(ybuf_ref, 0, rows_ref.at[k], 0, sem).wait()
        return carry

    lax.fori_loop(0, tm, wait, 0)

    eye = (lax.broadcasted_iota(jnp.int32, (tm, tm), 0) == lax.broadcasted_iota(jnp.int32, (tm, tm), 1))
    g0 = jnp.sum(jnp.where(eye, gates_ref[0:1, :], 0.0), axis=1, keepdims=True)
    g1 = jnp.sum(jnp.where(eye, gates_ref[1:2, :], 0.0), axis=1, keepdims=True)
    y = g0 * rows_ref[0] + g1 * rows_ref[1]
    xn = x_ref[...] + gate5_ref[0] * y
    ms = jnp.mean(xn * xn, axis=-1, keepdims=True)
    o_ref[...] = xn * lax.rsqrt(ms + EPS) * gf_ref[...]


def _combine(dest, x2, mod3, mod_row, gates, g_final, ybuf, *, tm):
    m, d = x2.shape
    grid_spec = pltpu.PrefetchScalarGridSpec(
        num_scalar_prefetch=1,
        grid=(m // tm,),
        in_specs=[pl.BlockSpec((tm, d), lambda i, dest: (i, 0)),
                  pl.BlockSpec((1, 1, d), lambda i, dest: (mod_row(i), 0, 5)),
                  pl.BlockSpec((2, tm), lambda i, dest: (0, i)),
                  pl.BlockSpec((1, d), lambda i, dest: (0, 0)),
                  pl.BlockSpec(memory_space=pl.ANY)],
        out_specs=pl.BlockSpec((tm, d), lambda i, dest: (i, 0)),
        scratch_shapes=[pltpu.VMEM((2, tm, d), F32), pltpu.SemaphoreType.DMA(())],
    )
    return pl.pallas_call(
        functools.partial(_combine_kernel, n_tok=m),
        grid_spec=grid_spec,
        out_shape=jax.ShapeDtypeStruct((m, d), F32),
        compiler_params=_cparams(1),
        name="moe_combine",
    )(dest, x2, mod3, gates, g_final, ybuf)


def _rope_tables(seq):
    rows = seq // GRID_W
    row = jnp.repeat(jnp.arange(rows), GRID_W).astype(F32)
    col = jnp.tile(jnp.arange(GRID_W), rows).astype(F32)
    axis_dim = HEAD_DIM // 2
    inv_freq = ROPE_THETA ** (-jnp.arange(0, axis_dim, 2, dtype=F32) / axis_dim)
    ang_r = row[:, None] * inv_freq
    ang_c = col[:, None] * inv_freq
    cr, sr, cc, sc = jnp.cos(ang_r), jnp.sin(ang_r), jnp.cos(ang_c), jnp.sin(ang_c)
    return (jnp.concatenate([cr, cr, cc, cc], axis=-1),
            jnp.concatenate([-sr, sr, -sc, sc], axis=-1))


def kernel(x, c, ctx, c_ctx, w_ada, b_ada, g_mix, w_in, g_q, g_k, sconv_w, conf_dw, conf_db, conf_ln_g,
           conf_ln_b, conf_pw, conf_pb, g_group, w_o, g_ffn, dense_wg, dense_wu, dense_wd, router_w,
           moe_wg, moe_wu, moe_wd, g_final):
    b, s, d = x.shape
    n_ctx = ctx.shape[1]
    depth = w_ada.shape[0]
    assert b + 1 <= MOD_ROWS and depth == 2
    m_lat, m_ctx = b * s, b * n_ctx
    ctx_row = b

    cin = jnp.concatenate([c, c_ctx[None, :], jnp.zeros((MOD_ROWS - b - 1, d), F32)], axis=0)
    mod = _ada(cin, w_ada, b_ada)
    cos_t, sin_t = _rope_tables(s)
    row2 = lambda v: v.reshape(1, -1)

    lat_tm = 512
    lat_row_for = lambda tm: (lambda i: i // (s // tm))
    lat_row = lat_row_for(lat_tm)
    ctx_tm = 512
    ctx_mod_row = lambda i: ctx_row

    x2 = x.reshape(m_lat, d)
    xc2 = ctx.reshape(m_ctx, d)
    out = None
    for l in range(depth):
        last = l == depth - 1
        mod3 = mod[l].reshape(MOD_ROWS, 1, 6 * d)
        w_in_b = w_in[l].astype(BF16)
        w_o_b = w_o[l].astype(BF16)
        pw_b = conf_pw[l].astype(BF16)
        gg = g_group[l]
        gga, ggs, ggc = row2(gg[:ATTN_W]), row2(gg[ATTN_W:ATTN_W + SCONV_W]), row2(gg[ATTN_W + SCONV_W:])
        mixer_args = (sconv_w[l], conf_dw[l], row2(conf_db[l]), row2(conf_ln_g[l]), row2(conf_ln_b[l]),
                      pw_b, row2(conf_pb[l]), ggs, ggc)
        in_args = (row2(g_mix[l]), w_in_b, row2(g_q[l]), row2(g_k[l]), cos_t, sin_t)

        p_lat = _in_proj(x2, mod3, lat_row, *in_args, tm=lat_tm, rope=True, j0=0, nj=8, seq=s)
        if last:
            p_ctx = _in_proj(xc2, mod3, ctx_mod_row, *in_args, tm=ctx_tm, rope=False, j0=2, nj=1, seq=ctx_tm)
            kc_blk = 0
        else:
            p_ctx = _in_proj(xc2, mod3, ctx_mod_row, *in_args, tm=ctx_tm, rope=False, j0=0, nj=8, seq=ctx_tm)
            kc_blk = ATTN_W // KV_W
        att = _attention(p_lat, p_ctx, p_lat, gga, n_batch=b, q_len=s, tq=512, kc_blk=kc_blk,
                         has_latent=True, n_ctx=n_ctx, seq=s)
        mix = _local_mixers(p_lat, *mixer_args, n_seq=b, seq=s)
        x2 = _out_proj(att, mix, w_o_b, x2, mod3, lat_row, tm=lat_tm)
        if not last:
            att_c = _attention(p_ctx, p_ctx, p_ctx, gga, n_batch=b, q_len=n_ctx, tq=n_ctx, kc_blk=kc_blk,
                               has_latent=False, n_ctx=n_ctx, seq=n_ctx)
            mix_c = _local_mixers(p_ctx, *mixer_args, n_seq=b, seq=n_ctx)
            xc2 = _out_proj(att_c, mix_c, w_o_b, xc2, mod3, ctx_mod_row, tm=ctx_tm)

        if l % 2 == 0:
            wg_b, wu_b, wd_b = (dense_wg[l // 2].astype(BF16), dense_wu[l // 2].astype(BF16),
                                dense_wd[l // 2].astype(BF16))
            ffn = functools.partial(_dense_ffn, g=row2(g_ffn[l]), wg=wg_b, wu=wu_b, wd=wd_b, tf=512)
            x2 = ffn(x2, mod3, lat_row, tm=lat_tm)
            if not last:
                xc2 = ffn(xc2, mod3, ctx_mod_row, tm=ctx_tm)
        else:
            assert last, "routed layer is implemented for the final layer (latent tokens only)"
            rw16 = jnp.zeros((ROUTER_ROWS, d), F32).at[:N_EXPERTS].set(router_w[l // 2].T)
            t, idx, gates, cnt = _router(x2, mod3, lat_row, row2(g_ffn[l]), rw16, tm=lat_tm)
            counts = cnt[:N_EXPERTS, 0]
            padded = (counts + EXPERT_TM - 1) // EXPERT_TM * EXPERT_TM
            pends = jnp.cumsum(padded)
            pstarts = pends - padded
            slot_e = idx[0:2]
            slot_start = sum(jnp.where(slot_e == e, pstarts[e], 0) for e in range(N_EXPERTS))
            dest = (slot_start + idx[2:4]).reshape(-1).astype(jnp.int32)
            n_tiles = (2 * m_lat) // EXPERT_TM + N_EXPERTS
            tile_start = jnp.arange(n_tiles, dtype=jnp.int32) * EXPERT_TM
            tile_v = (tile_start < pends[-1]).astype(jnp.int32)
            last_tile = pends[-1] // EXPERT_TM - 1
            tile_x = jnp.minimum(jnp.arange(n_tiles, dtype=jnp.int32), last_tile).astype(jnp.int32)
            tile_e = jnp.minimum(jnp.searchsorted(pends, tile_x * EXPERT_TM, side='right'),
                                 N_EXPERTS - 1).astype(jnp.int32)
            buf = jnp.zeros((n_tiles * EXPERT_TM, d), F32)
            buf = _scatter_rows(dest, t, buf, tm=lat_tm)
            ybuf = _experts(tile_e, tile_v, tile_x, buf, moe_wg[l // 2].astype(BF16),
                            moe_wu[l // 2].astype(BF16), moe_wd[l // 2].astype(BF16), tf=512)
            out = _combine(dest, x2, mod3, lat_row_for(256), gates, row2(g_final), ybuf, tm=256)
    return out.reshape(b, s, d)
```

```python
import functools

import jax
import jax.numpy as jnp
from jax import lax
from jax.experimental import pallas as pl
from jax.experimental.pallas import tpu as pltpu

F32 = jnp.float32
BF16 = jnp.bfloat16

GRID_W = 64
HEAD_DIM = 128
N_Q_HEADS = 8
N_KV_HEADS = 2
GQA_GROUP = N_Q_HEADS // N_KV_HEADS
ATTN_W = N_Q_HEADS * HEAD_DIM
KV_W = N_KV_HEADS * HEAD_DIM
ROPE_THETA = 10000.0
SCONV_W = 512
SCONV_K = 3
CONF_W = 512
CONF_K = 31
N_EXPERTS = 8
EPS = 1e-6
LOG2E = 1.4426950408889634

V7X_VMEM_LIMIT_BYTES = 56 * 1024 * 1024
MOD_ROWS = 16
IN_TN = 512
EXPERT_TM = 512


def _cparams(n_axes):
    return pltpu.CompilerParams(dimension_semantics=("arbitrary",) * n_axes,
                                vmem_limit_bytes=V7X_VMEM_LIMIT_BYTES)


def _sigmoid(x):
    return 1.0 / (1.0 + jnp.exp(-x))


NORM_CHUNK = 16
NORM_UNROLL = 4


def _norm_modulate_rows(x_ref, h_ref, g, shift, scale):
    rows = x_ref.shape[0]
    gain = g * (1.0 + scale)

    def body(c, carry):
        r = pl.multiple_of(c * NORM_CHUNK, NORM_CHUNK)
        x = x_ref[pl.ds(r, NORM_CHUNK), :]
        inv = lax.rsqrt(jnp.mean(x * x, axis=-1, keepdims=True) + EPS)
        h_ref[pl.ds(r, NORM_CHUNK), :] = (x * inv * gain + shift).astype(h_ref.dtype)
        return carry

    lax.fori_loop(0, rows // NORM_CHUNK, body, 0, unroll=NORM_UNROLL)


def _ada_kernel(c_ref, w_ref, b_ref, o_ref):
    a = c_ref[...]
    a = (a * _sigmoid(a)).astype(BF16)
    o_ref[0] = jnp.dot(a, w_ref[0].astype(BF16), preferred_element_type=F32) + b_ref[0]


def _ada(cin, w_ada, b_ada):
    depth, d, n = w_ada.shape
    tn = 1536
    return pl.pallas_call(
        _ada_kernel,
        grid=(depth, n // tn),
        in_specs=[pl.BlockSpec((MOD_ROWS, d), lambda l, j: (0, 0)),
                  pl.BlockSpec((1, d, tn), lambda l, j: (l, 0, j)),
                  pl.BlockSpec((1, 1, tn), lambda l, j: (l, 0, j))],
        out_specs=pl.BlockSpec((1, MOD_ROWS, tn), lambda l, j: (l, 0, j)),
        out_shape=jax.ShapeDtypeStruct((depth, MOD_ROWS, n), F32),
        compiler_params=_cparams(2),
        name="ada",
    )(cin, w_ada, b_ada.reshape(depth, 1, n))


def _head_norm(seg, gain):
    ms = jnp.mean(seg * seg, axis=-1, keepdims=True)
    return seg * lax.rsqrt(ms + EPS) * gain


def _rope(n, cos_t, sin_t):
    lane = lax.broadcasted_iota(jnp.int32, n.shape, 1)
    fwd = pltpu.roll(n, 32, 1)
    bwd = pltpu.roll(n, 96, 1)
    partner = jnp.where((lane // 32) % 2 == 0, bwd, fwd)
    return n * cos_t + partner * sin_t


def _in_kernel(x_ref, shift_ref, scale_ref, g_ref, w_ref, gq_ref, gk_ref, cos_ref, sin_ref,
               *rest, rope, j0, nj, q_scale, n_cast):
    cast_in, o_ref, cast_out, h_ref = rest[:n_cast], rest[n_cast], rest[n_cast + 1:2 * n_cast + 1], rest[-1]
    _norm_modulate_rows(x_ref, h_ref, g_ref[...], shift_ref[0], scale_ref[0])
    _cast_slabs(cast_in, cast_out)
    h = h_ref[...]

    def head(seg, gain, scale):
        n = _head_norm(seg, gain)
        if rope:
            n = _rope(n, cos_ref[...], sin_ref[...])
        if scale != 1.0:
            n = n * scale
        return n

    for t in range(nj):
        j = t + j0
        c0 = t * IN_TN
        acc = jnp.dot(h, w_ref[:, c0:c0 + IN_TN], preferred_element_type=F32)
        if j < 2:
            for hd in range(IN_TN // HEAD_DIM):
                sl = slice(hd * HEAD_DIM, (hd + 1) * HEAD_DIM)
                osl = slice(c0 + hd * HEAD_DIM, c0 + (hd + 1) * HEAD_DIM)
                o_ref[:, osl] = head(acc[:, sl], gq_ref[...], q_scale).astype(o_ref.dtype)
        elif j == 2:
            for hd in range(N_KV_HEADS):
                sl = slice(hd * HEAD_DIM, (hd + 1) * HEAD_DIM)
                osl = slice(c0 + hd * HEAD_DIM, c0 + (hd + 1) * HEAD_DIM)
                o_ref[:, osl] = head(acc[:, sl], gk_ref[...], 1.0).astype(o_ref.dtype)
            o_ref[:, c0 + KV_W:c0 + IN_TN] = acc[:, KV_W:].astype(o_ref.dtype)
        else:
            o_ref[:, c0:c0 + IN_TN] = acc.astype(o_ref.dtype)


def _const_spec(shape, idx):
    return pl.BlockSpec(shape, lambda i: idx, pipeline_mode=pl.Buffered(1))


def _in_proj(x2, mod3, mod_row, g, w, gq, gk, cos_t, sin_t, *, tm, rope, j0, nj, seq, cast=()):
    m, d = x2.shape
    tiles_per_seq = seq // tm
    ncols = nj * IN_TN
    assert j0 % nj == 0
    q_scale = HEAD_DIM ** -0.5 * LOG2E
    kern = functools.partial(_in_kernel, rope=rope, j0=j0, nj=nj, q_scale=q_scale, n_cast=len(cast))
    cast_specs = _cast_specs(cast, m // tm, lambda i: i)
    outs = pl.pallas_call(
        kern,
        grid=(m // tm,),
        in_specs=[pl.BlockSpec((tm, d), lambda i: (i, 0)),
                  pl.BlockSpec((1, 1, d), lambda i: (mod_row(i), 0, 0)),
                  pl.BlockSpec((1, 1, d), lambda i: (mod_row(i), 0, 1)),
                  _const_spec((1, d), (0, 0)),
                  _const_spec((d, ncols), (0, j0 // nj)),
                  _const_spec((1, HEAD_DIM), (0, 0)),
                  _const_spec((1, HEAD_DIM), (0, 0)),
                  pl.BlockSpec((tm, HEAD_DIM), lambda i: (i % tiles_per_seq, 0)),
                  pl.BlockSpec((tm, HEAD_DIM), lambda i: (i % tiles_per_seq, 0))] + cast_specs,
        out_specs=[pl.BlockSpec((tm, ncols), lambda i: (i, 0))] + cast_specs,
        out_shape=[jax.ShapeDtypeStruct((m, ncols), BF16)] + [jax.ShapeDtypeStruct(a.shape, BF16) for a in cast],
        scratch_shapes=[pltpu.VMEM((tm, d), BF16)],
        compiler_params=_cparams(1),
        name="in_proj",
    )(x2, mod3, mod3, g, w, gq, gk, cos_t, sin_t, *cast)
    return outs[0], tuple(outs[1:])


def _attn_kernel(q_ref, kc_ref, vc_ref, kl_ref, vl_ref, gg_ref, o_ref, acc_ref, *, has_latent):
    tq = q_ref.shape[0]
    dn = (((1,), (1,)), ((), ()))
    ssq = jnp.zeros((tq, 1), F32)
    for hd in range(N_Q_HEADS):
        kv = hd // GQA_GROUP
        ksl = slice(kv * HEAD_DIM, (kv + 1) * HEAD_DIM)
        q = q_ref[:, hd * HEAD_DIM:(hd + 1) * HEAD_DIM]
        s_c = lax.dot_general(q, kc_ref[:, ksl], dn, preferred_element_type=F32)
        mx = jnp.max(s_c, axis=-1, keepdims=True)
        if has_latent:
            s_l = lax.dot_general(q, kl_ref[:, ksl], dn, preferred_element_type=F32)
            mx = jnp.maximum(mx, jnp.max(s_l, axis=-1, keepdims=True))
        p_c = jnp.exp2(s_c - mx)
        den = jnp.sum(p_c, axis=-1, keepdims=True)
        o = jnp.dot(p_c.astype(BF16), vc_ref[:, ksl], preferred_element_type=F32)
        if has_latent:
            p_l = jnp.exp2(s_l - mx)
            den = den + jnp.sum(p_l, axis=-1, keepdims=True)
            o = o + jnp.dot(p_l.astype(BF16), vl_ref[:, ksl], preferred_element_type=F32)
        o = o * (1.0 / den)
        ssq = ssq + jnp.sum(o * o, axis=-1, keepdims=True)
        acc_ref[:, hd * HEAD_DIM:(hd + 1) * HEAD_DIM] = o
    inv = lax.rsqrt(ssq * (1.0 / ATTN_W) + EPS)
    o_ref[...] = (acc_ref[...] * inv * gg_ref[...]).astype(o_ref.dtype)


def _attention(qsrc, csrc, lsrc, gg, *, n_batch, q_len, tq, kc_blk, has_latent, n_ctx, seq):
    m = qsrc.shape[0]
    tiles = q_len // tq
    kern = functools.partial(_attn_kernel, has_latent=has_latent)
    return pl.pallas_call(
        kern,
        grid=(n_batch, tiles),
        in_specs=[pl.BlockSpec((tq, ATTN_W), lambda b, i: (b * tiles + i, 0)),
                  pl.BlockSpec((n_ctx, KV_W), lambda b, i: (b, kc_blk)),
                  pl.BlockSpec((n_ctx, KV_W), lambda b, i: (b, kc_blk + 1)),
                  pl.BlockSpec((seq, KV_W), lambda b, i: (b, ATTN_W // KV_W)),
                  pl.BlockSpec((seq, KV_W), lambda b, i: (b, ATTN_W // KV_W + 1)),
                  pl.BlockSpec((1, ATTN_W), lambda b, i: (0, 0))],
        out_specs=pl.BlockSpec((tq, ATTN_W), lambda b, i: (b * tiles + i, 0)),
        out_shape=jax.ShapeDtypeStruct((m, ATTN_W), BF16),
        scratch_shapes=[pltpu.VMEM((tq, ATTN_W), F32)],
        compiler_params=_cparams(2),
        name="attention",
    )(qsrc, csrc, csrc, lsrc, lsrc, gg)


MIX_CHUNK = 64
SCONV_PAD = 8
CONF_PAD = 16


LANES = 128
SUBLANES = 8
N_LANE_GROUPS = SCONV_W // LANES


def _window_conv(p_ref, slot, w_ref, group, first, n_taps, ch):
    lanes = slice(group * LANES, (group + 1) * LANES)
    win = p_ref[slot, :, lanes]
    n_rows = win.shape[0]
    acc = None
    for phase in range(SUBLANES):
        taps = [k for k in range(n_taps) if (first + k) % SUBLANES == phase]
        if not taps:
            continue
        shifted = win if phase == 0 else pltpu.roll(win, n_rows - phase, 0)
        for k in taps:
            base = first + k - phase
            term = w_ref[k:k + 1, lanes] * shifted[base:base + ch]
            acc = term if acc is None else acc + term
    return acc


def _mix_kernel(sb_ref, sg_ref, sx_ref, ga_ref, gb_ref, sw_ref, dw_ref, db_ref, lng_ref, lnb_ref,
                pw_ref, pb_ref, ggs_ref, ggc_ref, o_ref, p1_ref, p2_ref, a_ref):
    seq = sb_ref.shape[0]
    ch = MIX_CHUNK
    n_chunks = seq // ch
    hs, hc = SCONV_PAD, CONF_PAD
    p1_ref[1, 0:hs, :] = jnp.zeros((hs, SCONV_W), F32)
    p1_ref[n_chunks, ch + hs:ch + 2 * hs, :] = jnp.zeros((hs, SCONV_W), F32)
    p2_ref[1, 0:hc, :] = jnp.zeros((hc, CONF_W), F32)
    p2_ref[n_chunks, ch + hc:ch + 2 * hc, :] = jnp.zeros((hc, CONF_W), F32)

    def fill(c, carry):
        r = pl.multiple_of(c * ch, ch)
        rows = pl.ds(r, ch)
        v = sg_ref[rows, :].astype(F32) * sx_ref[rows, :].astype(F32)
        p1_ref[c + 1, hs:hs + ch, :] = v
        p1_ref[c, ch + hs:ch + 2 * hs, :] = v[0:hs]
        p1_ref[c + 2, 0:hs, :] = v[ch - hs:ch]
        u = ga_ref[rows, :].astype(F32) * _sigmoid(gb_ref[rows, :].astype(F32))
        p2_ref[c + 1, hc:hc + ch, :] = u
        p2_ref[c, ch + hc:ch + 2 * hc, :] = u[0:hc]
        p2_ref[c + 2, 0:hc, :] = u[ch - hc:ch]
        return carry

    lax.fori_loop(0, n_chunks, fill, 0)

    def conv(c, carry):
        r = pl.multiple_of(c * ch, ch)
        rows = pl.ds(r, ch)
        acc = jnp.concatenate(
            [_window_conv(p1_ref, c + 1, sw_ref, g, hs - SCONV_K // 2, SCONV_K, ch) for g in range(N_LANE_GROUPS)],
            axis=1)
        ys = sb_ref[rows, :].astype(F32) * acc
        ms = jnp.mean(ys * ys, axis=-1, keepdims=True)
        o_ref[rows, 0:SCONV_W] = (ys * lax.rsqrt(ms + EPS) * ggs_ref[...]).astype(o_ref.dtype)

        u = db_ref[...] + jnp.concatenate(
            [_window_conv(p2_ref, c + 1, dw_ref, g, hc - CONF_K // 2, CONF_K, ch) for g in range(N_LANE_GROUPS)],
            axis=1)
        mu = jnp.mean(u, axis=-1, keepdims=True)
        uc = u - mu
        var = jnp.mean(uc * uc, axis=-1, keepdims=True)
        v = uc * lax.rsqrt(var + EPS) * lng_ref[...] + lnb_ref[...]
        a_ref[rows, :] = (v * _sigmoid(v)).astype(a_ref.dtype)
        return carry

    lax.fori_loop(0, n_chunks, conv, 0)

    yc = jnp.dot(a_ref[...], pw_ref[...], preferred_element_type=F32) + pb_ref[...]
    ms = jnp.mean(yc * yc, axis=-1, keepdims=True)
    o_ref[:, SCONV_W:] = (yc * lax.rsqrt(ms + EPS) * ggc_ref[...]).astype(o_ref.dtype)


def _local_mixers(src, sconv_w, conf_dw, conf_db, ln_g, ln_b, pw, pb, ggs, ggc, *, n_seq, seq):
    w = SCONV_W
    col = lambda k: pl.BlockSpec((seq, w), lambda b: (b, k))
    vec = lambda n: pl.BlockSpec((1, n), lambda b: (0, 0))
    return pl.pallas_call(
        _mix_kernel,
        grid=(n_seq,),
        in_specs=[col(3), col(4), col(5), col(6), col(7),
                  pl.BlockSpec((SCONV_K, w), lambda b: (0, 0)),
                  pl.BlockSpec((CONF_K, w), lambda b: (0, 0)),
                  vec(w), vec(w), vec(w),
                  pl.BlockSpec((w, w), lambda b: (0, 0)),
                  vec(w), vec(w), vec(w)],
        out_specs=pl.BlockSpec((seq, 2 * w), lambda b: (b, 0)),
        out_shape=jax.ShapeDtypeStruct((src.shape[0], 2 * w), BF16),
        scratch_shapes=[pltpu.VMEM((seq // MIX_CHUNK + 2, MIX_CHUNK + 2 * SCONV_PAD, w), F32),
                        pltpu.VMEM((seq // MIX_CHUNK + 2, MIX_CHUNK + 2 * CONF_PAD, w), F32),
                        pltpu.VMEM((seq, w), BF16)],
        compiler_params=_cparams(1),
        name="local_mixers",
    )(src, src, src, src, src, sconv_w, conf_dw, conf_db, ln_g, ln_b, pw, pb, ggs, ggc)


OUT_TN = 512


def _out_kernel(a_ref, m_ref, wa_ref, wm_ref, x_ref, gate_ref, o_ref):
    a = a_ref[...]
    mx = m_ref[...]
    for t in range(o_ref.shape[1] // OUT_TN):
        sl = slice(t * OUT_TN, (t + 1) * OUT_TN)
        acc = (jnp.dot(a, wa_ref[:, sl], preferred_element_type=F32)
               + jnp.dot(mx, wm_ref[:, sl], preferred_element_type=F32))
        o_ref[:, sl] = x_ref[:, sl] + gate_ref[0, :, sl] * acc


def _out_proj(att, mix, w, x2, mod3, mod_row, *, tm):
    m, d = x2.shape
    half = att.shape[1]
    return pl.pallas_call(
        _out_kernel,
        grid=(m // tm,),
        in_specs=[pl.BlockSpec((tm, half), lambda i: (i, 0)),
                  pl.BlockSpec((tm, half), lambda i: (i, 0)),
                  _const_spec((half, d), (0, 0)),
                  _const_spec((half, d), (1, 0)),
                  pl.BlockSpec((tm, d), lambda i: (i, 0)),
                  pl.BlockSpec((1, 1, d), lambda i: (mod_row(i), 0, 2))],
        out_specs=pl.BlockSpec((tm, d), lambda i: (i, 0)),
        out_shape=jax.ShapeDtypeStruct((m, d), F32),
        compiler_params=_cparams(1),
        name="out_proj",
    )(att, mix, w, w, x2, mod3)


def _ffn_kernel(x_ref, shift_ref, scale_ref, gate_ref, g_ref, wg_ref, wu_ref, wd_ref, *rest, n_cast):
    cast_in, o_ref, cast_out, h_ref = rest[:n_cast], rest[n_cast], rest[n_cast + 1:2 * n_cast + 1], rest[-1]
    j = pl.program_id(1)

    _cast_slabs(cast_in, cast_out)

    @pl.when(j == 0)
    def _():
        _norm_modulate_rows(x_ref, h_ref, g_ref[...], shift_ref[0], scale_ref[0])
        o_ref[...] = jnp.zeros(o_ref.shape, o_ref.dtype)

    h = h_ref[...]
    gv = jnp.dot(h, wg_ref[...], preferred_element_type=F32)
    uv = jnp.dot(h, wu_ref[...], preferred_element_type=F32)
    a = (gv * _sigmoid(gv) * uv).astype(BF16)
    o_ref[...] += jnp.dot(a, wd_ref[...], preferred_element_type=F32)

    @pl.when(j == pl.num_programs(1) - 1)
    def _():
        o_ref[...] = x_ref[...] + gate_ref[0] * o_ref[...]


BF16_SUBLANE_TILE = 16


def _slab_rows(rows, n_lead, n_steps):
    for height in range(BF16_SUBLANE_TILE, rows + 1, BF16_SUBLANE_TILE):
        if rows % height == 0 and n_lead * (rows // height) <= n_steps:
            return height
    raise ValueError(f"no slab height for rows={rows}, n_lead={n_lead}, n_steps={n_steps}")


def _cast_specs(cast, n_steps, step_of):
    specs = []
    for arr in cast:
        n_lead, rows, cols = arr.shape
        height = _slab_rows(rows, n_lead, n_steps)
        per_lead = rows // height
        last = n_lead * per_lead - 1

        def index(*grid_idx, per_lead=per_lead, last=last):
            s = jnp.minimum(step_of(*grid_idx), last)
            return (s // per_lead, s % per_lead, 0)

        specs.append(pl.BlockSpec((1, height, cols), index))
    return specs


def _cast_slabs(cast_in, cast_out):
    for src, dst in zip(cast_in, cast_out):
        dst[...] = src[...].astype(dst.dtype)


def _dense_ffn(x2, mod3, mod_row, g, wg, wu, wd, *, tm, tf, cast=()):
    m, d = x2.shape
    f = wg.shape[1]
    nf = f // tf
    n_steps = (m // tm) * nf
    modspec = lambda k: pl.BlockSpec((1, 1, d), lambda i, j: (mod_row(i), 0, k))
    cast_specs = _cast_specs(cast, n_steps, lambda i, j: i * nf + j)
    outs = pl.pallas_call(
        functools.partial(_ffn_kernel, n_cast=len(cast)),
        grid=(m // tm, nf),
        in_specs=[pl.BlockSpec((tm, d), lambda i, j: (i, 0)),
                  modspec(3), modspec(4), modspec(5),
                  pl.BlockSpec((1, d), lambda i, j: (0, 0)),
                  pl.BlockSpec((d, tf), lambda i, j: (0, j)),
                  pl.BlockSpec((d, tf), lambda i, j: (0, j)),
                  pl.BlockSpec((tf, d), lambda i, j: (j, 0))] + cast_specs,
        out_specs=[pl.BlockSpec((tm, d), lambda i, j: (i, 0))] + cast_specs,
        out_shape=[jax.ShapeDtypeStruct((m, d), F32)] + [jax.ShapeDtypeStruct(a.shape, BF16) for a in cast],
        scratch_shapes=[pltpu.VMEM((tm, d), BF16)],
        compiler_params=_cparams(2),
        name="dense_ffn",
    )(x2, mod3, mod3, mod3, g, wg, wu, wd, *cast)
    return outs[0], tuple(outs[1:])


ROUTER_ROWS = 16


def _router_kernel(x_ref, shift_ref, scale_ref, g_ref, rw_ref, t_ref, idx_ref, gate_ref, cnt_ref,
                   tri_ref, base_ref):
    tm = x_ref.shape[0]

    @pl.when(pl.program_id(0) == 0)
    def _():
        r = lax.broadcasted_iota(jnp.int32, (tm, tm), 0)
        c = lax.broadcasted_iota(jnp.int32, (tm, tm), 1)
        tri_ref[...] = jnp.where(r < c, 1.0, 0.0).astype(BF16)
        base_ref[...] = jnp.zeros(base_ref.shape, F32)

    _norm_modulate_rows(x_ref, t_ref, g_ref[...], shift_ref[0], scale_ref[0])

    t = t_ref[...]
    t_hi = t.astype(BF16)
    t_lo = (t - t_hi.astype(F32)).astype(BF16)
    w = rw_ref[...]
    w_hi = w.astype(BF16)
    w_lo = (w - w_hi.astype(F32)).astype(BF16)
    dn = (((1,), (1,)), ((), ()))
    logits = (lax.dot_general(w_hi, t_hi, dn, preferred_element_type=F32)
              + lax.dot_general(w_lo, t_hi, dn, preferred_element_type=F32)
              + lax.dot_general(w_hi, t_lo, dn, preferred_element_type=F32))

    e = lax.broadcasted_iota(jnp.int32, (ROUTER_ROWS, tm), 0).astype(F32)
    neg = jnp.float32(-jnp.inf)
    lg = jnp.where(e < N_EXPERTS, logits, neg)
    m1 = jnp.max(lg, axis=0, keepdims=True)
    i1 = jnp.min(jnp.where(lg == m1, e, float(ROUTER_ROWS)), axis=0, keepdims=True)
    lg2 = jnp.where(e == i1, neg, lg)
    m2 = jnp.max(lg2, axis=0, keepdims=True)
    i2 = jnp.min(jnp.where(lg2 == m2, e, float(ROUTER_ROWS)), axis=0, keepdims=True)
    ex = jnp.exp(m2 - m1)
    den = 1.0 + ex
    gate_ref[0:1, :] = 1.0 / den
    gate_ref[1:2, :] = ex / den

    hit1 = e == i1
    hit2 = e == i2
    onehot = jnp.where(hit1 | hit2, 1.0, 0.0)
    prefix = jnp.dot(onehot.astype(BF16), tri_ref[...], preferred_element_type=F32) + base_ref[:, 0:1]
    r1 = jnp.sum(jnp.where(hit1, prefix, 0.0), axis=0, keepdims=True)
    r2 = jnp.sum(jnp.where(hit2, prefix, 0.0), axis=0, keepdims=True)
    idx_ref[0:1, :] = i1.astype(jnp.int32)
    idx_ref[1:2, :] = i2.astype(jnp.int32)
    idx_ref[2:3, :] = r1.astype(jnp.int32)
    idx_ref[3:4, :] = r2.astype(jnp.int32)
    base_ref[...] = base_ref[...] + jnp.sum(onehot, axis=1, keepdims=True)
    cnt_ref[...] = base_ref[...].astype(jnp.int32)


def _router(x2, mod3, mod_row, g, rw16, *, tm):
    m, d = x2.shape
    modspec = lambda k: pl.BlockSpec((1, 1, d), lambda i: (mod_row(i), 0, k))
    return pl.pallas_call(
        _router_kernel,
        grid=(m // tm,),
        in_specs=[pl.BlockSpec((tm, d), lambda i: (i, 0)),
                  modspec(3), modspec(4),
                  pl.BlockSpec((1, d), lambda i: (0, 0)),
                  pl.BlockSpec((ROUTER_ROWS, d), lambda i: (0, 0))],
        out_specs=[pl.BlockSpec((tm, d), lambda i: (i, 0)),
                   pl.BlockSpec((4, tm), lambda i: (0, i)),
                   pl.BlockSpec((2, tm), lambda i: (0, i)),
                   pl.BlockSpec((ROUTER_ROWS, 128), lambda i: (0, 0))],
        out_shape=[jax.ShapeDtypeStruct((m, d), F32),
                   jax.ShapeDtypeStruct((4, m), jnp.int32),
                   jax.ShapeDtypeStruct((2, m), F32),
                   jax.ShapeDtypeStruct((ROUTER_ROWS, 128), jnp.int32)],
        scratch_shapes=[pltpu.VMEM((tm, tm), BF16), pltpu.VMEM((ROUTER_ROWS, 128), F32)],
        compiler_params=_cparams(1),
        name="router",
    )(x2, mod3, mod3, g, rw16)


def _row_copy(src_ref, src_row, dst_ref, dst_row, sem):
    return pltpu.make_async_copy(src_ref.at[pl.ds(src_row, 1), :], dst_ref.at[pl.ds(dst_row, 1), :], sem)


DMA_ISSUE_UNROLL = 8


def _scatter_kernel(dest_ref, pad_ref, t_ref, buf_ref, zero_ref, sem, zsem, *, n_tok, n_tiles):
    tm = t_ref.shape[0]
    base = pl.program_id(0) * tm

    @pl.when(pl.program_id(0) == 0)
    def _():
        zero_ref[...] = jnp.zeros(zero_ref.shape, zero_ref.dtype)
        for e in range(N_EXPERTS):
            start, length = pad_ref[e], pad_ref[N_EXPERTS + e]

            def fill(r, carry, start=start):
                _row_copy(zero_ref, 0, buf_ref, start + r, zsem).start()
                return carry

            def drain(r, carry):
                _row_copy(zero_ref, 0, buf_ref, 0, zsem).wait()
                return carry

            lax.fori_loop(0, length, fill, 0)
            lax.fori_loop(0, length, drain, 0)

        def tile_copy(tile):
            row0 = pl.multiple_of(tile * EXPERT_TM, EXPERT_TM)
            return pltpu.make_async_copy(zero_ref, buf_ref.at[pl.ds(row0, EXPERT_TM), :], zsem)

        def fill_tile(tile, carry):
            tile_copy(tile).start()
            return carry

        def drain_tile(tile, carry):
            tile_copy(tile).wait()
            return carry

        lax.fori_loop(pad_ref[2 * N_EXPERTS], n_tiles, fill_tile, 0)
        lax.fori_loop(pad_ref[2 * N_EXPERTS], n_tiles, drain_tile, 0)

    def start(r, carry):
        for k in range(2):
            _row_copy(t_ref, r, buf_ref, dest_ref[k * n_tok + base + r], sem).start()
        return carry

    lax.fori_loop(0, tm, start, 0, unroll=DMA_ISSUE_UNROLL)
    for k in range(2):
        pltpu.make_async_copy(t_ref, buf_ref.at[pl.ds(0, tm), :], sem).wait()


def _scatter_rows(dest, pad_info, t, *, tm, n_tiles):
    m, d = t.shape
    grid_spec = pltpu.PrefetchScalarGridSpec(
        num_scalar_prefetch=2,
        grid=(m // tm,),
        in_specs=[pl.BlockSpec((tm, d), lambda i, dest, pad: (i, 0))],
        out_specs=pl.BlockSpec(memory_space=pl.ANY),
        scratch_shapes=[pltpu.VMEM((EXPERT_TM, d), t.dtype),
                        pltpu.SemaphoreType.DMA(()), pltpu.SemaphoreType.DMA(())],
    )
    return pl.pallas_call(
        functools.partial(_scatter_kernel, n_tok=m, n_tiles=n_tiles),
        grid_spec=grid_spec,
        out_shape=jax.ShapeDtypeStruct((n_tiles * EXPERT_TM, d), t.dtype),
        compiler_params=_cparams(1),
        name="moe_scatter",
    )(dest, pad_info, t)


def _expert_kernel(te_ref, tv_ref, x_ref, wg_ref, wu_ref, wd_ref, o_ref, xb_ref):
    i = pl.program_id(0)
    j = pl.program_id(1)
    valid = tv_ref[i] == 1

    @pl.when(j == 0)
    def _():
        o_ref[...] = jnp.zeros(o_ref.shape, o_ref.dtype)

    @pl.when(valid)
    def _():
        @pl.when(j == 0)
        def _():
            xb_ref[...] = x_ref[...].astype(BF16)

        xb = xb_ref[...]
        gv = jnp.dot(xb, wg_ref[...], preferred_element_type=F32)
        uv = jnp.dot(xb, wu_ref[...], preferred_element_type=F32)
        a = (gv * _sigmoid(gv) * uv).astype(BF16)
        o_ref[...] += jnp.dot(a, wd_ref[...], preferred_element_type=F32)


def _experts(tile_e, tile_v, tile_x, buf, wg, wu, wd, *, tf):
    rows, d = buf.shape
    f = wg.shape[2]
    nf = f // tf
    fcol = lambda i, j, tv: jnp.where(tv[i] == 1, j, nf - 1)
    grid_spec = pltpu.PrefetchScalarGridSpec(
        num_scalar_prefetch=3,
        grid=(rows // EXPERT_TM, nf),
        in_specs=[pl.BlockSpec((EXPERT_TM, d), lambda i, j, te, tv, tx: (tx[i], 0)),
                  pl.BlockSpec((None, d, tf), lambda i, j, te, tv, tx: (te[i], 0, fcol(i, j, tv))),
                  pl.BlockSpec((None, d, tf), lambda i, j, te, tv, tx: (te[i], 0, fcol(i, j, tv))),
                  pl.BlockSpec((None, tf, d), lambda i, j, te, tv, tx: (te[i], fcol(i, j, tv), 0))],
        out_specs=pl.BlockSpec((EXPERT_TM, d), lambda i, j, te, tv, tx: (i, 0)),
        scratch_shapes=[pltpu.VMEM((EXPERT_TM, d), BF16)],
    )

    def kern(te_ref, tv_ref, tx_ref, x_ref, wg_ref, wu_ref, wd_ref, o_ref, xb_ref):
        del tx_ref
        _expert_kernel(te_ref, tv_ref, x_ref, wg_ref, wu_ref, wd_ref, o_ref, xb_ref)

    return pl.pallas_call(
        kern,
        grid_spec=grid_spec,
        out_shape=jax.ShapeDtypeStruct((rows, d), F32),
        compiler_params=_cparams(2),
        name="moe_experts",
    )(tile_e, tile_v, tile_x, buf, wg, wu, wd)


def _combine_kernel(dest_ref, x_ref, gate5_ref, gates_ref, gf_ref, ybuf_ref, o_ref, rows_ref, sem, *, n_tok):
    tm = x_ref.shape[0]
    i = pl.program_id(0)

    def issue(tile, slot):
        base = tile * tm

        def start(r, carry):
            for k in range(2):
                _row_copy(ybuf_ref, dest_ref[k * n_tok + base + r], rows_ref.at[slot, k], r, sem.at[slot]).start()
            return carry

        lax.fori_loop(0, tm, start, 0, unroll=DMA_ISSUE_UNROLL)

    @pl.when(i == 0)
    def _():
        issue(0, 0)

    @pl.when(i + 1 < pl.num_programs(0))
    def _():
        issue(i + 1, (i + 1) % 2)

    slot = i % 2
    for k in range(2):
        pltpu.make_async_copy(ybuf_ref.at[pl.ds(0, tm), :], rows_ref.at[slot, k], sem.at[slot]).wait()

    eye = (lax.broadcasted_iota(jnp.int32, (tm, tm), 0) == lax.broadcasted_iota(jnp.int32, (tm, tm), 1))
    g0 = jnp.sum(jnp.where(eye, gates_ref[0:1, :], 0.0), axis=1, keepdims=True)
    g1 = jnp.sum(jnp.where(eye, gates_ref[1:2, :], 0.0), axis=1, keepdims=True)
    y = g0 * rows_ref[slot, 0] + g1 * rows_ref[slot, 1]
    xn = x_ref[...] + gate5_ref[0] * y
    ms = jnp.mean(xn * xn, axis=-1, keepdims=True)
    o_ref[...] = xn * lax.rsqrt(ms + EPS) * gf_ref[...]


def _combine(dest, x2, mod3, mod_row, gates, g_final, ybuf, *, tm):
    m, d = x2.shape
    grid_spec = pltpu.PrefetchScalarGridSpec(
        num_scalar_prefetch=1,
        grid=(m // tm,),
        in_specs=[pl.BlockSpec((tm, d), lambda i, dest: (i, 0)),
                  pl.BlockSpec((1, 1, d), lambda i, dest: (mod_row(i), 0, 5)),
                  pl.BlockSpec((2, tm), lambda i, dest: (0, i)),
                  pl.BlockSpec((1, d), lambda i, dest: (0, 0)),
                  pl.BlockSpec(memory_space=pl.ANY)],
        out_specs=pl.BlockSpec((tm, d), lambda i, dest: (i, 0)),
        scratch_shapes=[pltpu.VMEM((2, 2, tm, d), F32), pltpu.SemaphoreType.DMA((2,))],
    )
    return pl.pallas_call(
        functools.partial(_combine_kernel, n_tok=m),
        grid_spec=grid_spec,
        out_shape=jax.ShapeDtypeStruct((m, d), F32),
        compiler_params=_cparams(1),
        name="moe_combine",
    )(dest, x2, mod3, gates, g_final, ybuf)


def _rope_tables(seq):
    rows = seq // GRID_W
    row = jnp.repeat(jnp.arange(rows), GRID_W).astype(F32)
    col = jnp.tile(jnp.arange(GRID_W), rows).astype(F32)
    axis_dim = HEAD_DIM // 2
    inv_freq = ROPE_THETA ** (-jnp.arange(0, axis_dim, 2, dtype=F32) / axis_dim)
    ang_r = row[:, None] * inv_freq
    ang_c = col[:, None] * inv_freq
    cr, sr, cc, sc = jnp.cos(ang_r), jnp.sin(ang_r), jnp.cos(ang_c), jnp.sin(ang_c)
    return (jnp.concatenate([cr, cr, cc, cc], axis=-1),
            jnp.concatenate([-sr, sr, -sc, sc], axis=-1))


def kernel(x, c, ctx, c_ctx, w_ada, b_ada, g_mix, w_in, g_q, g_k, sconv_w, conf_dw, conf_db, conf_ln_g,
           conf_ln_b, conf_pw, conf_pb, g_group, w_o, g_ffn, dense_wg, dense_wu, dense_wd, router_w,
           moe_wg, moe_wu, moe_wd, g_final):
    b, s, d = x.shape
    n_ctx = ctx.shape[1]
    depth = w_ada.shape[0]
    assert b + 1 <= MOD_ROWS and depth == 2
    m_lat, m_ctx = b * s, b * n_ctx
    ctx_row = b

    cin = jnp.concatenate([c, c_ctx[None, :], jnp.zeros((MOD_ROWS - b - 1, d), F32)], axis=0)
    mod = _ada(cin, w_ada, b_ada)
    cos_t, sin_t = _rope_tables(s)
    row2 = lambda v: v.reshape(1, -1)

    lat_tm = 512
    lat_row_for = lambda tm: (lambda i: i // (s // tm))
    lat_row = lat_row_for(lat_tm)
    ctx_tm = 512
    ctx_mod_row = lambda i: ctx_row

    x2 = x.reshape(m_lat, d)
    xc2 = ctx.reshape(m_ctx, d)
    out = None
    for l in range(depth):
        last = l == depth - 1
        mod3 = mod[l].reshape(MOD_ROWS, 1, 6 * d)
        w_in_b = w_in[l].astype(BF16)
        w_o_b = w_o[l].astype(BF16)
        pw_b = conf_pw[l].astype(BF16)
        gg = g_group[l]
        gga, ggs, ggc = row2(gg[:ATTN_W]), row2(gg[ATTN_W:ATTN_W + SCONV_W]), row2(gg[ATTN_W + SCONV_W:])
        mixer_args = (sconv_w[l], conf_dw[l], row2(conf_db[l]), row2(conf_ln_g[l]), row2(conf_ln_b[l]),
                      pw_b, row2(conf_pb[l]), ggs, ggc)
        in_args = (row2(g_mix[l]), w_in_b, row2(g_q[l]), row2(g_k[l]), cos_t, sin_t)

        dense_here = l % 2 == 0
        k_dense = l // 2
        cast = ((dense_wg[k_dense:k_dense + 1], dense_wu[k_dense:k_dense + 1], dense_wd[k_dense:k_dense + 1])
                if dense_here else ())
        p_lat, dense_w_b = _in_proj(x2, mod3, lat_row, *in_args, tm=lat_tm, rope=True, j0=0, nj=8, seq=s,
                                    cast=cast)
        if last:
            p_ctx, _ = _in_proj(xc2, mod3, ctx_mod_row, *in_args, tm=ctx_tm, rope=False, j0=2, nj=1, seq=ctx_tm)
            kc_blk = 0
        else:
            p_ctx, _ = _in_proj(xc2, mod3, ctx_mod_row, *in_args, tm=ctx_tm, rope=False, j0=0, nj=8, seq=ctx_tm)
            kc_blk = ATTN_W // KV_W
        att = _attention(p_lat, p_ctx, p_lat, gga, n_batch=b, q_len=s, tq=512, kc_blk=kc_blk,
                         has_latent=True, n_ctx=n_ctx, seq=s)
        mix = _local_mixers(p_lat, *mixer_args, n_seq=b, seq=s)
        x2 = _out_proj(att, mix, w_o_b, x2, mod3, lat_row, tm=lat_tm)
        if not last:
            att_c = _attention(p_ctx, p_ctx, p_ctx, gga, n_batch=b, q_len=n_ctx, tq=n_ctx, kc_blk=kc_blk,
                               has_latent=False, n_ctx=n_ctx, seq=n_ctx)
            mix_c = _local_mixers(p_ctx, *mixer_args, n_seq=b, seq=n_ctx)
            xc2 = _out_proj(att_c, mix_c, w_o_b, xc2, mod3, ctx_mod_row, tm=ctx_tm)

        if l % 2 == 0:
            wg_b, wu_b, wd_b = (w[0] for w in dense_w_b)
            ffn = functools.partial(_dense_ffn, g=row2(g_ffn[l]), wg=wg_b, wu=wu_b, wd=wd_b, tf=512)
            routed_next = l + 1 < depth and (l + 1) % 2 == 1
            cast = (moe_wg[(l + 1) // 2], moe_wu[(l + 1) // 2], moe_wd[(l + 1) // 2]) if routed_next else ()
            x2, moe_w_b = ffn(x2, mod3, lat_row, tm=lat_tm, cast=cast)
            if not last:
                xc2, _ = ffn(xc2, mod3, ctx_mod_row, tm=ctx_tm)
        else:
            assert last, "routed layer is implemented for the final layer (latent tokens only)"
            rw16 = jnp.zeros((ROUTER_ROWS, d), F32).at[:N_EXPERTS].set(router_w[l // 2].T)
            t, idx, gates, cnt = _router(x2, mod3, lat_row, row2(g_ffn[l]), rw16, tm=lat_tm)
            counts = cnt[:N_EXPERTS, 0]
            padded = (counts + EXPERT_TM - 1) // EXPERT_TM * EXPERT_TM
            pends = jnp.cumsum(padded)
            pstarts = pends - padded
            slot_e = idx[0:2]
            slot_start = sum(jnp.where(slot_e == e, pstarts[e], 0) for e in range(N_EXPERTS))
            dest = (slot_start + idx[2:4]).reshape(-1).astype(jnp.int32)
            n_tiles = (2 * m_lat) // EXPERT_TM + N_EXPERTS
            tile_start = jnp.arange(n_tiles, dtype=jnp.int32) * EXPERT_TM
            tile_v = (tile_start < pends[-1]).astype(jnp.int32)
            last_tile = pends[-1] // EXPERT_TM - 1
            tile_x = jnp.minimum(jnp.arange(n_tiles, dtype=jnp.int32), last_tile).astype(jnp.int32)
            tile_e = jnp.minimum(jnp.searchsorted(pends, tile_x * EXPERT_TM, side='right'),
                                 N_EXPERTS - 1).astype(jnp.int32)
            pad_info = jnp.concatenate([pstarts + counts, padded - counts,
                                        (pends[-1:] // EXPERT_TM)]).astype(jnp.int32)
            buf = _scatter_rows(dest, pad_info, t, tm=lat_tm, n_tiles=n_tiles)
            ybuf = _experts(tile_e, tile_v, tile_x, buf, *moe_w_b, tf=512)
            out = _combine(dest, x2, mod3, lat_row_for(256), gates, row2(g_final), ybuf, tm=256)
    return out.reshape(b, s, d)
```

```python
import functools

import jax
import jax.numpy as jnp
from jax import lax
from jax.experimental import pallas as pl
from jax.experimental.pallas import tpu as pltpu

F32 = jnp.float32
BF16 = jnp.bfloat16

GRID_W = 64
HEAD_DIM = 128
N_Q_HEADS = 8
N_KV_HEADS = 2
GQA_GROUP = N_Q_HEADS // N_KV_HEADS
ATTN_W = N_Q_HEADS * HEAD_DIM
KV_W = N_KV_HEADS * HEAD_DIM
ROPE_THETA = 10000.0
SCONV_W = 512
SCONV_K = 3
CONF_W = 512
CONF_K = 31
N_EXPERTS = 8
EPS = 1e-6
LOG2E = 1.4426950408889634

V7X_VMEM_LIMIT_BYTES = 56 * 1024 * 1024
MOD_ROWS = 16
IN_TN = 512
EXPERT_TM = 512


def _cparams(n_axes):
    return pltpu.CompilerParams(dimension_semantics=("arbitrary",) * n_axes,
                                vmem_limit_bytes=V7X_VMEM_LIMIT_BYTES)


def _sigmoid(x):
    return 1.0 / (1.0 + jnp.exp(-x))


NORM_CHUNK = 16
NORM_UNROLL = 4


def _norm_modulate_rows(x_ref, h_ref, g, shift, scale):
    rows = x_ref.shape[0]
    gain = g * (1.0 + scale)

    def body(c, carry):
        r = pl.multiple_of(c * NORM_CHUNK, NORM_CHUNK)
        x = x_ref[pl.ds(r, NORM_CHUNK), :]
        inv = lax.rsqrt(jnp.mean(x * x, axis=-1, keepdims=True) + EPS)
        h_ref[pl.ds(r, NORM_CHUNK), :] = (x * inv * gain + shift).astype(h_ref.dtype)
        return carry

    lax.fori_loop(0, rows // NORM_CHUNK, body, 0, unroll=NORM_UNROLL)


def _ada_kernel(c_ref, w_ref, b_ref, o_ref):
    a = c_ref[...]
    a = (a * _sigmoid(a)).astype(BF16)
    o_ref[0] = jnp.dot(a, w_ref[0].astype(BF16), preferred_element_type=F32) + b_ref[0]


def _ada(cin, w_ada, b_ada):
    depth, d, n = w_ada.shape
    tn = 1536
    return pl.pallas_call(
        _ada_kernel,
        grid=(depth, n // tn),
        in_specs=[pl.BlockSpec((MOD_ROWS, d), lambda l, j: (0, 0)),
                  pl.BlockSpec((1, d, tn), lambda l, j: (l, 0, j)),
                  pl.BlockSpec((1, 1, tn), lambda l, j: (l, 0, j))],
        out_specs=pl.BlockSpec((1, MOD_ROWS, tn), lambda l, j: (l, 0, j)),
        out_shape=jax.ShapeDtypeStruct((depth, MOD_ROWS, n), F32),
        compiler_params=_cparams(2),
        name="ada",
    )(cin, w_ada, b_ada.reshape(depth, 1, n))


def _head_norm(seg, gain):
    ms = jnp.mean(seg * seg, axis=-1, keepdims=True)
    return seg * lax.rsqrt(ms + EPS) * gain


def _rope(n, cos_t, sin_t):
    lane = lax.broadcasted_iota(jnp.int32, n.shape, 1)
    fwd = pltpu.roll(n, 32, 1)
    bwd = pltpu.roll(n, 96, 1)
    partner = jnp.where((lane // 32) % 2 == 0, bwd, fwd)
    return n * cos_t + partner * sin_t


def _in_kernel(x_ref, shift_ref, scale_ref, g_ref, w_ref, gq_ref, gk_ref, cos_ref, sin_ref,
               *rest, rope, j0, nj, q_scale, n_cast):
    cast_in, o_ref, cast_out, h_ref = rest[:n_cast], rest[n_cast], rest[n_cast + 1:2 * n_cast + 1], rest[-1]
    _norm_modulate_rows(x_ref, h_ref, g_ref[...], shift_ref[0], scale_ref[0])
    _cast_slabs(cast_in, cast_out)
    h = h_ref[...]

    def head(seg, gain, scale):
        n = _head_norm(seg, gain)
        if rope:
            n = _rope(n, cos_ref[...], sin_ref[...])
        if scale != 1.0:
            n = n * scale
        return n

    for t in range(nj):
        j = t + j0
        c0 = t * IN_TN
        acc = jnp.dot(h, w_ref[:, c0:c0 + IN_TN], preferred_element_type=F32)
        if j < 2:
            for hd in range(IN_TN // HEAD_DIM):
                sl = slice(hd * HEAD_DIM, (hd + 1) * HEAD_DIM)
                osl = slice(c0 + hd * HEAD_DIM, c0 + (hd + 1) * HEAD_DIM)
                o_ref[:, osl] = head(acc[:, sl], gq_ref[...], q_scale).astype(o_ref.dtype)
        elif j == 2:
            for hd in range(N_KV_HEADS):
                sl = slice(hd * HEAD_DIM, (hd + 1) * HEAD_DIM)
                osl = slice(c0 + hd * HEAD_DIM, c0 + (hd + 1) * HEAD_DIM)
                o_ref[:, osl] = head(acc[:, sl], gk_ref[...], 1.0).astype(o_ref.dtype)
            o_ref[:, c0 + KV_W:c0 + IN_TN] = acc[:, KV_W:].astype(o_ref.dtype)
        else:
            o_ref[:, c0:c0 + IN_TN] = acc.astype(o_ref.dtype)


def _const_spec(shape, idx):
    return pl.BlockSpec(shape, lambda i: idx, pipeline_mode=pl.Buffered(1))


def _in_proj(x2, mod3, mod_row, g, w, gq, gk, cos_t, sin_t, *, tm, rope, j0, nj, seq, cast=()):
    m, d = x2.shape
    tiles_per_seq = seq // tm
    ncols = nj * IN_TN
    assert j0 % nj == 0
    q_scale = HEAD_DIM ** -0.5 * LOG2E
    kern = functools.partial(_in_kernel, rope=rope, j0=j0, nj=nj, q_scale=q_scale, n_cast=len(cast))
    cast_arrays, cast_in, cast_out, cast_shapes = _cast_plan(cast, m // tm, lambda i: i)
    outs = pl.pallas_call(
        kern,
        grid=(m // tm,),
        in_specs=[pl.BlockSpec((tm, d), lambda i: (i, 0)),
                  pl.BlockSpec((1, 1, d), lambda i: (mod_row(i), 0, 0)),
                  pl.BlockSpec((1, 1, d), lambda i: (mod_row(i), 0, 1)),
                  _const_spec((1, d), (0, 0)),
                  _const_spec((d, ncols), (0, j0 // nj)),
                  _const_spec((1, HEAD_DIM), (0, 0)),
                  _const_spec((1, HEAD_DIM), (0, 0)),
                  pl.BlockSpec((tm, HEAD_DIM), lambda i: (i % tiles_per_seq, 0)),
                  pl.BlockSpec((tm, HEAD_DIM), lambda i: (i % tiles_per_seq, 0))] + cast_in,
        out_specs=[pl.BlockSpec((tm, ncols), lambda i: (i, 0))] + cast_out,
        out_shape=[jax.ShapeDtypeStruct((m, ncols), BF16)] + cast_shapes,
        scratch_shapes=[pltpu.VMEM((tm, d), BF16)],
        compiler_params=_cparams(1),
        name="in_proj",
    )(x2, mod3, mod3, g, w, gq, gk, cos_t, sin_t, *cast_arrays)
    return outs[0], tuple(outs[1:])


def _attn_kernel(q_ref, kc_ref, vc_ref, kl_ref, vl_ref, gg_ref, *rest, has_latent, n_cast):
    cast_in, o_ref, cast_out, acc_ref = rest[:n_cast], rest[n_cast], rest[n_cast + 1:2 * n_cast + 1], rest[-1]
    _cast_slabs(cast_in, cast_out)
    tq = q_ref.shape[0]
    dn = (((1,), (1,)), ((), ()))
    ssq = jnp.zeros((tq, 1), F32)
    for hd in range(N_Q_HEADS):
        kv = hd // GQA_GROUP
        ksl = slice(kv * HEAD_DIM, (kv + 1) * HEAD_DIM)
        q = q_ref[:, hd * HEAD_DIM:(hd + 1) * HEAD_DIM]
        s_c = lax.dot_general(q, kc_ref[:, ksl], dn, preferred_element_type=F32)
        mx = jnp.max(s_c, axis=-1, keepdims=True)
        if has_latent:
            s_l = lax.dot_general(q, kl_ref[:, ksl], dn, preferred_element_type=F32)
            mx = jnp.maximum(mx, jnp.max(s_l, axis=-1, keepdims=True))
        p_c = jnp.exp2(s_c - mx)
        den = jnp.sum(p_c, axis=-1, keepdims=True)
        o = jnp.dot(p_c.astype(BF16), vc_ref[:, ksl], preferred_element_type=F32)
        if has_latent:
            p_l = jnp.exp2(s_l - mx)
            den = den + jnp.sum(p_l, axis=-1, keepdims=True)
            o = o + jnp.dot(p_l.astype(BF16), vl_ref[:, ksl], preferred_element_type=F32)
        o = o * (1.0 / den)
        ssq = ssq + jnp.sum(o * o, axis=-1, keepdims=True)
        acc_ref[:, hd * HEAD_DIM:(hd + 1) * HEAD_DIM] = o
    inv = lax.rsqrt(ssq * (1.0 / ATTN_W) + EPS)
    o_ref[...] = (acc_ref[...] * inv * gg_ref[...]).astype(o_ref.dtype)


def _attention(qsrc, csrc, lsrc, gg, *, n_batch, q_len, tq, kc_blk, has_latent, n_ctx, seq, cast=()):
    m = qsrc.shape[0]
    tiles = q_len // tq
    kern = functools.partial(_attn_kernel, has_latent=has_latent, n_cast=len(cast))
    cast_arrays, cast_in, cast_out, cast_shapes = _cast_plan(cast, n_batch * tiles, lambda b, i: b * tiles + i)
    outs = pl.pallas_call(
        kern,
        grid=(n_batch, tiles),
        in_specs=[pl.BlockSpec((tq, ATTN_W), lambda b, i: (b * tiles + i, 0)),
                  pl.BlockSpec((n_ctx, KV_W), lambda b, i: (b, kc_blk)),
                  pl.BlockSpec((n_ctx, KV_W), lambda b, i: (b, kc_blk + 1)),
                  pl.BlockSpec((seq, KV_W), lambda b, i: (b, ATTN_W // KV_W)),
                  pl.BlockSpec((seq, KV_W), lambda b, i: (b, ATTN_W // KV_W + 1)),
                  pl.BlockSpec((1, ATTN_W), lambda b, i: (0, 0))] + cast_in,
        out_specs=[pl.BlockSpec((tq, ATTN_W), lambda b, i: (b * tiles + i, 0))] + cast_out,
        out_shape=[jax.ShapeDtypeStruct((m, ATTN_W), BF16)] + cast_shapes,
        scratch_shapes=[pltpu.VMEM((tq, ATTN_W), F32)],
        compiler_params=_cparams(2),
        name="attention",
    )(qsrc, csrc, csrc, lsrc, lsrc, gg, *cast_arrays)
    return outs[0], tuple(outs[1:])


MIX_CHUNK = 64
SCONV_PAD = 8
CONF_PAD = 16


LANES = 128
SUBLANES = 8
N_LANE_GROUPS = SCONV_W // LANES


def _window_conv(p_ref, slot, w_ref, group, first, n_taps, ch):
    lanes = slice(group * LANES, (group + 1) * LANES)
    win = p_ref[slot, :, lanes]
    n_rows = win.shape[0]
    acc = None
    for phase in range(SUBLANES):
        taps = [k for k in range(n_taps) if (first + k) % SUBLANES == phase]
        if not taps:
            continue
        shifted = win if phase == 0 else pltpu.roll(win, n_rows - phase, 0)
        for k in taps:
            base = first + k - phase
            term = w_ref[k:k + 1, lanes] * shifted[base:base + ch]
            acc = term if acc is None else acc + term
    return acc


def _mix_kernel(sb_ref, sg_ref, sx_ref, ga_ref, gb_ref, sw_ref, dw_ref, db_ref, lng_ref, lnb_ref,
                pw_ref, pb_ref, ggs_ref, ggc_ref, o_ref, p1_ref, p2_ref, a_ref):
    seq = sb_ref.shape[0]
    ch = MIX_CHUNK
    n_chunks = seq // ch
    hs, hc = SCONV_PAD, CONF_PAD
    p1_ref[1, 0:hs, :] = jnp.zeros((hs, SCONV_W), F32)
    p1_ref[n_chunks, ch + hs:ch + 2 * hs, :] = jnp.zeros((hs, SCONV_W), F32)
    p2_ref[1, 0:hc, :] = jnp.zeros((hc, CONF_W), F32)
    p2_ref[n_chunks, ch + hc:ch + 2 * hc, :] = jnp.zeros((hc, CONF_W), F32)

    def fill(c, carry):
        r = pl.multiple_of(c * ch, ch)
        rows = pl.ds(r, ch)
        v = sg_ref[rows, :].astype(F32) * sx_ref[rows, :].astype(F32)
        p1_ref[c + 1, hs:hs + ch, :] = v
        p1_ref[c, ch + hs:ch + 2 * hs, :] = v[0:hs]
        p1_ref[c + 2, 0:hs, :] = v[ch - hs:ch]
        u = ga_ref[rows, :].astype(F32) * _sigmoid(gb_ref[rows, :].astype(F32))
        p2_ref[c + 1, hc:hc + ch, :] = u
        p2_ref[c, ch + hc:ch + 2 * hc, :] = u[0:hc]
        p2_ref[c + 2, 0:hc, :] = u[ch - hc:ch]
        return carry

    lax.fori_loop(0, n_chunks, fill, 0)

    def conv(c, carry):
        r = pl.multiple_of(c * ch, ch)
        rows = pl.ds(r, ch)
        acc = jnp.concatenate(
            [_window_conv(p1_ref, c + 1, sw_ref, g, hs - SCONV_K // 2, SCONV_K, ch) for g in range(N_LANE_GROUPS)],
            axis=1)
        ys = sb_ref[rows, :].astype(F32) * acc
        ms = jnp.mean(ys * ys, axis=-1, keepdims=True)
        o_ref[rows, 0:SCONV_W] = (ys * lax.rsqrt(ms + EPS) * ggs_ref[...]).astype(o_ref.dtype)

        u = db_ref[...] + jnp.concatenate(
            [_window_conv(p2_ref, c + 1, dw_ref, g, hc - CONF_K // 2, CONF_K, ch) for g in range(N_LANE_GROUPS)],
            axis=1)
        mu = jnp.mean(u, axis=-1, keepdims=True)
        uc = u - mu
        var = jnp.mean(uc * uc, axis=-1, keepdims=True)
        v = uc * lax.rsqrt(var + EPS) * lng_ref[...] + lnb_ref[...]
        a_ref[rows, :] = (v * _sigmoid(v)).astype(a_ref.dtype)
        return carry

    lax.fori_loop(0, n_chunks, conv, 0)

    yc = jnp.dot(a_ref[...], pw_ref[...], preferred_element_type=F32) + pb_ref[...]
    ms = jnp.mean(yc * yc, axis=-1, keepdims=True)
    o_ref[:, SCONV_W:] = (yc * lax.rsqrt(ms + EPS) * ggc_ref[...]).astype(o_ref.dtype)


def _local_mixers(src, sconv_w, conf_dw, conf_db, ln_g, ln_b, pw, pb, ggs, ggc, *, n_seq, seq):
    w = SCONV_W
    col = lambda k: pl.BlockSpec((seq, w), lambda b: (b, k))
    vec = lambda n: pl.BlockSpec((1, n), lambda b: (0, 0))
    return pl.pallas_call(
        _mix_kernel,
        grid=(n_seq,),
        in_specs=[col(3), col(4), col(5), col(6), col(7),
                  pl.BlockSpec((SCONV_K, w), lambda b: (0, 0)),
                  pl.BlockSpec((CONF_K, w), lambda b: (0, 0)),
                  vec(w), vec(w), vec(w),
                  pl.BlockSpec((w, w), lambda b: (0, 0)),
                  vec(w), vec(w), vec(w)],
        out_specs=pl.BlockSpec((seq, 2 * w), lambda b: (b, 0)),
        out_shape=jax.ShapeDtypeStruct((src.shape[0], 2 * w), BF16),
        scratch_shapes=[pltpu.VMEM((seq // MIX_CHUNK + 2, MIX_CHUNK + 2 * SCONV_PAD, w), F32),
                        pltpu.VMEM((seq // MIX_CHUNK + 2, MIX_CHUNK + 2 * CONF_PAD, w), F32),
                        pltpu.VMEM((seq, w), BF16)],
        compiler_params=_cparams(1),
        name="local_mixers",
    )(src, src, src, src, src, sconv_w, conf_dw, conf_db, ln_g, ln_b, pw, pb, ggs, ggc)


OUT_TN = 512


def _out_kernel(a_ref, m_ref, wa_ref, wm_ref, x_ref, gate_ref, o_ref):
    a = a_ref[...]
    mx = m_ref[...]
    for t in range(o_ref.shape[1] // OUT_TN):
        sl = slice(t * OUT_TN, (t + 1) * OUT_TN)
        acc = (jnp.dot(a, wa_ref[:, sl], preferred_element_type=F32)
               + jnp.dot(mx, wm_ref[:, sl], preferred_element_type=F32))
        o_ref[:, sl] = x_ref[:, sl] + gate_ref[0, :, sl] * acc


def _out_proj(att, mix, w, x2, mod3, mod_row, *, tm):
    m, d = x2.shape
    half = att.shape[1]
    return pl.pallas_call(
        _out_kernel,
        grid=(m // tm,),
        in_specs=[pl.BlockSpec((tm, half), lambda i: (i, 0)),
                  pl.BlockSpec((tm, half), lambda i: (i, 0)),
                  _const_spec((half, d), (0, 0)),
                  _const_spec((half, d), (1, 0)),
                  pl.BlockSpec((tm, d), lambda i: (i, 0)),
                  pl.BlockSpec((1, 1, d), lambda i: (mod_row(i), 0, 2))],
        out_specs=pl.BlockSpec((tm, d), lambda i: (i, 0)),
        out_shape=jax.ShapeDtypeStruct((m, d), F32),
        compiler_params=_cparams(1),
        name="out_proj",
    )(att, mix, w, w, x2, mod3)


def _swiglu_accumulate(h, wg_ref, wu_ref, wd_refs, o_ref):
    tf = wg_ref.shape[1]
    halves = (slice(0, tf // 2), slice(tf // 2, tf))
    ups = [(jnp.dot(h, wg_ref[:, sl], preferred_element_type=F32),
            jnp.dot(h, wu_ref[:, sl], preferred_element_type=F32)) for sl in halves]
    acts = [(gv * _sigmoid(gv) * uv).astype(BF16) for gv, uv in ups]
    col = 0
    for wd_ref in wd_refs:
        width = wd_ref.shape[1]
        o_ref[:, col:col + width] += (jnp.dot(acts[0], wd_ref[halves[0], :], preferred_element_type=F32)
                                      + jnp.dot(acts[1], wd_ref[halves[1], :], preferred_element_type=F32))
        col += width


def _ffn_kernel(x_ref, shift_ref, scale_ref, gate_ref, g_ref, wg_ref, wu_ref, wd_ref, *rest, n_cast):
    cast_in, o_ref, cast_out, h_ref = rest[:n_cast], rest[n_cast], rest[n_cast + 1:2 * n_cast + 1], rest[-1]
    j = pl.program_id(1)

    _cast_slabs(cast_in, cast_out)

    @pl.when(j == 0)
    def _():
        _norm_modulate_rows(x_ref, h_ref, g_ref[...], shift_ref[0], scale_ref[0])
        o_ref[...] = jnp.zeros(o_ref.shape, o_ref.dtype)

    _swiglu_accumulate(h_ref[...], wg_ref, wu_ref, (wd_ref,), o_ref)

    @pl.when(j == pl.num_programs(1) - 1)
    def _():
        o_ref[...] = x_ref[...] + gate_ref[0] * o_ref[...]


BF16_SUBLANE_TILE = 16


def _slab_rows(rows, n_lead, n_steps):
    for height in range(BF16_SUBLANE_TILE, rows + 1, BF16_SUBLANE_TILE):
        if rows % height == 0 and n_lead * (rows // height) <= n_steps:
            return height
    raise ValueError(f"no slab height for rows={rows}, n_lead={n_lead}, n_steps={n_steps}")


def _cast_plan(jobs, n_steps, step_of):
    arrays, in_specs, out_specs, out_shapes = [], [], [], []
    for arr, col_block, n_col_blocks in jobs:
        n_lead, rows, cols = arr.shape
        width = cols // n_col_blocks
        height = _slab_rows(rows, n_lead, n_steps)
        per_lead = rows // height
        last = n_lead * per_lead - 1

        def index(*grid_idx, per_lead=per_lead, last=last, col=0):
            s = jnp.minimum(step_of(*grid_idx), last)
            return (s // per_lead, s % per_lead, col)

        arrays.append(arr)
        in_specs.append(pl.BlockSpec((1, height, width), functools.partial(index, col=col_block)))
        out_specs.append(pl.BlockSpec((1, height, width), index))
        out_shapes.append(jax.ShapeDtypeStruct((n_lead, rows, width), BF16))
    return arrays, in_specs, out_specs, out_shapes


def _cast_slabs(cast_in, cast_out):
    for src, dst in zip(cast_in, cast_out):
        dst[...] = src[...].astype(dst.dtype)


def _dense_ffn(x2, mod3, mod_row, g, wg, wu, wd, *, tm, tf, cast=()):
    m, d = x2.shape
    f = wg.shape[1]
    nf = f // tf
    n_steps = (m // tm) * nf
    modspec = lambda k: pl.BlockSpec((1, 1, d), lambda i, j: (mod_row(i), 0, k))
    cast_arrays, cast_in, cast_out, cast_shapes = _cast_plan(cast, n_steps, lambda i, j: i * nf + j)
    outs = pl.pallas_call(
        functools.partial(_ffn_kernel, n_cast=len(cast)),
        grid=(m // tm, nf),
        in_specs=[pl.BlockSpec((tm, d), lambda i, j: (i, 0)),
                  modspec(3), modspec(4), modspec(5),
                  pl.BlockSpec((1, d), lambda i, j: (0, 0)),
                  pl.BlockSpec((d, tf), lambda i, j: (0, j)),
                  pl.BlockSpec((d, tf), lambda i, j: (0, j)),
                  pl.BlockSpec((tf, d), lambda i, j: (j, 0))] + cast_in,
        out_specs=[pl.BlockSpec((tm, d), lambda i, j: (i, 0))] + cast_out,
        out_shape=[jax.ShapeDtypeStruct((m, d), F32)] + cast_shapes,
        scratch_shapes=[pltpu.VMEM((tm, d), BF16)],
        compiler_params=_cparams(2),
        name="dense_ffn",
    )(x2, mod3, mod3, mod3, g, wg, wu, wd, *cast_arrays)
    return outs[0], tuple(outs[1:])


ROUTER_ROWS = 16


def _router_kernel(x_ref, shift_ref, scale_ref, g_ref, rw_ref, t_ref, idx_ref, gate_ref, cnt_ref,
                   tri_ref, base_ref):
    tm = x_ref.shape[0]

    @pl.when(pl.program_id(0) == 0)
    def _():
        r = lax.broadcasted_iota(jnp.int32, (tm, tm), 0)
        c = lax.broadcasted_iota(jnp.int32, (tm, tm), 1)
        tri_ref[...] = jnp.where(r < c, 1.0, 0.0).astype(BF16)
        base_ref[...] = jnp.zeros(base_ref.shape, F32)

    _norm_modulate_rows(x_ref, t_ref, g_ref[...], shift_ref[0], scale_ref[0])

    t = t_ref[...]
    t_hi = t.astype(BF16)
    t_lo = (t - t_hi.astype(F32)).astype(BF16)
    w = rw_ref[...]
    w_hi = w.astype(BF16)
    w_lo = (w - w_hi.astype(F32)).astype(BF16)
    dn = (((1,), (1,)), ((), ()))
    logits = (lax.dot_general(w_hi, t_hi, dn, preferred_element_type=F32)
              + lax.dot_general(w_lo, t_hi, dn, preferred_element_type=F32)
              + lax.dot_general(w_hi, t_lo, dn, preferred_element_type=F32))

    e = lax.broadcasted_iota(jnp.int32, (ROUTER_ROWS, tm), 0).astype(F32)
    neg = jnp.float32(-jnp.inf)
    lg = jnp.where(e < N_EXPERTS, logits, neg)
    m1 = jnp.max(lg, axis=0, keepdims=True)
    i1 = jnp.min(jnp.where(lg == m1, e, float(ROUTER_ROWS)), axis=0, keepdims=True)
    lg2 = jnp.where(e == i1, neg, lg)
    m2 = jnp.max(lg2, axis=0, keepdims=True)
    i2 = jnp.min(jnp.where(lg2 == m2, e, float(ROUTER_ROWS)), axis=0, keepdims=True)
    ex = jnp.exp(m2 - m1)
    den = 1.0 + ex
    gate_ref[0:1, :] = 1.0 / den
    gate_ref[1:2, :] = ex / den

    hit1 = e == i1
    hit2 = e == i2
    onehot = jnp.where(hit1 | hit2, 1.0, 0.0)
    prefix = jnp.dot(onehot.astype(BF16), tri_ref[...], preferred_element_type=F32) + base_ref[:, 0:1]
    r1 = jnp.sum(jnp.where(hit1, prefix, 0.0), axis=0, keepdims=True)
    r2 = jnp.sum(jnp.where(hit2, prefix, 0.0), axis=0, keepdims=True)
    idx_ref[0:1, :] = i1.astype(jnp.int32)
    idx_ref[1:2, :] = i2.astype(jnp.int32)
    idx_ref[2:3, :] = r1.astype(jnp.int32)
    idx_ref[3:4, :] = r2.astype(jnp.int32)
    base_ref[...] = base_ref[...] + jnp.sum(onehot, axis=1, keepdims=True)
    cnt_ref[...] = base_ref[...].astype(jnp.int32)


def _router(x2, mod3, mod_row, g, rw16, *, tm):
    m, d = x2.shape
    modspec = lambda k: pl.BlockSpec((1, 1, d), lambda i: (mod_row(i), 0, k))
    return pl.pallas_call(
        _router_kernel,
        grid=(m // tm,),
        in_specs=[pl.BlockSpec((tm, d), lambda i: (i, 0)),
                  modspec(3), modspec(4),
                  pl.BlockSpec((1, d), lambda i: (0, 0)),
                  pl.BlockSpec((ROUTER_ROWS, d), lambda i: (0, 0))],
        out_specs=[pl.BlockSpec((tm, d), lambda i: (i, 0)),
                   pl.BlockSpec((4, tm), lambda i: (0, i)),
                   pl.BlockSpec((2, tm), lambda i: (0, i)),
                   pl.BlockSpec((ROUTER_ROWS, 128), lambda i: (0, 0))],
        out_shape=[jax.ShapeDtypeStruct((m, d), F32),
                   jax.ShapeDtypeStruct((4, m), jnp.int32),
                   jax.ShapeDtypeStruct((2, m), F32),
                   jax.ShapeDtypeStruct((ROUTER_ROWS, 128), jnp.int32)],
        scratch_shapes=[pltpu.VMEM((tm, tm), BF16), pltpu.VMEM((ROUTER_ROWS, 128), F32)],
        compiler_params=_cparams(1),
        name="router",
    )(x2, mod3, mod3, g, rw16)


def _row_copy(src_ref, src_row, dst_ref, dst_row, sem):
    return pltpu.make_async_copy(src_ref.at[pl.ds(src_row, 1), :], dst_ref.at[pl.ds(dst_row, 1), :], sem)


DMA_ISSUE_UNROLL = 8


def _scatter_kernel(dest_ref, pad_ref, t_ref, buf_ref, zero_ref, sem, zsem, *, n_tok, n_tiles):
    tm = t_ref.shape[0]
    base = pl.program_id(0) * tm

    @pl.when(pl.program_id(0) == 0)
    def _():
        zero_ref[...] = jnp.zeros(zero_ref.shape, zero_ref.dtype)
        for e in range(N_EXPERTS):
            start, length = pad_ref[e], pad_ref[N_EXPERTS + e]

            def fill(r, carry, start=start):
                _row_copy(zero_ref, 0, buf_ref, start + r, zsem).start()
                return carry

            def drain(r, carry):
                _row_copy(zero_ref, 0, buf_ref, 0, zsem).wait()
                return carry

            lax.fori_loop(0, length, fill, 0)
            lax.fori_loop(0, length, drain, 0)

        def tile_copy(tile):
            row0 = pl.multiple_of(tile * EXPERT_TM, EXPERT_TM)
            return pltpu.make_async_copy(zero_ref, buf_ref.at[pl.ds(row0, EXPERT_TM), :], zsem)

        def fill_tile(tile, carry):
            tile_copy(tile).start()
            return carry

        def drain_tile(tile, carry):
            tile_copy(tile).wait()
            return carry

        lax.fori_loop(pad_ref[2 * N_EXPERTS], n_tiles, fill_tile, 0)
        lax.fori_loop(pad_ref[2 * N_EXPERTS], n_tiles, drain_tile, 0)

    def start(r, carry):
        for k in range(2):
            _row_copy(t_ref, r, buf_ref, dest_ref[k * n_tok + base + r], sem).start()
        return carry

    lax.fori_loop(0, tm, start, 0, unroll=DMA_ISSUE_UNROLL)
    for k in range(2):
        pltpu.make_async_copy(t_ref, buf_ref.at[pl.ds(0, tm), :], sem).wait()


def _scatter_rows(dest, pad_info, t, *, tm, n_tiles):
    m, d = t.shape
    grid_spec = pltpu.PrefetchScalarGridSpec(
        num_scalar_prefetch=2,
        grid=(m // tm,),
        in_specs=[pl.BlockSpec((tm, d), lambda i, dest, pad: (i, 0))],
        out_specs=pl.BlockSpec(memory_space=pl.ANY),
        scratch_shapes=[pltpu.VMEM((EXPERT_TM, d), t.dtype),
                        pltpu.SemaphoreType.DMA(()), pltpu.SemaphoreType.DMA(())],
    )
    return pl.pallas_call(
        functools.partial(_scatter_kernel, n_tok=m, n_tiles=n_tiles),
        grid_spec=grid_spec,
        out_shape=jax.ShapeDtypeStruct((n_tiles * EXPERT_TM, d), t.dtype),
        compiler_params=_cparams(1),
        name="moe_scatter",
    )(dest, pad_info, t)


def _expert_kernel(te_ref, tv_ref, x_ref, wg_ref, wu_ref, wd_refs, o_ref, xb_ref):
    i = pl.program_id(0)
    j = pl.program_id(1)
    valid = tv_ref[i] == 1

    @pl.when(j == 0)
    def _():
        o_ref[...] = jnp.zeros(o_ref.shape, o_ref.dtype)

    @pl.when(valid)
    def _():
        @pl.when(j == 0)
        def _():
            xb_ref[...] = x_ref[...].astype(BF16)

        _swiglu_accumulate(xb_ref[...], wg_ref, wu_ref, wd_refs, o_ref)


def _experts(tile_e, tile_v, tile_x, buf, wg, wu, wds, *, tf):
    rows, d = buf.shape
    f = wg.shape[2]
    nf = f // tf
    fcol = lambda i, j, tv: jnp.where(tv[i] == 1, j, nf - 1)
    grid_spec = pltpu.PrefetchScalarGridSpec(
        num_scalar_prefetch=3,
        grid=(rows // EXPERT_TM, nf),
        in_specs=[pl.BlockSpec((EXPERT_TM, d), lambda i, j, te, tv, tx: (tx[i], 0)),
                  pl.BlockSpec((None, d, tf), lambda i, j, te, tv, tx: (te[i], 0, fcol(i, j, tv))),
                  pl.BlockSpec((None, d, tf), lambda i, j, te, tv, tx: (te[i], 0, fcol(i, j, tv)))]
                 + [pl.BlockSpec((None, tf, wd.shape[2]), lambda i, j, te, tv, tx: (te[i], fcol(i, j, tv), 0))
                    for wd in wds],
        out_specs=pl.BlockSpec((EXPERT_TM, d), lambda i, j, te, tv, tx: (i, 0)),
        scratch_shapes=[pltpu.VMEM((EXPERT_TM, d), BF16)],
    )
    n_wd = len(wds)

    def kern(te_ref, tv_ref, tx_ref, x_ref, wg_ref, wu_ref, *rest):
        del tx_ref
        _expert_kernel(te_ref, tv_ref, x_ref, wg_ref, wu_ref, rest[:n_wd], rest[n_wd], rest[n_wd + 1])

    return pl.pallas_call(
        kern,
        grid_spec=grid_spec,
        out_shape=jax.ShapeDtypeStruct((rows, d), F32),
        compiler_params=_cparams(2),
        name="moe_experts",
    )(tile_e, tile_v, tile_x, buf, wg, wu, *wds)


def _combine_kernel(dest_ref, x_ref, gate5_ref, gates_ref, gf_ref, ybuf_ref, o_ref, rows_ref, sem, *, n_tok):
    tm = x_ref.shape[0]
    i = pl.program_id(0)

    def issue(tile, slot):
        base = tile * tm

        def start(r, carry):
            for k in range(2):
                _row_copy(ybuf_ref, dest_ref[k * n_tok + base + r], rows_ref.at[slot, k], r, sem.at[slot]).start()
            return carry

        lax.fori_loop(0, tm, start, 0, unroll=DMA_ISSUE_UNROLL)

    @pl.when(i == 0)
    def _():
        issue(0, 0)

    @pl.when(i + 1 < pl.num_programs(0))
    def _():
        issue(i + 1, (i + 1) % 2)

    slot = i % 2
    for k in range(2):
        pltpu.make_async_copy(ybuf_ref.at[pl.ds(0, tm), :], rows_ref.at[slot, k], sem.at[slot]).wait()

    eye = (lax.broadcasted_iota(jnp.int32, (tm, tm), 0) == lax.broadcasted_iota(jnp.int32, (tm, tm), 1))
    g0 = jnp.sum(jnp.where(eye, gates_ref[0:1, :], 0.0), axis=1, keepdims=True)
    g1 = jnp.sum(jnp.where(eye, gates_ref[1:2, :], 0.0), axis=1, keepdims=True)
    y = g0 * rows_ref[slot, 0] + g1 * rows_ref[slot, 1]
    xn = x_ref[...] + gate5_ref[0] * y
    ms = jnp.mean(xn * xn, axis=-1, keepdims=True)
    o_ref[...] = xn * lax.rsqrt(ms + EPS) * gf_ref[...]


def _combine(dest, x2, mod3, mod_row, gates, g_final, ybuf, *, tm):
    m, d = x2.shape
    grid_spec = pltpu.PrefetchScalarGridSpec(
        num_scalar_prefetch=1,
        grid=(m // tm,),
        in_specs=[pl.BlockSpec((tm, d), lambda i, dest: (i, 0)),
                  pl.BlockSpec((1, 1, d), lambda i, dest: (mod_row(i), 0, 5)),
                  pl.BlockSpec((2, tm), lambda i, dest: (0, i)),
                  pl.BlockSpec((1, d), lambda i, dest: (0, 0)),
                  pl.BlockSpec(memory_space=pl.ANY)],
        out_specs=pl.BlockSpec((tm, d), lambda i, dest: (i, 0)),
        scratch_shapes=[pltpu.VMEM((2, 2, tm, d), F32), pltpu.SemaphoreType.DMA((2,))],
    )
    return pl.pallas_call(
        functools.partial(_combine_kernel, n_tok=m),
        grid_spec=grid_spec,
        out_shape=jax.ShapeDtypeStruct((m, d), F32),
        compiler_params=_cparams(1),
        name="moe_combine",
    )(dest, x2, mod3, gates, g_final, ybuf)


def _rope_tables(seq):
    rows = seq // GRID_W
    row = jnp.repeat(jnp.arange(rows), GRID_W).astype(F32)
    col = jnp.tile(jnp.arange(GRID_W), rows).astype(F32)
    axis_dim = HEAD_DIM // 2
    inv_freq = ROPE_THETA ** (-jnp.arange(0, axis_dim, 2, dtype=F32) / axis_dim)
    ang_r = row[:, None] * inv_freq
    ang_c = col[:, None] * inv_freq
    cr, sr, cc, sc = jnp.cos(ang_r), jnp.sin(ang_r), jnp.cos(ang_c), jnp.sin(ang_c)
    return (jnp.concatenate([cr, cr, cc, cc], axis=-1),
            jnp.concatenate([-sr, sr, -sc, sc], axis=-1))


def kernel(x, c, ctx, c_ctx, w_ada, b_ada, g_mix, w_in, g_q, g_k, sconv_w, conf_dw, conf_db, conf_ln_g,
           conf_ln_b, conf_pw, conf_pb, g_group, w_o, g_ffn, dense_wg, dense_wu, dense_wd, router_w,
           moe_wg, moe_wu, moe_wd, g_final):
    b, s, d = x.shape
    n_ctx = ctx.shape[1]
    depth = w_ada.shape[0]
    assert b + 1 <= MOD_ROWS and depth == 2
    m_lat, m_ctx = b * s, b * n_ctx
    ctx_row = b

    cin = jnp.concatenate([c, c_ctx[None, :], jnp.zeros((MOD_ROWS - b - 1, d), F32)], axis=0)
    mod = _ada(cin, w_ada, b_ada)
    cos_t, sin_t = _rope_tables(s)
    row2 = lambda v: v.reshape(1, -1)

    lat_tm = 512
    lat_row_for = lambda tm: (lambda i: i // (s // tm))
    lat_row = lat_row_for(lat_tm)
    ctx_tm = 512
    ctx_mod_row = lambda i: ctx_row

    x2 = x.reshape(m_lat, d)
    xc2 = ctx.reshape(m_ctx, d)
    out = None
    moe_wd_b = ()
    for l in range(depth):
        last = l == depth - 1
        mod3 = mod[l].reshape(MOD_ROWS, 1, 6 * d)
        w_in_b = w_in[l].astype(BF16)
        w_o_b = w_o[l].astype(BF16)
        pw_b = conf_pw[l].astype(BF16)
        gg = g_group[l]
        gga, ggs, ggc = row2(gg[:ATTN_W]), row2(gg[ATTN_W:ATTN_W + SCONV_W]), row2(gg[ATTN_W + SCONV_W:])
        mixer_args = (sconv_w[l], conf_dw[l], row2(conf_db[l]), row2(conf_ln_g[l]), row2(conf_ln_b[l]),
                      pw_b, row2(conf_pb[l]), ggs, ggc)
        in_args = (row2(g_mix[l]), w_in_b, row2(g_q[l]), row2(g_k[l]), cos_t, sin_t)

        dense_here = l % 2 == 0
        k_dense = l // 2
        cast = (tuple((w[k_dense:k_dense + 1], 0, 1) for w in (dense_wg, dense_wu, dense_wd))
                if dense_here else ())
        k_moe = (l + 1) // 2 if dense_here else l // 2
        moe_wd_job = ((moe_wd[k_moe], l % 2, 2),) if k_moe < moe_wd.shape[0] else ()
        p_lat, dense_w_b = _in_proj(x2, mod3, lat_row, *in_args, tm=lat_tm, rope=True, j0=0, nj=8, seq=s,
                                    cast=cast)
        if last:
            p_ctx, _ = _in_proj(xc2, mod3, ctx_mod_row, *in_args, tm=ctx_tm, rope=False, j0=2, nj=1, seq=ctx_tm)
            kc_blk = 0
        else:
            p_ctx, _ = _in_proj(xc2, mod3, ctx_mod_row, *in_args, tm=ctx_tm, rope=False, j0=0, nj=8, seq=ctx_tm)
            kc_blk = ATTN_W // KV_W
        att, moe_wd_half = _attention(p_lat, p_ctx, p_lat, gga, n_batch=b, q_len=s, tq=512, kc_blk=kc_blk,
                                      has_latent=True, n_ctx=n_ctx, seq=s, cast=moe_wd_job)
        moe_wd_b = moe_wd_b + moe_wd_half
        mix = _local_mixers(p_lat, *mixer_args, n_seq=b, seq=s)
        x2 = _out_proj(att, mix, w_o_b, x2, mod3, lat_row, tm=lat_tm)
        if not last:
            att_c, _ = _attention(p_ctx, p_ctx, p_ctx, gga, n_batch=b, q_len=n_ctx, tq=n_ctx, kc_blk=kc_blk,
                                  has_latent=False, n_ctx=n_ctx, seq=n_ctx)
            mix_c = _local_mixers(p_ctx, *mixer_args, n_seq=b, seq=n_ctx)
            xc2 = _out_proj(att_c, mix_c, w_o_b, xc2, mod3, ctx_mod_row, tm=ctx_tm)

        if l % 2 == 0:
            wg_b, wu_b, wd_b = (w[0] for w in dense_w_b)
            ffn = functools.partial(_dense_ffn, g=row2(g_ffn[l]), wg=wg_b, wu=wu_b, wd=wd_b, tf=512)
            routed_next = l + 1 < depth and (l + 1) % 2 == 1
            cast = ((moe_wg[(l + 1) // 2], 0, 1), (moe_wu[(l + 1) // 2], 0, 1)) if routed_next else ()
            x2, moe_up_b = ffn(x2, mod3, lat_row, tm=lat_tm, cast=cast)
            if not last:
                xc2, _ = ffn(xc2, mod3, ctx_mod_row, tm=ctx_tm)
        else:
            assert last, "routed layer is implemented for the final layer (latent tokens only)"
            rw16 = jnp.zeros((ROUTER_ROWS, d), F32).at[:N_EXPERTS].set(router_w[l // 2].T)
            t, idx, gates, cnt = _router(x2, mod3, lat_row, row2(g_ffn[l]), rw16, tm=lat_tm)
            counts = cnt[:N_EXPERTS, 0]
            padded = (counts + EXPERT_TM - 1) // EXPERT_TM * EXPERT_TM
            pends = jnp.cumsum(padded)
            pstarts = pends - padded
            slot_e = idx[0:2]
            slot_start = sum(jnp.where(slot_e == e, pstarts[e], 0) for e in range(N_EXPERTS))
            dest = (slot_start + idx[2:4]).reshape(-1).astype(jnp.int32)
            n_tiles = (2 * m_lat) // EXPERT_TM + N_EXPERTS
            tile_start = jnp.arange(n_tiles, dtype=jnp.int32) * EXPERT_TM
            tile_v = (tile_start < pends[-1]).astype(jnp.int32)
            last_tile = pends[-1] // EXPERT_TM - 1
            tile_x = jnp.minimum(jnp.arange(n_tiles, dtype=jnp.int32), last_tile).astype(jnp.int32)
            tile_e = jnp.minimum(jnp.searchsorted(pends, tile_x * EXPERT_TM, side='right'),
                                 N_EXPERTS - 1).astype(jnp.int32)
            pad_info = jnp.concatenate([pstarts + counts, padded - counts,
                                        (pends[-1:] // EXPERT_TM)]).astype(jnp.int32)
            buf = _scatter_rows(dest, pad_info, t, tm=lat_tm, n_tiles=n_tiles)
            ybuf = _experts(tile_e, tile_v, tile_x, buf, *moe_up_b, moe_wd_b, tf=512)
            out = _combine(dest, x2, mod3, lat_row_for(256), gates, row2(g_final), ybuf, tm=256)
    return out.reshape(b, s, d)
```

```python
import functools

import jax
import jax.numpy as jnp
from jax import lax
from jax.experimental import pallas as pl
from jax.experimental.pallas import tpu as pltpu

F32 = jnp.float32
BF16 = jnp.bfloat16

GRID_W = 64
HEAD_DIM = 128
N_Q_HEADS = 8
N_KV_HEADS = 2
GQA_GROUP = N_Q_HEADS // N_KV_HEADS
ATTN_W = N_Q_HEADS * HEAD_DIM
KV_W = N_KV_HEADS * HEAD_DIM
ROPE_THETA = 10000.0
SCONV_W = 512
SCONV_K = 3
CONF_W = 512
CONF_K = 31
N_EXPERTS = 8
EPS = 1e-6
LOG2E = 1.4426950408889634

V7X_VMEM_LIMIT_BYTES = 56 * 1024 * 1024
MOD_ROWS = 16
IN_TN = 512
EXPERT_TM = 512


def _cparams(n_axes):
    return pltpu.CompilerParams(dimension_semantics=("arbitrary",) * n_axes,
                                vmem_limit_bytes=V7X_VMEM_LIMIT_BYTES)


def _sigmoid(x):
    return 1.0 / (1.0 + jnp.exp(-x))


NORM_CHUNK = 16
NORM_UNROLL = 4


def _norm_modulate_rows(x_ref, h_ref, g, shift, scale):
    rows = x_ref.shape[0]
    gain = g * (1.0 + scale)

    def body(c, carry):
        r = pl.multiple_of(c * NORM_CHUNK, NORM_CHUNK)
        x = x_ref[pl.ds(r, NORM_CHUNK), :]
        inv = lax.rsqrt(jnp.mean(x * x, axis=-1, keepdims=True) + EPS)
        h_ref[pl.ds(r, NORM_CHUNK), :] = (x * inv * gain + shift).astype(h_ref.dtype)
        return carry

    lax.fori_loop(0, rows // NORM_CHUNK, body, 0, unroll=NORM_UNROLL)


def _ada_kernel(c_ref, w_ref, b_ref, o_ref):
    a = c_ref[...]
    a = (a * _sigmoid(a)).astype(BF16)
    o_ref[0] = jnp.dot(a, w_ref[0].astype(BF16), preferred_element_type=F32) + b_ref[0]


def _ada(cin, w_ada, b_ada):
    depth, d, n = w_ada.shape
    tn = 1536
    return pl.pallas_call(
        _ada_kernel,
        grid=(depth, n // tn),
        in_specs=[pl.BlockSpec((MOD_ROWS, d), lambda l, j: (0, 0)),
                  pl.BlockSpec((1, d, tn), lambda l, j: (l, 0, j)),
                  pl.BlockSpec((1, 1, tn), lambda l, j: (l, 0, j))],
        out_specs=pl.BlockSpec((1, MOD_ROWS, tn), lambda l, j: (l, 0, j)),
        out_shape=jax.ShapeDtypeStruct((depth, MOD_ROWS, n), F32),
        compiler_params=_cparams(2),
        name="ada",
    )(cin, w_ada, b_ada.reshape(depth, 1, n))


def _head_norm(seg, gain):
    ms = jnp.mean(seg * seg, axis=-1, keepdims=True)
    return seg * lax.rsqrt(ms + EPS) * gain


def _rope(n, cos_t, sin_t):
    lane = lax.broadcasted_iota(jnp.int32, n.shape, 1)
    fwd = pltpu.roll(n, 32, 1)
    bwd = pltpu.roll(n, 96, 1)
    partner = jnp.where((lane // 32) % 2 == 0, bwd, fwd)
    return n * cos_t + partner * sin_t


def _in_kernel(x_ref, shift_ref, scale_ref, g_ref, w_ref, gq_ref, gk_ref, cos_ref, sin_ref,
               *rest, rope, j0, nj, q_scale, n_cast):
    cast_in, o_ref, cast_out, h_ref = rest[:n_cast], rest[n_cast], rest[n_cast + 1:2 * n_cast + 1], rest[-1]
    _norm_modulate_rows(x_ref, h_ref, g_ref[...], shift_ref[0], scale_ref[0])
    _cast_slabs(cast_in, cast_out)
    h = h_ref[...]

    def head(seg, gain, scale):
        n = _head_norm(seg, gain)
        if rope:
            n = _rope(n, cos_ref[...], sin_ref[...])
        if scale != 1.0:
            n = n * scale
        return n

    for t in range(nj):
        j = t + j0
        c0 = t * IN_TN
        acc = jnp.dot(h, w_ref[:, c0:c0 + IN_TN], preferred_element_type=F32)
        if j < 2:
            for hd in range(IN_TN // HEAD_DIM):
                sl = slice(hd * HEAD_DIM, (hd + 1) * HEAD_DIM)
                osl = slice(c0 + hd * HEAD_DIM, c0 + (hd + 1) * HEAD_DIM)
                o_ref[:, osl] = head(acc[:, sl], gq_ref[...], q_scale).astype(o_ref.dtype)
        elif j == 2:
            for hd in range(N_KV_HEADS):
                sl = slice(hd * HEAD_DIM, (hd + 1) * HEAD_DIM)
                osl = slice(c0 + hd * HEAD_DIM, c0 + (hd + 1) * HEAD_DIM)
                o_ref[:, osl] = head(acc[:, sl], gk_ref[...], 1.0).astype(o_ref.dtype)
            o_ref[:, c0 + KV_W:c0 + IN_TN] = acc[:, KV_W:].astype(o_ref.dtype)
        else:
            o_ref[:, c0:c0 + IN_TN] = acc.astype(o_ref.dtype)


def _const_spec(shape, idx):
    return pl.BlockSpec(shape, lambda i: idx, pipeline_mode=pl.Buffered(1))


def _in_proj(x2, mod3, mod_row, g, w, gq, gk, cos_t, sin_t, *, tm, rope, j0, nj, seq, cast=()):
    m, d = x2.shape
    tiles_per_seq = seq // tm
    ncols = nj * IN_TN
    assert j0 % nj == 0
    q_scale = HEAD_DIM ** -0.5 * LOG2E
    kern = functools.partial(_in_kernel, rope=rope, j0=j0, nj=nj, q_scale=q_scale, n_cast=len(cast))
    cast_arrays, cast_in, cast_out, cast_shapes = _cast_plan(cast, m // tm, lambda i: i)
    outs = pl.pallas_call(
        kern,
        grid=(m // tm,),
        in_specs=[pl.BlockSpec((tm, d), lambda i: (i, 0)),
                  pl.BlockSpec((1, 1, d), lambda i: (mod_row(i), 0, 0)),
                  pl.BlockSpec((1, 1, d), lambda i: (mod_row(i), 0, 1)),
                  _const_spec((1, d), (0, 0)),
                  _const_spec((d, ncols), (0, j0 // nj)),
                  _const_spec((1, HEAD_DIM), (0, 0)),
                  _const_spec((1, HEAD_DIM), (0, 0)),
                  pl.BlockSpec((tm, HEAD_DIM), lambda i: (i % tiles_per_seq, 0)),
                  pl.BlockSpec((tm, HEAD_DIM), lambda i: (i % tiles_per_seq, 0))] + cast_in,
        out_specs=[pl.BlockSpec((tm, ncols), lambda i: (i, 0))] + cast_out,
        out_shape=[jax.ShapeDtypeStruct((m, ncols), BF16)] + cast_shapes,
        scratch_shapes=[pltpu.VMEM((tm, d), BF16)],
        compiler_params=_cparams(1),
        name="in_proj",
    )(x2, mod3, mod3, g, w, gq, gk, cos_t, sin_t, *cast_arrays)
    return outs[0], tuple(outs[1:])


def _attn_kernel(q_ref, kc_ref, vc_ref, kl_ref, vl_ref, gg_ref, *rest, has_latent, n_cast):
    cast_in, o_ref, cast_out, acc_ref = rest[:n_cast], rest[n_cast], rest[n_cast + 1:2 * n_cast + 1], rest[-1]
    _cast_slabs(cast_in, cast_out)
    tq = q_ref.shape[0]
    dn = (((1,), (1,)), ((), ()))
    ssq = jnp.zeros((tq, 1), F32)
    for hd in range(N_Q_HEADS):
        kv = hd // GQA_GROUP
        ksl = slice(kv * HEAD_DIM, (kv + 1) * HEAD_DIM)
        q = q_ref[:, hd * HEAD_DIM:(hd + 1) * HEAD_DIM]
        s_c = lax.dot_general(q, kc_ref[:, ksl], dn, preferred_element_type=F32)
        mx = jnp.max(s_c, axis=-1, keepdims=True)
        if has_latent:
            s_l = lax.dot_general(q, kl_ref[:, ksl], dn, preferred_element_type=F32)
            mx = jnp.maximum(mx, jnp.max(s_l, axis=-1, keepdims=True))
        p_c = jnp.exp2(s_c - mx)
        den = jnp.sum(p_c, axis=-1, keepdims=True)
        o = jnp.dot(p_c.astype(BF16), vc_ref[:, ksl], preferred_element_type=F32)
        if has_latent:
            p_l = jnp.exp2(s_l - mx)
            den = den + jnp.sum(p_l, axis=-1, keepdims=True)
            o = o + jnp.dot(p_l.astype(BF16), vl_ref[:, ksl], preferred_element_type=F32)
        o = o * (1.0 / den)
        ssq = ssq + jnp.sum(o * o, axis=-1, keepdims=True)
        acc_ref[:, hd * HEAD_DIM:(hd + 1) * HEAD_DIM] = o
    inv = lax.rsqrt(ssq * (1.0 / ATTN_W) + EPS)
    o_ref[...] = (acc_ref[...] * inv * gg_ref[...]).astype(o_ref.dtype)


def _attention(qsrc, csrc, lsrc, gg, *, n_batch, q_len, tq, kc_blk, has_latent, n_ctx, seq, cast=()):
    m = qsrc.shape[0]
    tiles = q_len // tq
    kern = functools.partial(_attn_kernel, has_latent=has_latent, n_cast=len(cast))
    cast_arrays, cast_in, cast_out, cast_shapes = _cast_plan(cast, n_batch * tiles, lambda b, i: b * tiles + i)
    outs = pl.pallas_call(
        kern,
        grid=(n_batch, tiles),
        in_specs=[pl.BlockSpec((tq, ATTN_W), lambda b, i: (b * tiles + i, 0)),
                  pl.BlockSpec((n_ctx, KV_W), lambda b, i: (b, kc_blk)),
                  pl.BlockSpec((n_ctx, KV_W), lambda b, i: (b, kc_blk + 1)),
                  pl.BlockSpec((seq, KV_W), lambda b, i: (b, ATTN_W // KV_W)),
                  pl.BlockSpec((seq, KV_W), lambda b, i: (b, ATTN_W // KV_W + 1)),
                  pl.BlockSpec((1, ATTN_W), lambda b, i: (0, 0))] + cast_in,
        out_specs=[pl.BlockSpec((tq, ATTN_W), lambda b, i: (b * tiles + i, 0))] + cast_out,
        out_shape=[jax.ShapeDtypeStruct((m, ATTN_W), BF16)] + cast_shapes,
        scratch_shapes=[pltpu.VMEM((tq, ATTN_W), F32)],
        compiler_params=_cparams(2),
        name="attention",
    )(qsrc, csrc, csrc, lsrc, lsrc, gg, *cast_arrays)
    return outs[0], tuple(outs[1:])


MIX_CHUNK = 64
SCONV_PAD = 8
CONF_PAD = 16


LANES = 128
SUBLANES = 8
N_LANE_GROUPS = SCONV_W // LANES


def _window_conv(p_ref, slot, w_ref, group, first, n_taps, ch):
    lanes = slice(group * LANES, (group + 1) * LANES)
    win = p_ref[slot, :, lanes]
    n_rows = win.shape[0]
    acc = None
    for phase in range(SUBLANES):
        taps = [k for k in range(n_taps) if (first + k) % SUBLANES == phase]
        if not taps:
            continue
        shifted = win if phase == 0 else pltpu.roll(win, n_rows - phase, 0)
        for k in taps:
            base = first + k - phase
            term = w_ref[k:k + 1, lanes] * shifted[base:base + ch]
            acc = term if acc is None else acc + term
    return acc


def _mix_kernel(sb_ref, sg_ref, sx_ref, ga_ref, gb_ref, sw_ref, dw_ref, db_ref, lng_ref, lnb_ref,
                pw_ref, pb_ref, ggs_ref, ggc_ref, o_ref, p1_ref, p2_ref, a_ref):
    seq = sb_ref.shape[0]
    ch = MIX_CHUNK
    n_chunks = seq // ch
    hs, hc = SCONV_PAD, CONF_PAD
    p1_ref[1, 0:hs, :] = jnp.zeros((hs, SCONV_W), F32)
    p1_ref[n_chunks, ch + hs:ch + 2 * hs, :] = jnp.zeros((hs, SCONV_W), F32)
    p2_ref[1, 0:hc, :] = jnp.zeros((hc, CONF_W), F32)
    p2_ref[n_chunks, ch + hc:ch + 2 * hc, :] = jnp.zeros((hc, CONF_W), F32)

    def fill(c, carry):
        r = pl.multiple_of(c * ch, ch)
        rows = pl.ds(r, ch)
        v = sg_ref[rows, :].astype(F32) * sx_ref[rows, :].astype(F32)
        p1_ref[c + 1, hs:hs + ch, :] = v
        p1_ref[c, ch + hs:ch + 2 * hs, :] = v[0:hs]
        p1_ref[c + 2, 0:hs, :] = v[ch - hs:ch]
        u = ga_ref[rows, :].astype(F32) * _sigmoid(gb_ref[rows, :].astype(F32))
        p2_ref[c + 1, hc:hc + ch, :] = u
        p2_ref[c, ch + hc:ch + 2 * hc, :] = u[0:hc]
        p2_ref[c + 2, 0:hc, :] = u[ch - hc:ch]
        return carry

    lax.fori_loop(0, n_chunks, fill, 0)

    def conv(c, carry):
        r = pl.multiple_of(c * ch, ch)
        rows = pl.ds(r, ch)
        acc = jnp.concatenate(
            [_window_conv(p1_ref, c + 1, sw_ref, g, hs - SCONV_K // 2, SCONV_K, ch) for g in range(N_LANE_GROUPS)],
            axis=1)
        ys = sb_ref[rows, :].astype(F32) * acc
        ms = jnp.mean(ys * ys, axis=-1, keepdims=True)
        o_ref[rows, 0:SCONV_W] = (ys * lax.rsqrt(ms + EPS) * ggs_ref[...]).astype(o_ref.dtype)

        u = db_ref[...] + jnp.concatenate(
            [_window_conv(p2_ref, c + 1, dw_ref, g, hc - CONF_K // 2, CONF_K, ch) for g in range(N_LANE_GROUPS)],
            axis=1)
        mu = jnp.mean(u, axis=-1, keepdims=True)
        uc = u - mu
        var = jnp.mean(uc * uc, axis=-1, keepdims=True)
        v = uc * lax.rsqrt(var + EPS) * lng_ref[...] + lnb_ref[...]
        a_ref[rows, :] = (v * _sigmoid(v)).astype(a_ref.dtype)
        return carry

    lax.fori_loop(0, n_chunks, conv, 0)

    yc = jnp.dot(a_ref[...], pw_ref[...], preferred_element_type=F32) + pb_ref[...]
    ms = jnp.mean(yc * yc, axis=-1, keepdims=True)
    o_ref[:, SCONV_W:] = (yc * lax.rsqrt(ms + EPS) * ggc_ref[...]).astype(o_ref.dtype)


def _local_mixers(src, sconv_w, conf_dw, conf_db, ln_g, ln_b, pw, pb, ggs, ggc, *, n_seq, seq):
    w = SCONV_W
    col = lambda k: pl.BlockSpec((seq, w), lambda b: (b, k))
    vec = lambda n: pl.BlockSpec((1, n), lambda b: (0, 0))
    return pl.pallas_call(
        _mix_kernel,
        grid=(n_seq,),
        in_specs=[col(3), col(4), col(5), col(6), col(7),
                  pl.BlockSpec((SCONV_K, w), lambda b: (0, 0)),
                  pl.BlockSpec((CONF_K, w), lambda b: (0, 0)),
                  vec(w), vec(w), vec(w),
                  pl.BlockSpec((w, w), lambda b: (0, 0)),
                  vec(w), vec(w), vec(w)],
        out_specs=pl.BlockSpec((seq, 2 * w), lambda b: (b, 0)),
        out_shape=jax.ShapeDtypeStruct((src.shape[0], 2 * w), BF16),
        scratch_shapes=[pltpu.VMEM((seq // MIX_CHUNK + 2, MIX_CHUNK + 2 * SCONV_PAD, w), F32),
                        pltpu.VMEM((seq // MIX_CHUNK + 2, MIX_CHUNK + 2 * CONF_PAD, w), F32),
                        pltpu.VMEM((seq, w), BF16)],
        compiler_params=_cparams(1),
        name="local_mixers",
    )(src, src, src, src, src, sconv_w, conf_dw, conf_db, ln_g, ln_b, pw, pb, ggs, ggc)


OUT_TN = 512


def _out_kernel(a_ref, m_ref, wa_ref, wm_ref, x_ref, gate_ref, o_ref):
    a = a_ref[...]
    mx = m_ref[...]
    for t in range(o_ref.shape[1] // OUT_TN):
        sl = slice(t * OUT_TN, (t + 1) * OUT_TN)
        acc = (jnp.dot(a, wa_ref[:, sl], preferred_element_type=F32)
               + jnp.dot(mx, wm_ref[:, sl], preferred_element_type=F32))
        o_ref[:, sl] = x_ref[:, sl] + gate_ref[0, :, sl] * acc


def _out_proj(att, mix, w, x2, mod3, mod_row, *, tm):
    m, d = x2.shape
    half = att.shape[1]
    return pl.pallas_call(
        _out_kernel,
        grid=(m // tm,),
        in_specs=[pl.BlockSpec((tm, half), lambda i: (i, 0)),
                  pl.BlockSpec((tm, half), lambda i: (i, 0)),
                  _const_spec((half, d), (0, 0)),
                  _const_spec((half, d), (1, 0)),
                  pl.BlockSpec((tm, d), lambda i: (i, 0)),
                  pl.BlockSpec((1, 1, d), lambda i: (mod_row(i), 0, 2))],
        out_specs=pl.BlockSpec((tm, d), lambda i: (i, 0)),
        out_shape=jax.ShapeDtypeStruct((m, d), F32),
        compiler_params=_cparams(1),
        name="out_proj",
    )(att, mix, w, w, x2, mod3)


def _swiglu_accumulate(h, wg_ref, wu_ref, wd_refs, o_ref, assign=False):
    tf = wg_ref.shape[1]
    halves = (slice(0, tf // 2), slice(tf // 2, tf))
    ups = [(jnp.dot(h, wg_ref[:, sl], preferred_element_type=F32),
            jnp.dot(h, wu_ref[:, sl], preferred_element_type=F32)) for sl in halves]
    acts = [(gv * _sigmoid(gv) * uv).astype(BF16) for gv, uv in ups]
    col = 0
    for wd_ref in wd_refs:
        width = wd_ref.shape[1]
        part = (jnp.dot(acts[0], wd_ref[halves[0], :], preferred_element_type=F32)
                + jnp.dot(acts[1], wd_ref[halves[1], :], preferred_element_type=F32))
        if assign:
            o_ref[:, col:col + width] = part
        else:
            o_ref[:, col:col + width] += part
        col += width


def _ffn_kernel(x_ref, shift_ref, scale_ref, gate_ref, g_ref, wg_ref, wu_ref, wd_ref, *rest, n_cast):
    cast_in, o_ref, cast_out, h_ref = rest[:n_cast], rest[n_cast], rest[n_cast + 1:2 * n_cast + 1], rest[-1]
    j = pl.program_id(1)

    _cast_slabs(cast_in, cast_out)

    @pl.when(j == 0)
    def _():
        _norm_modulate_rows(x_ref, h_ref, g_ref[...], shift_ref[0], scale_ref[0])
        o_ref[...] = jnp.zeros(o_ref.shape, o_ref.dtype)

    _swiglu_accumulate(h_ref[...], wg_ref, wu_ref, (wd_ref,), o_ref)

    @pl.when(j == pl.num_programs(1) - 1)
    def _():
        o_ref[...] = x_ref[...] + gate_ref[0] * o_ref[...]


BF16_SUBLANE_TILE = 16


def _slab_rows(rows, n_lead, n_steps):
    for height in range(BF16_SUBLANE_TILE, rows + 1, BF16_SUBLANE_TILE):
        if rows % height == 0 and n_lead * (rows // height) <= n_steps:
            return height
    raise ValueError(f"no slab height for rows={rows}, n_lead={n_lead}, n_steps={n_steps}")


def _cast_plan(jobs, n_steps, step_of):
    arrays, in_specs, out_specs, out_shapes = [], [], [], []
    for arr, col_block, n_col_blocks in jobs:
        n_lead, rows, cols = arr.shape
        width = cols // n_col_blocks
        height = _slab_rows(rows, n_lead, n_steps)
        per_lead = rows // height
        last = n_lead * per_lead - 1

        def index(*grid_idx, per_lead=per_lead, last=last, col=0):
            s = jnp.minimum(step_of(*grid_idx), last)
            return (s // per_lead, s % per_lead, col)

        arrays.append(arr)
        in_specs.append(pl.BlockSpec((1, height, width), functools.partial(index, col=col_block)))
        out_specs.append(pl.BlockSpec((1, height, width), index))
        out_shapes.append(jax.ShapeDtypeStruct((n_lead, rows, width), BF16))
    return arrays, in_specs, out_specs, out_shapes


def _cast_slabs(cast_in, cast_out):
    for src, dst in zip(cast_in, cast_out):
        dst[...] = src[...].astype(dst.dtype)


def _dense_ffn(x2, mod3, mod_row, g, wg, wu, wd, *, tm, tf, cast=()):
    m, d = x2.shape
    f = wg.shape[1]
    nf = f // tf
    n_steps = (m // tm) * nf
    modspec = lambda k: pl.BlockSpec((1, 1, d), lambda i, j: (mod_row(i), 0, k))
    cast_arrays, cast_in, cast_out, cast_shapes = _cast_plan(cast, n_steps, lambda i, j: i * nf + j)
    outs = pl.pallas_call(
        functools.partial(_ffn_kernel, n_cast=len(cast)),
        grid=(m // tm, nf),
        in_specs=[pl.BlockSpec((tm, d), lambda i, j: (i, 0)),
                  modspec(3), modspec(4), modspec(5),
                  pl.BlockSpec((1, d), lambda i, j: (0, 0)),
                  pl.BlockSpec((d, tf), lambda i, j: (0, j)),
                  pl.BlockSpec((d, tf), lambda i, j: (0, j)),
                  pl.BlockSpec((tf, d), lambda i, j: (j, 0))] + cast_in,
        out_specs=[pl.BlockSpec((tm, d), lambda i, j: (i, 0))] + cast_out,
        out_shape=[jax.ShapeDtypeStruct((m, d), F32)] + cast_shapes,
        scratch_shapes=[pltpu.VMEM((tm, d), BF16)],
        compiler_params=_cparams(2),
        name="dense_ffn",
    )(x2, mod3, mod3, mod3, g, wg, wu, wd, *cast_arrays)
    return outs[0], tuple(outs[1:])


ROUTER_ROWS = 16


def _router_kernel(x_ref, shift_ref, scale_ref, g_ref, rw_ref, t_ref, idx_ref, gate_ref, cnt_ref,
                   tri_ref, base_ref):
    tm = x_ref.shape[0]

    @pl.when(pl.program_id(0) == 0)
    def _():
        r = lax.broadcasted_iota(jnp.int32, (tm, tm), 0)
        c = lax.broadcasted_iota(jnp.int32, (tm, tm), 1)
        tri_ref[...] = jnp.where(r < c, 1.0, 0.0).astype(BF16)
        base_ref[...] = jnp.zeros(base_ref.shape, F32)

    _norm_modulate_rows(x_ref, t_ref, g_ref[...], shift_ref[0], scale_ref[0])

    t = t_ref[...]
    t_hi = t.astype(BF16)
    t_lo = (t - t_hi.astype(F32)).astype(BF16)
    w = rw_ref[...]
    w_hi = w.astype(BF16)
    w_lo = (w - w_hi.astype(F32)).astype(BF16)
    dn = (((1,), (1,)), ((), ()))
    logits = (lax.dot_general(w_hi, t_hi, dn, preferred_element_type=F32)
              + lax.dot_general(w_lo, t_hi, dn, preferred_element_type=F32)
              + lax.dot_general(w_hi, t_lo, dn, preferred_element_type=F32))

    e = lax.broadcasted_iota(jnp.int32, (ROUTER_ROWS, tm), 0).astype(F32)
    neg = jnp.float32(-jnp.inf)
    lg = jnp.where(e < N_EXPERTS, logits, neg)
    m1 = jnp.max(lg, axis=0, keepdims=True)
    i1 = jnp.min(jnp.where(lg == m1, e, float(ROUTER_ROWS)), axis=0, keepdims=True)
    lg2 = jnp.where(e == i1, neg, lg)
    m2 = jnp.max(lg2, axis=0, keepdims=True)
    i2 = jnp.min(jnp.where(lg2 == m2, e, float(ROUTER_ROWS)), axis=0, keepdims=True)
    ex = jnp.exp(m2 - m1)
    den = 1.0 + ex
    gate_ref[0:1, :] = 1.0 / den
    gate_ref[1:2, :] = ex / den

    hit1 = e == i1
    hit2 = e == i2
    onehot = jnp.where(hit1 | hit2, 1.0, 0.0)
    prefix = jnp.dot(onehot.astype(BF16), tri_ref[...], preferred_element_type=F32) + base_ref[:, 0:1]
    r1 = jnp.sum(jnp.where(hit1, prefix, 0.0), axis=0, keepdims=True)
    r2 = jnp.sum(jnp.where(hit2, prefix, 0.0), axis=0, keepdims=True)
    idx_ref[0:1, :] = i1.astype(jnp.int32)
    idx_ref[1:2, :] = i2.astype(jnp.int32)
    idx_ref[2:3, :] = r1.astype(jnp.int32)
    idx_ref[3:4, :] = r2.astype(jnp.int32)
    base_ref[...] = base_ref[...] + jnp.sum(onehot, axis=1, keepdims=True)
    cnt_ref[...] = base_ref[...].astype(jnp.int32)


def _router(x2, mod3, mod_row, g, rw16, *, tm):
    m, d = x2.shape
    modspec = lambda k: pl.BlockSpec((1, 1, d), lambda i: (mod_row(i), 0, k))
    return pl.pallas_call(
        _router_kernel,
        grid=(m // tm,),
        in_specs=[pl.BlockSpec((tm, d), lambda i: (i, 0)),
                  modspec(3), modspec(4),
                  pl.BlockSpec((1, d), lambda i: (0, 0)),
                  pl.BlockSpec((ROUTER_ROWS, d), lambda i: (0, 0))],
        out_specs=[pl.BlockSpec((tm, d), lambda i: (i, 0)),
                   pl.BlockSpec((4, tm), lambda i: (0, i)),
                   pl.BlockSpec((2, tm), lambda i: (0, i)),
                   pl.BlockSpec((ROUTER_ROWS, 128), lambda i: (0, 0))],
        out_shape=[jax.ShapeDtypeStruct((m, d), F32),
                   jax.ShapeDtypeStruct((4, m), jnp.int32),
                   jax.ShapeDtypeStruct((2, m), F32),
                   jax.ShapeDtypeStruct((ROUTER_ROWS, 128), jnp.int32)],
        scratch_shapes=[pltpu.VMEM((tm, tm), BF16), pltpu.VMEM((ROUTER_ROWS, 128), F32)],
        compiler_params=_cparams(1),
        name="router",
    )(x2, mod3, mod3, g, rw16)


def _row_copy(src_ref, src_row, dst_ref, dst_row, sem):
    return pltpu.make_async_copy(src_ref.at[pl.ds(src_row, 1), :], dst_ref.at[pl.ds(dst_row, 1), :], sem)


DMA_ISSUE_UNROLL = 8


def _scatter_kernel(dest_ref, pad_ref, t_ref, buf_ref, zero_ref, sem, zsem, *, n_tok, n_tiles):
    tm = t_ref.shape[0]
    base = pl.program_id(0) * tm

    @pl.when(pl.program_id(0) == 0)
    def _():
        zero_ref[...] = jnp.zeros(zero_ref.shape, zero_ref.dtype)
        for e in range(N_EXPERTS):
            start, length = pad_ref[e], pad_ref[N_EXPERTS + e]

            def fill(r, carry, start=start):
                _row_copy(zero_ref, 0, buf_ref, start + r, zsem).start()
                return carry

            def drain(r, carry):
                _row_copy(zero_ref, 0, buf_ref, 0, zsem).wait()
                return carry

            lax.fori_loop(0, length, fill, 0)
            lax.fori_loop(0, length, drain, 0)

        def tile_copy(tile):
            row0 = pl.multiple_of(tile * EXPERT_TM, EXPERT_TM)
            return pltpu.make_async_copy(zero_ref, buf_ref.at[pl.ds(row0, EXPERT_TM), :], zsem)

        def fill_tile(tile, carry):
            tile_copy(tile).start()
            return carry

        def drain_tile(tile, carry):
            tile_copy(tile).wait()
            return carry

        lax.fori_loop(pad_ref[2 * N_EXPERTS], n_tiles, fill_tile, 0)
        lax.fori_loop(pad_ref[2 * N_EXPERTS], n_tiles, drain_tile, 0)

    def start(r, carry):
        for k in range(2):
            _row_copy(t_ref, r, buf_ref, dest_ref[k * n_tok + base + r], sem).start()
        return carry

    lax.fori_loop(0, tm, start, 0, unroll=DMA_ISSUE_UNROLL)
    for k in range(2):
        pltpu.make_async_copy(t_ref, buf_ref.at[pl.ds(0, tm), :], sem).wait()


def _scatter_rows(dest, pad_info, t, *, tm, n_tiles):
    m, d = t.shape
    grid_spec = pltpu.PrefetchScalarGridSpec(
        num_scalar_prefetch=2,
        grid=(m // tm,),
        in_specs=[pl.BlockSpec((tm, d), lambda i, dest, pad: (i, 0))],
        out_specs=pl.BlockSpec(memory_space=pl.ANY),
        scratch_shapes=[pltpu.VMEM((EXPERT_TM, d), t.dtype),
                        pltpu.SemaphoreType.DMA(()), pltpu.SemaphoreType.DMA(())],
    )
    return pl.pallas_call(
        functools.partial(_scatter_kernel, n_tok=m, n_tiles=n_tiles),
        grid_spec=grid_spec,
        out_shape=jax.ShapeDtypeStruct((n_tiles * EXPERT_TM, d), t.dtype),
        compiler_params=_cparams(1),
        name="moe_scatter",
    )(dest, pad_info, t)


N_WEIGHT_SLOTS = 3


def _expert_kernel(te_ref, tv_ref, tx_ref, x_ref, *rest, n_wd, tf, nf):
    del tx_ref
    w_hbm = rest[:2 + n_wd]
    o_ref, xb_ref = rest[2 + n_wd], rest[3 + n_wd]
    w_buf = rest[4 + n_wd:6 + 2 * n_wd]
    sem = rest[-1]
    i = pl.program_id(0)
    n = pl.num_programs(0)

    def copies(tile, j, slot):
        e = te_ref[tile]
        cols = pl.ds(j * tf, tf)
        srcs = [w_hbm[0].at[e, :, cols], w_hbm[1].at[e, :, cols]] + [w.at[e, cols, :] for w in w_hbm[2:]]
        return [pltpu.make_async_copy(src, buf.at[slot], sem.at[slot, k])
                for k, (src, buf) in enumerate(zip(srcs, w_buf))]

    valid = tv_ref[i] == 1
    nxt = jnp.minimum(i + 1, n - 1)
    next_live = jnp.logical_and(i + 1 < n, tv_ref[nxt] == 1)

    @pl.when(jnp.logical_not(valid))
    def _():
        o_ref[...] = jnp.zeros(o_ref.shape, o_ref.dtype)

    @pl.when(jnp.logical_and(valid, i == 0))
    def _():
        for c in copies(i, 0, 0):
            c.start()

    @pl.when(valid)
    def _():
        xb_ref[...] = x_ref[...].astype(BF16)
        xb = xb_ref[...]
        for j in range(nf):
            slot = j % N_WEIGHT_SLOTS
            ahead = copies(i, j + 1, (j + 1) % N_WEIGHT_SLOTS) if j + 1 < nf else copies(nxt, 0, 0)
            for c in ahead:
                c.start()
            for c in copies(i, j, slot):
                c.wait()
            _swiglu_accumulate(xb, w_buf[0].at[slot], w_buf[1].at[slot], [b.at[slot] for b in w_buf[2:]],
                               o_ref, assign=j == 0)

    @pl.when(jnp.logical_and(valid, jnp.logical_not(next_live)))
    def _():
        for c in copies(nxt, 0, 0):
            c.wait()


def _experts(tile_e, tile_v, tile_x, buf, wg, wu, wds, *, tf):
    rows, d = buf.shape
    f = wg.shape[2]
    nf = f // tf
    assert nf % N_WEIGHT_SLOTS != 1, "slot 0 must be free while the last hidden tile computes"
    any_spec = pl.BlockSpec(memory_space=pl.ANY)
    grid_spec = pltpu.PrefetchScalarGridSpec(
        num_scalar_prefetch=3,
        grid=(rows // EXPERT_TM,),
        in_specs=[pl.BlockSpec((EXPERT_TM, d), lambda i, te, tv, tx: (tx[i], 0))] + [any_spec] * (2 + len(wds)),
        out_specs=pl.BlockSpec((EXPERT_TM, d), lambda i, te, tv, tx: (i, 0)),
        scratch_shapes=[pltpu.VMEM((EXPERT_TM, d), BF16),
                        pltpu.VMEM((N_WEIGHT_SLOTS, d, tf), BF16),
                        pltpu.VMEM((N_WEIGHT_SLOTS, d, tf), BF16)]
                       + [pltpu.VMEM((N_WEIGHT_SLOTS, tf, wd.shape[2]), BF16) for wd in wds]
                       + [pltpu.SemaphoreType.DMA((N_WEIGHT_SLOTS, 2 + len(wds)))],
    )
    return pl.pallas_call(
        functools.partial(_expert_kernel, n_wd=len(wds), tf=tf, nf=nf),
        grid_spec=grid_spec,
        out_shape=jax.ShapeDtypeStruct((rows, d), F32),
        compiler_params=_cparams(1),
        name="moe_experts",
    )(tile_e, tile_v, tile_x, buf, wg, wu, *wds)


def _combine_kernel(dest_ref, x_ref, gate5_ref, gates_ref, gf_ref, ybuf_ref, o_ref, rows_ref, sem, *, n_tok):
    tm = x_ref.shape[0]
    i = pl.program_id(0)

    def issue(tile, slot):
        base = tile * tm

        def start(r, carry):
            for k in range(2):
                _row_copy(ybuf_ref, dest_ref[k * n_tok + base + r], rows_ref.at[slot, k], r, sem.at[slot]).start()
            return carry

        lax.fori_loop(0, tm, start, 0, unroll=DMA_ISSUE_UNROLL)

    @pl.when(i == 0)
    def _():
        issue(0, 0)

    @pl.when(i + 1 < pl.num_programs(0))
    def _():
        issue(i + 1, (i + 1) % 2)

    slot = i % 2
    for k in range(2):
        pltpu.make_async_copy(ybuf_ref.at[pl.ds(0, tm), :], rows_ref.at[slot, k], sem.at[slot]).wait()

    eye = (lax.broadcasted_iota(jnp.int32, (tm, tm), 0) == lax.broadcasted_iota(jnp.int32, (tm, tm), 1))
    g0 = jnp.sum(jnp.where(eye, gates_ref[0:1, :], 0.0), axis=1, keepdims=True)
    g1 = jnp.sum(jnp.where(eye, gates_ref[1:2, :], 0.0), axis=1, keepdims=True)
    y = g0 * rows_ref[slot, 0] + g1 * rows_ref[slot, 1]
    xn = x_ref[...] + gate5_ref[0] * y
    ms = jnp.mean(xn * xn, axis=-1, keepdims=True)
    o_ref[...] = xn * lax.rsqrt(ms + EPS) * gf_ref[...]


def _combine(dest, x2, mod3, mod_row, gates, g_final, ybuf, *, tm):
    m, d = x2.shape
    grid_spec = pltpu.PrefetchScalarGridSpec(
        num_scalar_prefetch=1,
        grid=(m // tm,),
        in_specs=[pl.BlockSpec((tm, d), lambda i, dest: (i, 0)),
                  pl.BlockSpec((1, 1, d), lambda i, dest: (mod_row(i), 0, 5)),
                  pl.BlockSpec((2, tm), lambda i, dest: (0, i)),
                  pl.BlockSpec((1, d), lambda i, dest: (0, 0)),
                  pl.BlockSpec(memory_space=pl.ANY)],
        out_specs=pl.BlockSpec((tm, d), lambda i, dest: (i, 0)),
        scratch_shapes=[pltpu.VMEM((2, 2, tm, d), F32), pltpu.SemaphoreType.DMA((2,))],
    )
    return pl.pallas_call(
        functools.partial(_combine_kernel, n_tok=m),
        grid_spec=grid_spec,
        out_shape=jax.ShapeDtypeStruct((m, d), F32),
        compiler_params=_cparams(1),
        name="moe_combine",
    )(dest, x2, mod3, gates, g_final, ybuf)


def _rope_tables(seq):
    rows = seq // GRID_W
    row = jnp.repeat(jnp.arange(rows), GRID_W).astype(F32)
    col = jnp.tile(jnp.arange(GRID_W), rows).astype(F32)
    axis_dim = HEAD_DIM // 2
    inv_freq = ROPE_THETA ** (-jnp.arange(0, axis_dim, 2, dtype=F32) / axis_dim)
    ang_r = row[:, None] * inv_freq
    ang_c = col[:, None] * inv_freq
    cr, sr, cc, sc = jnp.cos(ang_r), jnp.sin(ang_r), jnp.cos(ang_c), jnp.sin(ang_c)
    return (jnp.concatenate([cr, cr, cc, cc], axis=-1),
            jnp.concatenate([-sr, sr, -sc, sc], axis=-1))


def kernel(x, c, ctx, c_ctx, w_ada, b_ada, g_mix, w_in, g_q, g_k, sconv_w, conf_dw, conf_db, conf_ln_g,
           conf_ln_b, conf_pw, conf_pb, g_group, w_o, g_ffn, dense_wg, dense_wu, dense_wd, router_w,
           moe_wg, moe_wu, moe_wd, g_final):
    b, s, d = x.shape
    n_ctx = ctx.shape[1]
    depth = w_ada.shape[0]
    assert b + 1 <= MOD_ROWS and depth == 2
    m_lat, m_ctx = b * s, b * n_ctx
    ctx_row = b

    cin = jnp.concatenate([c, c_ctx[None, :], jnp.zeros((MOD_ROWS - b - 1, d), F32)], axis=0)
    mod = _ada(cin, w_ada, b_ada)
    cos_t, sin_t = _rope_tables(s)
    row2 = lambda v: v.reshape(1, -1)

    lat_tm = 512
    lat_row_for = lambda tm: (lambda i: i // (s // tm))
    lat_row = lat_row_for(lat_tm)
    ctx_tm = 512
    ctx_mod_row = lambda i: ctx_row

    x2 = x.reshape(m_lat, d)
    xc2 = ctx.reshape(m_ctx, d)
    out = None
    moe_wd_b = ()
    for l in range(depth):
        last = l == depth - 1
        mod3 = mod[l].reshape(MOD_ROWS, 1, 6 * d)
        w_in_b = w_in[l].astype(BF16)
        w_o_b = w_o[l].astype(BF16)
        pw_b = conf_pw[l].astype(BF16)
        gg = g_group[l]
        gga, ggs, ggc = row2(gg[:ATTN_W]), row2(gg[ATTN_W:ATTN_W + SCONV_W]), row2(gg[ATTN_W + SCONV_W:])
        mixer_args = (sconv_w[l], conf_dw[l], row2(conf_db[l]), row2(conf_ln_g[l]), row2(conf_ln_b[l]),
                      pw_b, row2(conf_pb[l]), ggs, ggc)
        in_args = (row2(g_mix[l]), w_in_b, row2(g_q[l]), row2(g_k[l]), cos_t, sin_t)

        dense_here = l % 2 == 0
        k_dense = l // 2
        cast = (tuple((w[k_dense:k_dense + 1], 0, 1) for w in (dense_wg, dense_wu, dense_wd))
                if dense_here else ())
        k_moe = (l + 1) // 2 if dense_here else l // 2
        moe_wd_job = ((moe_wd[k_moe], l % 2, 2),) if k_moe < moe_wd.shape[0] else ()
        p_lat, dense_w_b = _in_proj(x2, mod3, lat_row, *in_args, tm=lat_tm, rope=True, j0=0, nj=8, seq=s,
                                    cast=cast)
        if last:
            p_ctx, _ = _in_proj(xc2, mod3, ctx_mod_row, *in_args, tm=ctx_tm, rope=False, j0=2, nj=1, seq=ctx_tm)
            kc_blk = 0
        else:
            p_ctx, _ = _in_proj(xc2, mod3, ctx_mod_row, *in_args, tm=ctx_tm, rope=False, j0=0, nj=8, seq=ctx_tm)
            kc_blk = ATTN_W // KV_W
        att, moe_wd_half = _attention(p_lat, p_ctx, p_lat, gga, n_batch=b, q_len=s, tq=512, kc_blk=kc_blk,
                                      has_latent=True, n_ctx=n_ctx, seq=s, cast=moe_wd_job)
        moe_wd_b = moe_wd_b + moe_wd_half
        mix = _local_mixers(p_lat, *mixer_args, n_seq=b, seq=s)
        x2 = _out_proj(att, mix, w_o_b, x2, mod3, lat_row, tm=lat_tm)
        if not last:
            att_c, _ = _attention(p_ctx, p_ctx, p_ctx, gga, n_batch=b, q_len=n_ctx, tq=n_ctx, kc_blk=kc_blk,
                                  has_latent=False, n_ctx=n_ctx, seq=n_ctx)
            mix_c = _local_mixers(p_ctx, *mixer_args, n_seq=b, seq=n_ctx)
            xc2 = _out_proj(att_c, mix_c, w_o_b, xc2, mod3, ctx_mod_row, tm=ctx_tm)

        if l % 2 == 0:
            wg_b, wu_b, wd_b = (w[0] for w in dense_w_b)
            ffn = functools.partial(_dense_ffn, g=row2(g_ffn[l]), wg=wg_b, wu=wu_b, wd=wd_b, tf=512)
            routed_next = l + 1 < depth and (l + 1) % 2 == 1
            cast = ((moe_wg[(l + 1) // 2], 0, 1), (moe_wu[(l + 1) // 2], 0, 1)) if routed_next else ()
            x2, moe_up_b = ffn(x2, mod3, lat_row, tm=lat_tm, cast=cast)
            if not last:
                xc2, _ = ffn(xc2, mod3, ctx_mod_row, tm=ctx_tm)
        else:
            assert last, "routed layer is implemented for the final layer (latent tokens only)"
            rw16 = jnp.zeros((ROUTER_ROWS, d), F32).at[:N_EXPERTS].set(router_w[l // 2].T)
            t, idx, gates, cnt = _router(x2, mod3, lat_row, row2(g_ffn[l]), rw16, tm=lat_tm)
            counts = cnt[:N_EXPERTS, 0]
            padded = (counts + EXPERT_TM - 1) // EXPERT_TM * EXPERT_TM
            pends = jnp.cumsum(padded)
            pstarts = pends - padded
            slot_e = idx[0:2]
            slot_start = sum(jnp.where(slot_e == e, pstarts[e], 0) for e in range(N_EXPERTS))
            dest = (slot_start + idx[2:4]).reshape(-1).astype(jnp.int32)
            n_tiles = (2 * m_lat) // EXPERT_TM + N_EXPERTS
            tile_start = jnp.arange(n_tiles, dtype=jnp.int32) * EXPERT_TM
            tile_v = (tile_start < pends[-1]).astype(jnp.int32)
            last_tile = pends[-1] // EXPERT_TM - 1
            tile_x = jnp.minimum(jnp.arange(n_tiles, dtype=jnp.int32), last_tile).astype(jnp.int32)
            tile_e = jnp.minimum(jnp.searchsorted(pends, tile_x * EXPERT_TM, side='right'),
                                 N_EXPERTS - 1).astype(jnp.int32)
            pad_info = jnp.concatenate([pstarts + counts, padded - counts,
                                        (pends[-1:] // EXPERT_TM)]).astype(jnp.int32)
            buf = _scatter_rows(dest, pad_info, t, tm=lat_tm, n_tiles=n_tiles)
            ybuf = _experts(tile_e, tile_v, tile_x, buf, *moe_up_b, moe_wd_b, tf=512)
            out = _combine(dest, x2, mod3, lat_row_for(256), gates, row2(g_final), ybuf, tm=256)
    return out.reshape(b, s, d)
```

```python
import functools

import jax
import jax.numpy as jnp
from jax import lax
from jax.experimental import pallas as pl
from jax.experimental.pallas import tpu as pltpu

F32 = jnp.float32
BF16 = jnp.bfloat16

GRID_W = 64
HEAD_DIM = 128
N_Q_HEADS = 8
N_KV_HEADS = 2
GQA_GROUP = N_Q_HEADS // N_KV_HEADS
ATTN_W = N_Q_HEADS * HEAD_DIM
KV_W = N_KV_HEADS * HEAD_DIM
ROPE_THETA = 10000.0
SCONV_W = 512
SCONV_K = 3
CONF_W = 512
CONF_K = 31
N_EXPERTS = 8
EPS = 1e-6
LOG2E = 1.4426950408889634

V7X_VMEM_LIMIT_BYTES = 56 * 1024 * 1024
MOD_ROWS = 16
IN_TN = 512
EXPERT_TM = 512
FFN_TM = 1024
FFN_TF = 256


def _cparams(n_axes):
    return pltpu.CompilerParams(dimension_semantics=("arbitrary",) * n_axes,
                                vmem_limit_bytes=V7X_VMEM_LIMIT_BYTES)


def _sigmoid(x):
    return 1.0 / (1.0 + jnp.exp(-x))


NORM_CHUNK = 16
NORM_UNROLL = 4


def _norm_modulate_rows(x_ref, h_ref, g, shift, scale):
    rows = x_ref.shape[0]
    gain = g * (1.0 + scale)

    def body(c, carry):
        r = pl.multiple_of(c * NORM_CHUNK, NORM_CHUNK)
        x = x_ref[pl.ds(r, NORM_CHUNK), :]
        inv = lax.rsqrt(jnp.mean(x * x, axis=-1, keepdims=True) + EPS)
        h_ref[pl.ds(r, NORM_CHUNK), :] = (x * inv * gain + shift).astype(h_ref.dtype)
        return carry

    lax.fori_loop(0, rows // NORM_CHUNK, body, 0, unroll=NORM_UNROLL)


def _ada_kernel(c_ref, w_ref, b_ref, o_ref):
    a = c_ref[...]
    a = (a * _sigmoid(a)).astype(BF16)
    o_ref[0] = jnp.dot(a, w_ref[0].astype(BF16), preferred_element_type=F32) + b_ref[0]


def _ada(cin, w_ada, b_ada):
    depth, d, n = w_ada.shape
    tn = 1536
    return pl.pallas_call(
        _ada_kernel,
        grid=(depth, n // tn),
        in_specs=[pl.BlockSpec((MOD_ROWS, d), lambda l, j: (0, 0)),
                  pl.BlockSpec((1, d, tn), lambda l, j: (l, 0, j)),
                  pl.BlockSpec((1, 1, tn), lambda l, j: (l, 0, j))],
        out_specs=pl.BlockSpec((1, MOD_ROWS, tn), lambda l, j: (l, 0, j)),
        out_shape=jax.ShapeDtypeStruct((depth, MOD_ROWS, n), F32),
        compiler_params=_cparams(2),
        name="ada",
    )(cin, w_ada, b_ada.reshape(depth, 1, n))


def _head_norm(seg, gain):
    ms = jnp.mean(seg * seg, axis=-1, keepdims=True)
    return seg * lax.rsqrt(ms + EPS) * gain


def _rope(n, cos_t, sin_t):
    lane = lax.broadcasted_iota(jnp.int32, n.shape, 1)
    fwd = pltpu.roll(n, 32, 1)
    bwd = pltpu.roll(n, 96, 1)
    partner = jnp.where((lane // 32) % 2 == 0, bwd, fwd)
    return n * cos_t + partner * sin_t


def _in_kernel(x_ref, shift_ref, scale_ref, g_ref, w_ref, gq_ref, gk_ref, cos_ref, sin_ref,
               *rest, rope, j0, nj, q_scale, n_cast):
    cast_in, o_ref, cast_out, h_ref = rest[:n_cast], rest[n_cast], rest[n_cast + 1:2 * n_cast + 1], rest[-1]
    _norm_modulate_rows(x_ref, h_ref, g_ref[...], shift_ref[0], scale_ref[0])
    _cast_slabs(cast_in, cast_out)
    h = h_ref[...]

    def head(seg, gain, scale):
        n = _head_norm(seg, gain)
        if rope:
            n = _rope(n, cos_ref[...], sin_ref[...])
        if scale != 1.0:
            n = n * scale
        return n

    for t in range(nj):
        j = t + j0
        c0 = t * IN_TN
        acc = jnp.dot(h, w_ref[:, c0:c0 + IN_TN], preferred_element_type=F32)
        if j < 2:
            for hd in range(IN_TN // HEAD_DIM):
                sl = slice(hd * HEAD_DIM, (hd + 1) * HEAD_DIM)
                osl = slice(c0 + hd * HEAD_DIM, c0 + (hd + 1) * HEAD_DIM)
                o_ref[:, osl] = head(acc[:, sl], gq_ref[...], q_scale).astype(o_ref.dtype)
        elif j == 2:
            for hd in range(N_KV_HEADS):
                sl = slice(hd * HEAD_DIM, (hd + 1) * HEAD_DIM)
                osl = slice(c0 + hd * HEAD_DIM, c0 + (hd + 1) * HEAD_DIM)
                o_ref[:, osl] = head(acc[:, sl], gk_ref[...], 1.0).astype(o_ref.dtype)
            o_ref[:, c0 + KV_W:c0 + IN_TN] = acc[:, KV_W:].astype(o_ref.dtype)
        else:
            o_ref[:, c0:c0 + IN_TN] = acc.astype(o_ref.dtype)


def _const_spec(shape, idx):
    return pl.BlockSpec(shape, lambda i: idx, pipeline_mode=pl.Buffered(1))


def _in_proj(x2, mod3, mod_row, g, w, gq, gk, cos_t, sin_t, *, tm, rope, j0, nj, seq, cast=()):
    m, d = x2.shape
    tiles_per_seq = seq // tm
    ncols = nj * IN_TN
    assert j0 % nj == 0
    q_scale = HEAD_DIM ** -0.5 * LOG2E
    kern = functools.partial(_in_kernel, rope=rope, j0=j0, nj=nj, q_scale=q_scale, n_cast=len(cast))
    cast_arrays, cast_in, cast_out, cast_shapes = _cast_plan(cast, m // tm, lambda i: i)
    outs = pl.pallas_call(
        kern,
        grid=(m // tm,),
        in_specs=[pl.BlockSpec((tm, d), lambda i: (i, 0)),
                  pl.BlockSpec((1, 1, d), lambda i: (mod_row(i), 0, 0)),
                  pl.BlockSpec((1, 1, d), lambda i: (mod_row(i), 0, 1)),
                  _const_spec((1, d), (0, 0)),
                  _const_spec((d, ncols), (0, j0 // nj)),
                  _const_spec((1, HEAD_DIM), (0, 0)),
                  _const_spec((1, HEAD_DIM), (0, 0)),
                  pl.BlockSpec((tm, HEAD_DIM), lambda i: (i % tiles_per_seq, 0)),
                  pl.BlockSpec((tm, HEAD_DIM), lambda i: (i % tiles_per_seq, 0))] + cast_in,
        out_specs=[pl.BlockSpec((tm, ncols), lambda i: (i, 0))] + cast_out,
        out_shape=[jax.ShapeDtypeStruct((m, ncols), BF16)] + cast_shapes,
        scratch_shapes=[pltpu.VMEM((tm, d), BF16)],
        compiler_params=_cparams(1),
        name="in_proj",
    )(x2, mod3, mod3, g, w, gq, gk, cos_t, sin_t, *cast_arrays)
    return outs[0], tuple(outs[1:])


def _attn_kernel(q_ref, kc_ref, vc_ref, kl_ref, vl_ref, gg_ref, *rest, has_latent, n_cast):
    cast_in, o_ref, cast_out, acc_ref = rest[:n_cast], rest[n_cast], rest[n_cast + 1:2 * n_cast + 1], rest[-1]
    _cast_slabs(cast_in, cast_out)
    tq = q_ref.shape[0]
    dn = (((1,), (1,)), ((), ()))
    ssq = jnp.zeros((tq, 1), F32)
    for hd in range(N_Q_HEADS):
        kv = hd // GQA_GROUP
        ksl = slice(kv * HEAD_DIM, (kv + 1) * HEAD_DIM)
        q = q_ref[:, hd * HEAD_DIM:(hd + 1) * HEAD_DIM]
        s_c = lax.dot_general(q, kc_ref[:, ksl], dn, preferred_element_type=F32)
        mx = jnp.max(s_c, axis=-1, keepdims=True)
        if has_latent:
            s_l = lax.dot_general(q, kl_ref[:, ksl], dn, preferred_element_type=F32)
            mx = jnp.maximum(mx, jnp.max(s_l, axis=-1, keepdims=True))
        p_c = jnp.exp2(s_c - mx)
        den = jnp.sum(p_c, axis=-1, keepdims=True)
        o = jnp.dot(p_c.astype(BF16), vc_ref[:, ksl], preferred_element_type=F32)
        if has_latent:
            p_l = jnp.exp2(s_l - mx)
            den = den + jnp.sum(p_l, axis=-1, keepdims=True)
            o = o + jnp.dot(p_l.astype(BF16), vl_ref[:, ksl], preferred_element_type=F32)
        o = o * (1.0 / den)
        ssq = ssq + jnp.sum(o * o, axis=-1, keepdims=True)
        acc_ref[:, hd * HEAD_DIM:(hd + 1) * HEAD_DIM] = o
    inv = lax.rsqrt(ssq * (1.0 / ATTN_W) + EPS)
    o_ref[...] = (acc_ref[...] * inv * gg_ref[...]).astype(o_ref.dtype)


def _attention(qsrc, csrc, lsrc, gg, *, n_batch, q_len, tq, kc_blk, has_latent, n_ctx, seq, cast=()):
    m = qsrc.shape[0]
    tiles = q_len // tq
    kern = functools.partial(_attn_kernel, has_latent=has_latent, n_cast=len(cast))
    cast_arrays, cast_in, cast_out, cast_shapes = _cast_plan(cast, n_batch * tiles, lambda b, i: b * tiles + i)
    outs = pl.pallas_call(
        kern,
        grid=(n_batch, tiles),
        in_specs=[pl.BlockSpec((tq, ATTN_W), lambda b, i: (b * tiles + i, 0)),
                  pl.BlockSpec((n_ctx, KV_W), lambda b, i: (b, kc_blk)),
                  pl.BlockSpec((n_ctx, KV_W), lambda b, i: (b, kc_blk + 1)),
                  pl.BlockSpec((seq, KV_W), lambda b, i: (b, ATTN_W // KV_W)),
                  pl.BlockSpec((seq, KV_W), lambda b, i: (b, ATTN_W // KV_W + 1)),
                  pl.BlockSpec((1, ATTN_W), lambda b, i: (0, 0))] + cast_in,
        out_specs=[pl.BlockSpec((tq, ATTN_W), lambda b, i: (b * tiles + i, 0))] + cast_out,
        out_shape=[jax.ShapeDtypeStruct((m, ATTN_W), BF16)] + cast_shapes,
        scratch_shapes=[pltpu.VMEM((tq, ATTN_W), F32)],
        compiler_params=_cparams(2),
        name="attention",
    )(qsrc, csrc, csrc, lsrc, lsrc, gg, *cast_arrays)
    return outs[0], tuple(outs[1:])


MIX_CHUNK = 64
SCONV_PAD = 8
CONF_PAD = 16


LANES = 128
SUBLANES = 8
N_LANE_GROUPS = SCONV_W // LANES


def _window_conv(p_ref, slot, w_ref, group, first, n_taps, ch):
    lanes = slice(group * LANES, (group + 1) * LANES)
    win = p_ref[slot, :, lanes]
    n_rows = win.shape[0]
    acc = None
    for phase in range(SUBLANES):
        taps = [k for k in range(n_taps) if (first + k) % SUBLANES == phase]
        if not taps:
            continue
        shifted = win if phase == 0 else pltpu.roll(win, n_rows - phase, 0)
        for k in taps:
            base = first + k - phase
            term = w_ref[k:k + 1, lanes] * shifted[base:base + ch]
            acc = term if acc is None else acc + term
    return acc


def _mix_kernel(sb_ref, sg_ref, sx_ref, ga_ref, gb_ref, sw_ref, dw_ref, db_ref, lng_ref, lnb_ref,
                pw_ref, pb_ref, ggs_ref, ggc_ref, o_ref, p1_ref, p2_ref, a_ref):
    seq = sb_ref.shape[0]
    ch = MIX_CHUNK
    n_chunks = seq // ch
    hs, hc = SCONV_PAD, CONF_PAD
    p1_ref[1, 0:hs, :] = jnp.zeros((hs, SCONV_W), F32)
    p1_ref[n_chunks, ch + hs:ch + 2 * hs, :] = jnp.zeros((hs, SCONV_W), F32)
    p2_ref[1, 0:hc, :] = jnp.zeros((hc, CONF_W), F32)
    p2_ref[n_chunks, ch + hc:ch + 2 * hc, :] = jnp.zeros((hc, CONF_W), F32)

    def fill(c, carry):
        r = pl.multiple_of(c * ch, ch)
        rows = pl.ds(r, ch)
        v = sg_ref[rows, :].astype(F32) * sx_ref[rows, :].astype(F32)
        p1_ref[c + 1, hs:hs + ch, :] = v
        p1_ref[c, ch + hs:ch + 2 * hs, :] = v[0:hs]
        p1_ref[c + 2, 0:hs, :] = v[ch - hs:ch]
        u = ga_ref[rows, :].astype(F32) * _sigmoid(gb_ref[rows, :].astype(F32))
        p2_ref[c + 1, hc:hc + ch, :] = u
        p2_ref[c, ch + hc:ch + 2 * hc, :] = u[0:hc]
        p2_ref[c + 2, 0:hc, :] = u[ch - hc:ch]
        return carry

    lax.fori_loop(0, n_chunks, fill, 0)

    def conv(c, carry):
        r = pl.multiple_of(c * ch, ch)
        rows = pl.ds(r, ch)
        acc = jnp.concatenate(
            [_window_conv(p1_ref, c + 1, sw_ref, g, hs - SCONV_K // 2, SCONV_K, ch) for g in range(N_LANE_GROUPS)],
            axis=1)
        ys = sb_ref[rows, :].astype(F32) * acc
        ms = jnp.mean(ys * ys, axis=-1, keepdims=True)
        o_ref[rows, 0:SCONV_W] = (ys * lax.rsqrt(ms + EPS) * ggs_ref[...]).astype(o_ref.dtype)

        u = db_ref[...] + jnp.concatenate(
            [_window_conv(p2_ref, c + 1, dw_ref, g, hc - CONF_K // 2, CONF_K, ch) for g in range(N_LANE_GROUPS)],
            axis=1)
        mu = jnp.mean(u, axis=-1, keepdims=True)
        uc = u - mu
        var = jnp.mean(uc * uc, axis=-1, keepdims=True)
        v = uc * lax.rsqrt(var + EPS) * lng_ref[...] + lnb_ref[...]
        a_ref[rows, :] = (v * _sigmoid(v)).astype(a_ref.dtype)
        return carry

    lax.fori_loop(0, n_chunks, conv, 0)

    yc = jnp.dot(a_ref[...], pw_ref[...], preferred_element_type=F32) + pb_ref[...]
    ms = jnp.mean(yc * yc, axis=-1, keepdims=True)
    o_ref[:, SCONV_W:] = (yc * lax.rsqrt(ms + EPS) * ggc_ref[...]).astype(o_ref.dtype)


def _local_mixers(src, sconv_w, conf_dw, conf_db, ln_g, ln_b, pw, pb, ggs, ggc, *, n_seq, seq):
    w = SCONV_W
    col = lambda k: pl.BlockSpec((seq, w), lambda b: (b, k))
    vec = lambda n: pl.BlockSpec((1, n), lambda b: (0, 0))
    return pl.pallas_call(
        _mix_kernel,
        grid=(n_seq,),
        in_specs=[col(3), col(4), col(5), col(6), col(7),
                  pl.BlockSpec((SCONV_K, w), lambda b: (0, 0)),
                  pl.BlockSpec((CONF_K, w), lambda b: (0, 0)),
                  vec(w), vec(w), vec(w),
                  pl.BlockSpec((w, w), lambda b: (0, 0)),
                  vec(w), vec(w), vec(w)],
        out_specs=pl.BlockSpec((seq, 2 * w), lambda b: (b, 0)),
        out_shape=jax.ShapeDtypeStruct((src.shape[0], 2 * w), BF16),
        scratch_shapes=[pltpu.VMEM((seq // MIX_CHUNK + 2, MIX_CHUNK + 2 * SCONV_PAD, w), F32),
                        pltpu.VMEM((seq // MIX_CHUNK + 2, MIX_CHUNK + 2 * CONF_PAD, w), F32),
                        pltpu.VMEM((seq, w), BF16)],
        compiler_params=_cparams(1),
        name="local_mixers",
    )(src, src, src, src, src, sconv_w, conf_dw, conf_db, ln_g, ln_b, pw, pb, ggs, ggc)


OUT_TN = 512


def _out_kernel(a_ref, m_ref, wa_ref, wm_ref, x_ref, gate_ref, o_ref):
    a = a_ref[...]
    mx = m_ref[...]
    for t in range(o_ref.shape[1] // OUT_TN):
        sl = slice(t * OUT_TN, (t + 1) * OUT_TN)
        acc = (jnp.dot(a, wa_ref[:, sl], preferred_element_type=F32)
               + jnp.dot(mx, wm_ref[:, sl], preferred_element_type=F32))
        o_ref[:, sl] = x_ref[:, sl] + gate_ref[0, :, sl] * acc


def _out_proj(att, mix, w, x2, mod3, mod_row, *, tm):
    m, d = x2.shape
    half = att.shape[1]
    return pl.pallas_call(
        _out_kernel,
        grid=(m // tm,),
        in_specs=[pl.BlockSpec((tm, half), lambda i: (i, 0)),
                  pl.BlockSpec((tm, half), lambda i: (i, 0)),
                  _const_spec((half, d), (0, 0)),
                  _const_spec((half, d), (1, 0)),
                  pl.BlockSpec((tm, d), lambda i: (i, 0)),
                  pl.BlockSpec((1, 1, d), lambda i: (mod_row(i), 0, 2))],
        out_specs=pl.BlockSpec((tm, d), lambda i: (i, 0)),
        out_shape=jax.ShapeDtypeStruct((m, d), F32),
        compiler_params=_cparams(1),
        name="out_proj",
    )(att, mix, w, w, x2, mod3)


MXU_WIDTH = 256


def _swiglu_accumulate(h, wg_ref, wu_ref, wd_refs, o_ref, assign=False):
    tm, tf = h.shape[0], wg_ref.shape[1]
    if tf % (2 * MXU_WIDTH) == 0:
        splits = [(slice(0, tm), slice(p * tf // 2, (p + 1) * tf // 2)) for p in range(2)]
    else:
        splits = [(slice(p * tm // 2, (p + 1) * tm // 2), slice(0, tf)) for p in range(2)]
    ups = [(jnp.dot(h[rows], wg_ref[:, cols], preferred_element_type=F32),
            jnp.dot(h[rows], wu_ref[:, cols], preferred_element_type=F32)) for rows, cols in splits]
    acts = [(gv * _sigmoid(gv) * uv).astype(BF16) for gv, uv in ups]
    same_rows = splits[0][0] == splits[1][0]
    col = 0
    for wd_ref in wd_refs:
        width = wd_ref.shape[1]
        downs = [jnp.dot(act, wd_ref[cols, :], preferred_element_type=F32) for act, (_, cols) in zip(acts, splits)]
        pieces = [(splits[0][0], downs[0] + downs[1])] if same_rows else list(zip((s[0] for s in splits), downs))
        for rows, part in pieces:
            if assign:
                o_ref[rows, col:col + width] = part
            else:
                o_ref[rows, col:col + width] += part
        col += width


def _ffn_kernel(x_ref, shift_ref, scale_ref, gate_ref, g_ref, wg_ref, wu_ref, wd_ref, *rest, n_cast):
    cast_in, o_ref, cast_out, h_ref = rest[:n_cast], rest[n_cast], rest[n_cast + 1:2 * n_cast + 1], rest[-1]
    j = pl.program_id(1)

    _cast_slabs(cast_in, cast_out)

    @pl.when(j == 0)
    def _():
        _norm_modulate_rows(x_ref, h_ref, g_ref[...], shift_ref[0], scale_ref[0])
        o_ref[...] = jnp.zeros(o_ref.shape, o_ref.dtype)

    _swiglu_accumulate(h_ref[...], wg_ref, wu_ref, (wd_ref,), o_ref)

    @pl.when(j == pl.num_programs(1) - 1)
    def _():
        o_ref[...] = x_ref[...] + gate_ref[0] * o_ref[...]


BF16_SUBLANE_TILE = 16


def _slab_rows(rows, n_lead, n_steps):
    for height in range(BF16_SUBLANE_TILE, rows + 1, BF16_SUBLANE_TILE):
        if rows % height == 0 and n_lead * (rows // height) <= n_steps:
            return height
    raise ValueError(f"no slab height for rows={rows}, n_lead={n_lead}, n_steps={n_steps}")


def _cast_plan(jobs, n_steps, step_of):
    arrays, in_specs, out_specs, out_shapes = [], [], [], []
    for arr, col_block, n_col_blocks in jobs:
        n_lead, rows, cols = arr.shape
        width = cols // n_col_blocks
        height = _slab_rows(rows, n_lead, n_steps)
        per_lead = rows // height
        last = n_lead * per_lead - 1

        def index(*grid_idx, per_lead=per_lead, last=last, col=0):
            s = jnp.minimum(step_of(*grid_idx), last)
            return (s // per_lead, s % per_lead, col)

        arrays.append(arr)
        in_specs.append(pl.BlockSpec((1, height, width), functools.partial(index, col=col_block)))
        out_specs.append(pl.BlockSpec((1, height, width), index))
        out_shapes.append(jax.ShapeDtypeStruct((n_lead, rows, width), BF16))
    return arrays, in_specs, out_specs, out_shapes


def _cast_slabs(cast_in, cast_out):
    for src, dst in zip(cast_in, cast_out):
        dst[...] = src[...].astype(dst.dtype)


def _dense_ffn(x2, mod3, mod_row, g, wg, wu, wd, *, tm, tf, cast=()):
    m, d = x2.shape
    f = wg.shape[2]
    nf = f // tf
    n_steps = (m // tm) * nf
    modspec = lambda k: pl.BlockSpec((1, 1, d), lambda i, j: (mod_row(i), 0, k))
    cast_arrays, cast_in, cast_out, cast_shapes = _cast_plan(cast, n_steps, lambda i, j: i * nf + j)
    outs = pl.pallas_call(
        functools.partial(_ffn_kernel, n_cast=len(cast)),
        grid=(m // tm, nf),
        in_specs=[pl.BlockSpec((tm, d), lambda i, j: (i, 0), pipeline_mode=pl.Buffered(1)),
                  modspec(3), modspec(4), modspec(5),
                  pl.BlockSpec((1, d), lambda i, j: (0, 0)),
                  pl.BlockSpec((None, d, tf), lambda i, j: (0, 0, j)),
                  pl.BlockSpec((None, d, tf), lambda i, j: (0, 0, j)),
                  pl.BlockSpec((None, tf, d), lambda i, j: (0, j, 0))] + cast_in,
        out_specs=[pl.BlockSpec((tm, d), lambda i, j: (i, 0))] + cast_out,
        out_shape=[jax.ShapeDtypeStruct((m, d), F32)] + cast_shapes,
        scratch_shapes=[pltpu.VMEM((tm, d), BF16)],
        compiler_params=_cparams(2),
        name="dense_ffn",
    )(x2, mod3, mod3, mod3, g, wg, wu, wd, *cast_arrays)
    return outs[0], tuple(outs[1:])


ROUTER_ROWS = 16


U32 = jnp.uint32


def _pack_bf16_pair(lo, hi):
    return (pltpu.bitcast(lo, U32) >> 16) | pltpu.bitcast(hi, U32)


def _unpack_bf16_pair(packed):
    lo = pltpu.bitcast(packed << 16, F32)
    hi = pltpu.bitcast(packed & jnp.uint32(0xFFFF0000), F32)
    return lo.astype(BF16), hi.astype(BF16)


def _router_kernel(x_ref, shift_ref, scale_ref, g_ref, rw_ref, tp_ref, idx_ref, gate_ref, cnt_ref,
                   t_ref, tri_ref, base_ref):
    tm = x_ref.shape[0]

    @pl.when(pl.program_id(0) == 0)
    def _():
        r = lax.broadcasted_iota(jnp.int32, (tm, tm), 0)
        c = lax.broadcasted_iota(jnp.int32, (tm, tm), 1)
        tri_ref[...] = jnp.where(r < c, 1.0, 0.0).astype(BF16)
        base_ref[...] = jnp.zeros(base_ref.shape, F32)

    _norm_modulate_rows(x_ref, t_ref, g_ref[...], shift_ref[0], scale_ref[0])

    t = t_ref[...]
    t_hi = t.astype(BF16)
    t_hi32 = t_hi.astype(F32)
    t_lo = (t - t_hi32).astype(BF16)
    half = t.shape[1] // 2
    tp_ref[...] = _pack_bf16_pair(t_hi32[:, :half], t_hi32[:, half:])
    w = rw_ref[...]
    w_hi = w.astype(BF16)
    w_lo = (w - w_hi.astype(F32)).astype(BF16)
    dn = (((1,), (1,)), ((), ()))
    logits = (lax.dot_general(w_hi, t_hi, dn, preferred_element_type=F32)
              + lax.dot_general(w_lo, t_hi, dn, preferred_element_type=F32)
              + lax.dot_general(w_hi, t_lo, dn, preferred_element_type=F32))

    e = lax.broadcasted_iota(jnp.int32, (ROUTER_ROWS, tm), 0).astype(F32)
    neg = jnp.float32(-jnp.inf)
    lg = jnp.where(e < N_EXPERTS, logits, neg)
    m1 = jnp.max(lg, axis=0, keepdims=True)
    i1 = jnp.min(jnp.where(lg == m1, e, float(ROUTER_ROWS)), axis=0, keepdims=True)
    lg2 = jnp.where(e == i1, neg, lg)
    m2 = jnp.max(lg2, axis=0, keepdims=True)
    i2 = jnp.min(jnp.where(lg2 == m2, e, float(ROUTER_ROWS)), axis=0, keepdims=True)
    ex = jnp.exp(m2 - m1)
    den = 1.0 + ex
    gate_ref[0:1, :] = 1.0 / den
    gate_ref[1:2, :] = ex / den

    hit1 = e == i1
    hit2 = e == i2
    onehot = jnp.where(hit1 | hit2, 1.0, 0.0)
    prefix = jnp.dot(onehot.astype(BF16), tri_ref[...], preferred_element_type=F32) + base_ref[:, 0:1]
    r1 = jnp.sum(jnp.where(hit1, prefix, 0.0), axis=0, keepdims=True)
    r2 = jnp.sum(jnp.where(hit2, prefix, 0.0), axis=0, keepdims=True)
    idx_ref[0:1, :] = i1.astype(jnp.int32)
    idx_ref[1:2, :] = i2.astype(jnp.int32)
    idx_ref[2:3, :] = r1.astype(jnp.int32)
    idx_ref[3:4, :] = r2.astype(jnp.int32)
    base_ref[...] = base_ref[...] + jnp.sum(onehot, axis=1, keepdims=True)
    cnt_ref[...] = base_ref[...].astype(jnp.int32)


def _router(x2, mod3, mod_row, g, rw16, *, tm):
    m, d = x2.shape
    modspec = lambda k: pl.BlockSpec((1, 1, d), lambda i: (mod_row(i), 0, k))
    return pl.pallas_call(
        _router_kernel,
        grid=(m // tm,),
        in_specs=[pl.BlockSpec((tm, d), lambda i: (i, 0)),
                  modspec(3), modspec(4),
                  pl.BlockSpec((1, d), lambda i: (0, 0)),
                  pl.BlockSpec((ROUTER_ROWS, d), lambda i: (0, 0))],
        out_specs=[pl.BlockSpec((tm, d // 2), lambda i: (i, 0)),
                   pl.BlockSpec((4, tm), lambda i: (0, i)),
                   pl.BlockSpec((2, tm), lambda i: (0, i)),
                   pl.BlockSpec((ROUTER_ROWS, 128), lambda i: (0, 0))],
        out_shape=[jax.ShapeDtypeStruct((m, d // 2), U32),
                   jax.ShapeDtypeStruct((4, m), jnp.int32),
                   jax.ShapeDtypeStruct((2, m), F32),
                   jax.ShapeDtypeStruct((ROUTER_ROWS, 128), jnp.int32)],
        scratch_shapes=[pltpu.VMEM((tm, d), F32), pltpu.VMEM((tm, tm), BF16),
                        pltpu.VMEM((ROUTER_ROWS, 128), F32)],
        compiler_params=_cparams(1),
        name="router",
    )(x2, mod3, mod3, g, rw16)


def _out_route_kernel(a_ref, m_ref, wa_ref, wm_ref, x_ref, gate_ref, shift_ref, scale_ref, g_ref, rw_ref,
                      o_ref, tp_ref, idx_ref, gates_ref, cnt_ref, t_ref, tri_ref, base_ref):
    _out_kernel(a_ref, m_ref, wa_ref, wm_ref, x_ref, gate_ref, o_ref)
    _router_kernel(o_ref, shift_ref, scale_ref, g_ref, rw_ref, tp_ref, idx_ref, gates_ref, cnt_ref,
                   t_ref, tri_ref, base_ref)


def _out_proj_route(att, mix, w, x2, mod3, mod_row, g, rw16, *, tm):
    m, d = x2.shape
    half = att.shape[1]
    modspec = lambda k: pl.BlockSpec((1, 1, d), lambda i: (mod_row(i), 0, k))
    return pl.pallas_call(
        _out_route_kernel,
        grid=(m // tm,),
        in_specs=[pl.BlockSpec((tm, half), lambda i: (i, 0)),
                  pl.BlockSpec((tm, half), lambda i: (i, 0)),
                  _const_spec((half, d), (0, 0)),
                  _const_spec((half, d), (1, 0)),
                  pl.BlockSpec((tm, d), lambda i: (i, 0)),
                  modspec(2), modspec(3), modspec(4),
                  _const_spec((1, d), (0, 0)),
                  _const_spec((ROUTER_ROWS, d), (0, 0))],
        out_specs=[pl.BlockSpec((tm, d), lambda i: (i, 0)),
                   pl.BlockSpec((tm, d // 2), lambda i: (i, 0)),
                   pl.BlockSpec((4, tm), lambda i: (0, i)),
                   pl.BlockSpec((2, tm), lambda i: (0, i)),
                   pl.BlockSpec((ROUTER_ROWS, 128), lambda i: (0, 0))],
        out_shape=[jax.ShapeDtypeStruct((m, d), F32),
                   jax.ShapeDtypeStruct((m, d // 2), U32),
                   jax.ShapeDtypeStruct((4, m), jnp.int32),
                   jax.ShapeDtypeStruct((2, m), F32),
                   jax.ShapeDtypeStruct((ROUTER_ROWS, 128), jnp.int32)],
        scratch_shapes=[pltpu.VMEM((tm, d), F32), pltpu.VMEM((tm, tm), BF16),
                        pltpu.VMEM((ROUTER_ROWS, 128), F32)],
        compiler_params=_cparams(1),
        name="out_proj_route",
    )(att, mix, w, w, x2, mod3, mod3, mod3, g, rw16)


def _row_copy(src_ref, src_row, dst_ref, dst_row, sem):
    return pltpu.make_async_copy(src_ref.at[pl.ds(src_row, 1), :], dst_ref.at[pl.ds(dst_row, 1), :], sem)


DMA_ISSUE_UNROLL = 8


def _scatter_kernel(dest_ref, pad_ref, t_ref, buf_ref, zero_ref, sem, zsem, *, n_tok, n_tiles):
    tm = t_ref.shape[0]
    base = pl.program_id(0) * tm

    @pl.when(pl.program_id(0) == 0)
    def _():
        zero_ref[...] = jnp.zeros(zero_ref.shape, zero_ref.dtype)
        for e in range(N_EXPERTS):
            start, length = pad_ref[e], pad_ref[N_EXPERTS + e]

            def fill(r, carry, start=start):
                _row_copy(zero_ref, 0, buf_ref, start + r, zsem).start()
                return carry

            def drain(r, carry):
                _row_copy(zero_ref, 0, buf_ref, 0, zsem).wait()
                return carry

            lax.fori_loop(0, length, fill, 0)
            lax.fori_loop(0, length, drain, 0)

        def tile_copy(tile):
            row0 = pl.multiple_of(tile * EXPERT_TM, EXPERT_TM)
            return pltpu.make_async_copy(zero_ref, buf_ref.at[pl.ds(row0, EXPERT_TM), :], zsem)

        def fill_tile(tile, carry):
            tile_copy(tile).start()
            return carry

        def drain_tile(tile, carry):
            tile_copy(tile).wait()
            return carry

        lax.fori_loop(pad_ref[2 * N_EXPERTS], n_tiles, fill_tile, 0)
        lax.fori_loop(pad_ref[2 * N_EXPERTS], n_tiles, drain_tile, 0)

    def start(r, carry):
        for k in range(2):
            _row_copy(t_ref, r, buf_ref, dest_ref[k * n_tok + base + r], sem).start()
        return carry

    lax.fori_loop(0, tm, start, 0, unroll=DMA_ISSUE_UNROLL)
    for k in range(2):
        pltpu.make_async_copy(t_ref, buf_ref.at[pl.ds(0, tm), :], sem).wait()


def _scatter_rows(dest, pad_info, t, *, tm, n_tiles):
    m, d = t.shape
    grid_spec = pltpu.PrefetchScalarGridSpec(
        num_scalar_prefetch=2,
        grid=(m // tm,),
        in_specs=[pl.BlockSpec((tm, d), lambda i, dest, pad: (i, 0))],
        out_specs=pl.BlockSpec(memory_space=pl.ANY),
        scratch_shapes=[pltpu.VMEM((EXPERT_TM, d), t.dtype),
                        pltpu.SemaphoreType.DMA(()), pltpu.SemaphoreType.DMA(())],
    )
    return pl.pallas_call(
        functools.partial(_scatter_kernel, n_tok=m, n_tiles=n_tiles),
        grid_spec=grid_spec,
        out_shape=jax.ShapeDtypeStruct((n_tiles * EXPERT_TM, d), t.dtype),
        compiler_params=_cparams(1),
        name="moe_scatter",
    )(dest, pad_info, t)


N_WEIGHT_SLOTS = 3


def _expert_kernel(te_ref, tv_ref, tx_ref, x_ref, *rest, n_wd, tf, nf):
    del tx_ref
    w_hbm = rest[:2 + n_wd]
    o_ref, xb_ref = rest[2 + n_wd], rest[3 + n_wd]
    w_buf = rest[4 + n_wd:6 + 2 * n_wd]
    sem = rest[-1]
    i = pl.program_id(0)
    n = pl.num_programs(0)

    def copies(tile, j, slot):
        e = te_ref[tile]
        cols = pl.ds(j * tf, tf)
        srcs = [w_hbm[0].at[e, :, cols], w_hbm[1].at[e, :, cols]] + [w.at[e, cols, :] for w in w_hbm[2:]]
        return [pltpu.make_async_copy(src, buf.at[slot], sem.at[slot, k])
                for k, (src, buf) in enumerate(zip(srcs, w_buf))]

    valid = tv_ref[i] == 1
    nxt = jnp.minimum(i + 1, n - 1)
    next_live = jnp.logical_and(i + 1 < n, tv_ref[nxt] == 1)

    @pl.when(jnp.logical_not(valid))
    def _():
        o_ref[...] = jnp.zeros(o_ref.shape, o_ref.dtype)

    @pl.when(jnp.logical_and(valid, i == 0))
    def _():
        for c in copies(i, 0, 0):
            c.start()

    @pl.when(valid)
    def _():
        half = x_ref.shape[1]
        xb_ref[:, :half], xb_ref[:, half:] = _unpack_bf16_pair(x_ref[...])
        xb = xb_ref[...]
        for j in range(nf):
            slot = j % N_WEIGHT_SLOTS
            ahead = copies(i, j + 1, (j + 1) % N_WEIGHT_SLOTS) if j + 1 < nf else copies(nxt, 0, 0)
            for c in ahead:
                c.start()
            for c in copies(i, j, slot):
                c.wait()
            _swiglu_accumulate(xb, w_buf[0].at[slot], w_buf[1].at[slot], [b.at[slot] for b in w_buf[2:]],
                               o_ref, assign=j == 0)

    @pl.when(jnp.logical_and(valid, jnp.logical_not(next_live)))
    def _():
        for c in copies(nxt, 0, 0):
            c.wait()


def _experts(tile_e, tile_v, tile_x, buf, wg, wu, wds, *, tf):
    rows, d_packed = buf.shape
    d = wg.shape[1]
    f = wg.shape[2]
    nf = f // tf
    assert nf % N_WEIGHT_SLOTS != 1, "slot 0 must be free while the last hidden tile computes"
    any_spec = pl.BlockSpec(memory_space=pl.ANY)
    grid_spec = pltpu.PrefetchScalarGridSpec(
        num_scalar_prefetch=3,
        grid=(rows // EXPERT_TM,),
        in_specs=[pl.BlockSpec((EXPERT_TM, d_packed), lambda i, te, tv, tx: (tx[i], 0))]
                 + [any_spec] * (2 + len(wds)),
        out_specs=pl.BlockSpec((EXPERT_TM, d), lambda i, te, tv, tx: (i, 0)),
        scratch_shapes=[pltpu.VMEM((EXPERT_TM, d), BF16),
                        pltpu.VMEM((N_WEIGHT_SLOTS, d, tf), BF16),
                        pltpu.VMEM((N_WEIGHT_SLOTS, d, tf), BF16)]
                       + [pltpu.VMEM((N_WEIGHT_SLOTS, tf, wd.shape[2]), BF16) for wd in wds]
                       + [pltpu.SemaphoreType.DMA((N_WEIGHT_SLOTS, 2 + len(wds)))],
    )
    return pl.pallas_call(
        functools.partial(_expert_kernel, n_wd=len(wds), tf=tf, nf=nf),
        grid_spec=grid_spec,
        out_shape=jax.ShapeDtypeStruct((rows, d), F32),
        compiler_params=_cparams(1),
        name="moe_experts",
    )(tile_e, tile_v, tile_x, buf, wg, wu, *wds)


def _combine_kernel(dest_ref, x_ref, gate5_ref, gates_ref, gf_ref, ybuf_ref, o_ref, rows_ref, sem, *, n_tok):
    tm = x_ref.shape[0]
    i = pl.program_id(0)

    def issue(tile, slot):
        base = tile * tm

        def start(r, carry):
            for k in range(2):
                _row_copy(ybuf_ref, dest_ref[k * n_tok + base + r], rows_ref.at[slot, k], r, sem.at[slot]).start()
            return carry

        lax.fori_loop(0, tm, start, 0, unroll=DMA_ISSUE_UNROLL)

    @pl.when(i == 0)
    def _():
        issue(0, 0)

    @pl.when(i + 1 < pl.num_programs(0))
    def _():
        issue(i + 1, (i + 1) % 2)

    slot = i % 2
    for k in range(2):
        pltpu.make_async_copy(ybuf_ref.at[pl.ds(0, tm), :], rows_ref.at[slot, k], sem.at[slot]).wait()

    eye = (lax.broadcasted_iota(jnp.int32, (tm, tm), 0) == lax.broadcasted_iota(jnp.int32, (tm, tm), 1))
    g0 = jnp.sum(jnp.where(eye, gates_ref[0:1, :], 0.0), axis=1, keepdims=True)
    g1 = jnp.sum(jnp.where(eye, gates_ref[1:2, :], 0.0), axis=1, keepdims=True)
    y = g0 * rows_ref[slot, 0] + g1 * rows_ref[slot, 1]
    xn = x_ref[...] + gate5_ref[0] * y
    ms = jnp.mean(xn * xn, axis=-1, keepdims=True)
    o_ref[...] = xn * lax.rsqrt(ms + EPS) * gf_ref[...]


def _combine(dest, x2, mod3, mod_row, gates, g_final, ybuf, *, tm):
    m, d = x2.shape
    grid_spec = pltpu.PrefetchScalarGridSpec(
        num_scalar_prefetch=1,
        grid=(m // tm,),
        in_specs=[pl.BlockSpec((tm, d), lambda i, dest: (i, 0)),
                  pl.BlockSpec((1, 1, d), lambda i, dest: (mod_row(i), 0, 5)),
                  pl.BlockSpec((2, tm), lambda i, dest: (0, i)),
                  pl.BlockSpec((1, d), lambda i, dest: (0, 0)),
                  pl.BlockSpec(memory_space=pl.ANY)],
        out_specs=pl.BlockSpec((tm, d), lambda i, dest: (i, 0)),
        scratch_shapes=[pltpu.VMEM((2, 2, tm, d), F32), pltpu.SemaphoreType.DMA((2,))],
    )
    return pl.pallas_call(
        functools.partial(_combine_kernel, n_tok=m),
        grid_spec=grid_spec,
        out_shape=jax.ShapeDtypeStruct((m, d), F32),
        compiler_params=_cparams(1),
        name="moe_combine",
    )(dest, x2, mod3, gates, g_final, ybuf)


def _rope_tables(seq):
    rows = seq // GRID_W
    row = jnp.repeat(jnp.arange(rows), GRID_W).astype(F32)
    col = jnp.tile(jnp.arange(GRID_W), rows).astype(F32)
    axis_dim = HEAD_DIM // 2
    inv_freq = ROPE_THETA ** (-jnp.arange(0, axis_dim, 2, dtype=F32) / axis_dim)
    ang_r = row[:, None] * inv_freq
    ang_c = col[:, None] * inv_freq
    cr, sr, cc, sc = jnp.cos(ang_r), jnp.sin(ang_r), jnp.cos(ang_c), jnp.sin(ang_c)
    return (jnp.concatenate([cr, cr, cc, cc], axis=-1),
            jnp.concatenate([-sr, sr, -sc, sc], axis=-1))


def kernel(x, c, ctx, c_ctx, w_ada, b_ada, g_mix, w_in, g_q, g_k, sconv_w, conf_dw, conf_db, conf_ln_g,
           conf_ln_b, conf_pw, conf_pb, g_group, w_o, g_ffn, dense_wg, dense_wu, dense_wd, router_w,
           moe_wg, moe_wu, moe_wd, g_final):
    b, s, d = x.shape
    n_ctx = ctx.shape[1]
    depth = w_ada.shape[0]
    assert b + 1 <= MOD_ROWS and depth == 2
    m_lat, m_ctx = b * s, b * n_ctx
    ctx_row = b

    cin = jnp.concatenate([c, c_ctx[None, :], jnp.zeros((MOD_ROWS - b - 1, d), F32)], axis=0)
    mod = _ada(cin, w_ada, b_ada)
    cos_t, sin_t = _rope_tables(s)
    row2 = lambda v: v.reshape(1, -1)

    lat_tm = 512
    lat_row_for = lambda tm: (lambda i: i // (s // tm))
    lat_row = lat_row_for(lat_tm)
    ctx_tm = 512
    ctx_mod_row = lambda i: ctx_row

    x2 = x.reshape(m_lat, d)
    xc2 = ctx.reshape(m_ctx, d)
    out = None
    moe_wd_b = ()
    for l in range(depth):
        last = l == depth - 1
        mod3 = mod[l].reshape(MOD_ROWS, 1, 6 * d)
        w_in_b = w_in[l].astype(BF16)
        w_o_b = w_o[l].astype(BF16)
        pw_b = conf_pw[l].astype(BF16)
        gg = g_group[l]
        gga, ggs, ggc = row2(gg[:ATTN_W]), row2(gg[ATTN_W:ATTN_W + SCONV_W]), row2(gg[ATTN_W + SCONV_W:])
        mixer_args = (sconv_w[l], conf_dw[l], row2(conf_db[l]), row2(conf_ln_g[l]), row2(conf_ln_b[l]),
                      pw_b, row2(conf_pb[l]), ggs, ggc)
        in_args = (row2(g_mix[l]), w_in_b, row2(g_q[l]), row2(g_k[l]), cos_t, sin_t)

        dense_here = l % 2 == 0
        k_dense = l // 2
        cast = (tuple((w[k_dense:k_dense + 1], 0, 1) for w in (dense_wg, dense_wu, dense_wd))
                if dense_here else ())
        k_moe = (l + 1) // 2 if dense_here else l // 2
        moe_wd_job = ((moe_wd[k_moe], l % 2, 2),) if k_moe < moe_wd.shape[0] else ()
        p_lat, dense_w_b = _in_proj(x2, mod3, lat_row, *in_args, tm=lat_tm, rope=True, j0=0, nj=8, seq=s,
                                    cast=cast)
        if last:
            p_ctx, _ = _in_proj(xc2, mod3, ctx_mod_row, *in_args, tm=ctx_tm, rope=False, j0=2, nj=1, seq=ctx_tm)
            kc_blk = 0
        else:
            p_ctx, _ = _in_proj(xc2, mod3, ctx_mod_row, *in_args, tm=ctx_tm, rope=False, j0=0, nj=8, seq=ctx_tm)
            kc_blk = ATTN_W // KV_W
        att, moe_wd_half = _attention(p_lat, p_ctx, p_lat, gga, n_batch=b, q_len=s, tq=512, kc_blk=kc_blk,
                                      has_latent=True, n_ctx=n_ctx, seq=s, cast=moe_wd_job)
        moe_wd_b = moe_wd_b + moe_wd_half
        mix = _local_mixers(p_lat, *mixer_args, n_seq=b, seq=s)
        routed_here = l % 2 == 1
        if routed_here:
            rw16 = jnp.zeros((ROUTER_ROWS, d), F32).at[:N_EXPERTS].set(router_w[l // 2].T)
            x2, t, idx, gates, cnt = _out_proj_route(att, mix, w_o_b, x2, mod3, lat_row, row2(g_ffn[l]), rw16,
                                                     tm=lat_tm)
        else:
            x2 = _out_proj(att, mix, w_o_b, x2, mod3, lat_row, tm=lat_tm)
        if not last:
            att_c, _ = _attention(p_ctx, p_ctx, p_ctx, gga, n_batch=b, q_len=n_ctx, tq=n_ctx, kc_blk=kc_blk,
                                  has_latent=False, n_ctx=n_ctx, seq=n_ctx)
            mix_c = _local_mixers(p_ctx, *mixer_args, n_seq=b, seq=n_ctx)
            xc2 = _out_proj(att_c, mix_c, w_o_b, xc2, mod3, ctx_mod_row, tm=ctx_tm)

        if l % 2 == 0:
            wg_b, wu_b, wd_b = dense_w_b
            ffn = functools.partial(_dense_ffn, g=row2(g_ffn[l]), wg=wg_b, wu=wu_b, wd=wd_b,
                                    tm=FFN_TM, tf=FFN_TF)
            routed_next = l + 1 < depth and (l + 1) % 2 == 1
            cast = ((moe_wg[(l + 1) // 2], 0, 1), (moe_wu[(l + 1) // 2], 0, 1)) if routed_next else ()
            x2, moe_up_b = ffn(x2, mod3, lat_row_for(FFN_TM), cast=cast)
            if not last:
                xc2, _ = ffn(xc2, mod3, ctx_mod_row)
        else:
            assert last, "routed layer is implemented for the final layer (latent tokens only)"
            counts = cnt[:N_EXPERTS, 0]
            padded = (counts + EXPERT_TM - 1) // EXPERT_TM * EXPERT_TM
            pends = jnp.cumsum(padded)
            pstarts = pends - padded
            slot_e = idx[0:2]
            slot_start = sum(jnp.where(slot_e == e, pstarts[e], 0) for e in range(N_EXPERTS))
            dest = (slot_start + idx[2:4]).reshape(-1).astype(jnp.int32)
            n_tiles = (2 * m_lat) // EXPERT_TM + N_EXPERTS
            tile_start = jnp.arange(n_tiles, dtype=jnp.int32) * EXPERT_TM
            tile_v = (tile_start < pends[-1]).astype(jnp.int32)
            last_tile = pends[-1] // EXPERT_TM - 1
            tile_x = jnp.minimum(jnp.arange(n_tiles, dtype=jnp.int32), last_tile).astype(jnp.int32)
            tile_e = jnp.minimum(jnp.sum((tile_x * EXPERT_TM)[:, None] >= pends[None, :], axis=1),
                                 N_EXPERTS - 1).astype(jnp.int32)
            pad_info = jnp.concatenate([pstarts + counts, padded - counts,
                                        (pends[-1:] // EXPERT_TM)]).astype(jnp.int32)
            buf = _scatter_rows(dest, pad_info, t, tm=lat_tm, n_tiles=n_tiles)
            ybuf = _experts(tile_e, tile_v, tile_x, buf, *moe_up_b, moe_wd_b, tf=512)
            out = _combine(dest, x2, mod3, lat_row_for(256), gates, row2(g_final), ybuf, tm=256)
    return out.reshape(b, s, d)
```

```python
import functools

import jax
import jax.numpy as jnp
from jax import lax
from jax.experimental import pallas as pl
from jax.experimental.pallas import tpu as pltpu

F32 = jnp.float32
BF16 = jnp.bfloat16

GRID_W = 64
HEAD_DIM = 128
N_Q_HEADS = 8
N_KV_HEADS = 2
GQA_GROUP = N_Q_HEADS // N_KV_HEADS
ATTN_W = N_Q_HEADS * HEAD_DIM
KV_W = N_KV_HEADS * HEAD_DIM
ROPE_THETA = 10000.0
SCONV_W = 512
SCONV_K = 3
CONF_W = 512
CONF_K = 31
N_EXPERTS = 8
EPS = 1e-6
LOG2E = 1.4426950408889634

V7X_VMEM_LIMIT_BYTES = 56 * 1024 * 1024
MOD_ROWS = 16
IN_TN = 512
EXPERT_TM = 512
ROW_TM = 512
FFN_TF = 512
ATTN_TQ = 512
COMBINE_TM = 512
ADA_TN = 1536


def _cparams(n_axes):
    return pltpu.CompilerParams(dimension_semantics=("arbitrary",) * n_axes,
                                vmem_limit_bytes=V7X_VMEM_LIMIT_BYTES)


def _sigmoid(x):
    return 1.0 / (1.0 + jnp.exp(-x))


NORM_CHUNK = 16
NORM_UNROLL = 4


def _norm_modulate_rows(x_ref, h_ref, g, shift, scale):
    rows = x_ref.shape[0]
    gain = g * (1.0 + scale)

    def body(c, carry):
        r = pl.multiple_of(c * NORM_CHUNK, NORM_CHUNK)
        x = x_ref[pl.ds(r, NORM_CHUNK), :]
        inv = lax.rsqrt(jnp.mean(x * x, axis=-1, keepdims=True) + EPS)
        h_ref[pl.ds(r, NORM_CHUNK), :] = (x * inv * gain + shift).astype(h_ref.dtype)
        return carry

    lax.fori_loop(0, rows // NORM_CHUNK, body, 0, unroll=NORM_UNROLL)


def _ada_kernel(c_ref, w_ref, b_ref, o_ref):
    a = c_ref[...]
    a = (a * _sigmoid(a)).astype(BF16)
    o_ref[0] = jnp.dot(a, w_ref[0].astype(BF16), preferred_element_type=F32) + b_ref[0]


def _ada(cin, w_ada, b_ada):
    depth, d, n = w_ada.shape
    tn = ADA_TN
    return pl.pallas_call(
        _ada_kernel,
        grid=(depth, n // tn),
        in_specs=[pl.BlockSpec((MOD_ROWS, d), lambda l, j: (0, 0)),
                  pl.BlockSpec((1, d, tn), lambda l, j: (l, 0, j)),
                  pl.BlockSpec((1, 1, tn), lambda l, j: (l, 0, j))],
        out_specs=pl.BlockSpec((1, MOD_ROWS, tn), lambda l, j: (l, 0, j)),
        out_shape=jax.ShapeDtypeStruct((depth, MOD_ROWS, n), F32),
        compiler_params=_cparams(2),
        name="ada",
    )(cin, w_ada, b_ada.reshape(depth, 1, n))


def _head_norm(seg, gain):
    ms = jnp.mean(seg * seg, axis=-1, keepdims=True)
    return seg * lax.rsqrt(ms + EPS) * gain


def _rope(n, cos_t, sin_t):
    lane = lax.broadcasted_iota(jnp.int32, n.shape, 1)
    fwd = pltpu.roll(n, 32, 1)
    bwd = pltpu.roll(n, 96, 1)
    partner = jnp.where((lane // 32) % 2 == 0, bwd, fwd)
    return n * cos_t + partner * sin_t


def _in_kernel(x_ref, shift_ref, scale_ref, g_ref, w_ref, gq_ref, gk_ref, cos_ref, sin_ref,
               *rest, rope, j0, nj, q_scale, n_cast):
    cast_in, o_ref, cast_out, h_ref = rest[:n_cast], rest[n_cast], rest[n_cast + 1:2 * n_cast + 1], rest[-1]
    _norm_modulate_rows(x_ref, h_ref, g_ref[...], shift_ref[0], scale_ref[0])
    _cast_slabs(cast_in, cast_out)
    h = h_ref[...]

    def head(seg, gain, scale):
        n = _head_norm(seg, gain)
        if rope:
            n = _rope(n, cos_ref[...], sin_ref[...])
        if scale != 1.0:
            n = n * scale
        return n

    for t in range(nj):
        j = t + j0
        c0 = t * IN_TN
        acc = jnp.dot(h, w_ref[:, c0:c0 + IN_TN], preferred_element_type=F32)
        if j < 2:
            for hd in range(IN_TN // HEAD_DIM):
                sl = slice(hd * HEAD_DIM, (hd + 1) * HEAD_DIM)
                osl = slice(c0 + hd * HEAD_DIM, c0 + (hd + 1) * HEAD_DIM)
                o_ref[:, osl] = head(acc[:, sl], gq_ref[...], q_scale).astype(o_ref.dtype)
        elif j == 2:
            for hd in range(N_KV_HEADS):
                sl = slice(hd * HEAD_DIM, (hd + 1) * HEAD_DIM)
                osl = slice(c0 + hd * HEAD_DIM, c0 + (hd + 1) * HEAD_DIM)
                o_ref[:, osl] = head(acc[:, sl], gk_ref[...], 1.0).astype(o_ref.dtype)
            o_ref[:, c0 + KV_W:c0 + IN_TN] = acc[:, KV_W:].astype(o_ref.dtype)
        else:
            o_ref[:, c0:c0 + IN_TN] = acc.astype(o_ref.dtype)


def _const_spec(shape, idx):
    return pl.BlockSpec(shape, lambda i: idx, pipeline_mode=pl.Buffered(1))


def _in_proj(x2, mod3, mod_row, g, w, gq, gk, cos_t, sin_t, *, tm, rope, j0, nj, seq, cast=()):
    m, d = x2.shape
    tiles_per_seq = seq // tm
    ncols = nj * IN_TN
    assert j0 % nj == 0
    q_scale = HEAD_DIM ** -0.5 * LOG2E
    kern = functools.partial(_in_kernel, rope=rope, j0=j0, nj=nj, q_scale=q_scale, n_cast=len(cast))
    cast_arrays, cast_in, cast_out, cast_shapes = _cast_plan(cast, m // tm, lambda i: i)
    outs = pl.pallas_call(
        kern,
        grid=(m // tm,),
        in_specs=[pl.BlockSpec((tm, d), lambda i: (i, 0)),
                  pl.BlockSpec((1, 1, d), lambda i: (mod_row(i), 0, 0)),
                  pl.BlockSpec((1, 1, d), lambda i: (mod_row(i), 0, 1)),
                  _const_spec((1, d), (0, 0)),
                  _const_spec((d, ncols), (0, j0 // nj)),
                  _const_spec((1, HEAD_DIM), (0, 0)),
                  _const_spec((1, HEAD_DIM), (0, 0)),
                  pl.BlockSpec((tm, HEAD_DIM), lambda i: (i % tiles_per_seq, 0)),
                  pl.BlockSpec((tm, HEAD_DIM), lambda i: (i % tiles_per_seq, 0))] + cast_in,
        out_specs=[pl.BlockSpec((tm, ncols), lambda i: (i, 0))] + cast_out,
        out_shape=[jax.ShapeDtypeStruct((m, ncols), BF16)] + cast_shapes,
        scratch_shapes=[pltpu.VMEM((tm, d), BF16)],
        compiler_params=_cparams(1),
        name="in_proj",
    )(x2, mod3, mod3, g, w, gq, gk, cos_t, sin_t, *cast_arrays)
    return outs[0], tuple(outs[1:])


def _attn_kernel(q_ref, kc_ref, vc_ref, kl_ref, vl_ref, gg_ref, *rest, has_latent, n_cast):
    cast_in, o_ref, cast_out, acc_ref = rest[:n_cast], rest[n_cast], rest[n_cast + 1:2 * n_cast + 1], rest[-1]
    _cast_slabs(cast_in, cast_out)
    tq = q_ref.shape[0]
    dn = (((1,), (1,)), ((), ()))
    ssq = jnp.zeros((tq, 1), F32)
    for hd in range(N_Q_HEADS):
        kv = hd // GQA_GROUP
        ksl = slice(kv * HEAD_DIM, (kv + 1) * HEAD_DIM)
        q = q_ref[:, hd * HEAD_DIM:(hd + 1) * HEAD_DIM]
        s_c = lax.dot_general(q, kc_ref[:, ksl], dn, preferred_element_type=F32)
        mx = jnp.max(s_c, axis=-1, keepdims=True)
        if has_latent:
            s_l = lax.dot_general(q, kl_ref[:, ksl], dn, preferred_element_type=F32)
            mx = jnp.maximum(mx, jnp.max(s_l, axis=-1, keepdims=True))
        p_c = jnp.exp2(s_c - mx)
        den = jnp.sum(p_c, axis=-1, keepdims=True)
        o = jnp.dot(p_c.astype(BF16), vc_ref[:, ksl], preferred_element_type=F32)
        if has_latent:
            p_l = jnp.exp2(s_l - mx)
            den = den + jnp.sum(p_l, axis=-1, keepdims=True)
            o = o + jnp.dot(p_l.astype(BF16), vl_ref[:, ksl], preferred_element_type=F32)
        o = o * (1.0 / den)
        ssq = ssq + jnp.sum(o * o, axis=-1, keepdims=True)
        acc_ref[:, hd * HEAD_DIM:(hd + 1) * HEAD_DIM] = o
    inv = lax.rsqrt(ssq * (1.0 / ATTN_W) + EPS)
    o_ref[...] = (acc_ref[...] * inv * gg_ref[...]).astype(o_ref.dtype)


def _attention(qsrc, csrc, lsrc, gg, *, n_batch, q_len, tq, kc_blk, has_latent, n_ctx, seq, cast=()):
    m = qsrc.shape[0]
    tiles = q_len // tq
    kern = functools.partial(_attn_kernel, has_latent=has_latent, n_cast=len(cast))
    cast_arrays, cast_in, cast_out, cast_shapes = _cast_plan(cast, n_batch * tiles, lambda b, i: b * tiles + i)
    outs = pl.pallas_call(
        kern,
        grid=(n_batch, tiles),
        in_specs=[pl.BlockSpec((tq, ATTN_W), lambda b, i: (b * tiles + i, 0)),
                  pl.BlockSpec((n_ctx, KV_W), lambda b, i: (b, kc_blk)),
                  pl.BlockSpec((n_ctx, KV_W), lambda b, i: (b, kc_blk + 1)),
                  pl.BlockSpec((seq, KV_W), lambda b, i: (b, ATTN_W // KV_W)),
                  pl.BlockSpec((seq, KV_W), lambda b, i: (b, ATTN_W // KV_W + 1)),
                  pl.BlockSpec((1, ATTN_W), lambda b, i: (0, 0))] + cast_in,
        out_specs=[pl.BlockSpec((tq, ATTN_W), lambda b, i: (b * tiles + i, 0))] + cast_out,
        out_shape=[jax.ShapeDtypeStruct((m, ATTN_W), BF16)] + cast_shapes,
        scratch_shapes=[pltpu.VMEM((tq, ATTN_W), F32)],
        compiler_params=_cparams(2),
        name="attention",
    )(qsrc, csrc, csrc, lsrc, lsrc, gg, *cast_arrays)
    return outs[0], tuple(outs[1:])


MIX_CHUNK = 128
SCONV_PAD = 8
CONF_PAD = 16


LANES = 128
SUBLANES = 8
N_LANE_GROUPS = SCONV_W // LANES


def _window_conv(p_ref, slot, w_ref, group, first, n_taps, ch):
    lanes = slice(group * LANES, (group + 1) * LANES)
    win = p_ref[slot, :, lanes]
    n_rows = win.shape[0]
    acc = None
    for phase in range(SUBLANES):
        taps = [k for k in range(n_taps) if (first + k) % SUBLANES == phase]
        if not taps:
            continue
        shifted = win if phase == 0 else pltpu.roll(win, n_rows - phase, 0)
        for k in taps:
            base = first + k - phase
            term = w_ref[k:k + 1, lanes] * shifted[base:base + ch]
            acc = term if acc is None else acc + term
    return acc


def _mix_kernel(sb_ref, sg_ref, sx_ref, ga_ref, gb_ref, sw_ref, dw_ref, db_ref, lng_ref, lnb_ref,
                pw_ref, pb_ref, ggs_ref, ggc_ref, o_ref, p1_ref, p2_ref, a_ref):
    seq = sb_ref.shape[0]
    ch = MIX_CHUNK
    n_chunks = seq // ch
    hs, hc = SCONV_PAD, CONF_PAD
    p1_ref[1, 0:hs, :] = jnp.zeros((hs, SCONV_W), F32)
    p1_ref[n_chunks, ch + hs:ch + 2 * hs, :] = jnp.zeros((hs, SCONV_W), F32)
    p2_ref[1, 0:hc, :] = jnp.zeros((hc, CONF_W), F32)
    p2_ref[n_chunks, ch + hc:ch + 2 * hc, :] = jnp.zeros((hc, CONF_W), F32)

    def fill(c, carry):
        r = pl.multiple_of(c * ch, ch)
        rows = pl.ds(r, ch)
        v = sg_ref[rows, :].astype(F32) * sx_ref[rows, :].astype(F32)
        p1_ref[c + 1, hs:hs + ch, :] = v
        p1_ref[c, ch + hs:ch + 2 * hs, :] = v[0:hs]
        p1_ref[c + 2, 0:hs, :] = v[ch - hs:ch]
        u = ga_ref[rows, :].astype(F32) * _sigmoid(gb_ref[rows, :].astype(F32))
        p2_ref[c + 1, hc:hc + ch, :] = u
        p2_ref[c, ch + hc:ch + 2 * hc, :] = u[0:hc]
        p2_ref[c + 2, 0:hc, :] = u[ch - hc:ch]
        return carry

    lax.fori_loop(0, n_chunks, fill, 0)

    def conv(c, carry):
        r = pl.multiple_of(c * ch, ch)
        rows = pl.ds(r, ch)
        acc = jnp.concatenate(
            [_window_conv(p1_ref, c + 1, sw_ref, g, hs - SCONV_K // 2, SCONV_K, ch) for g in range(N_LANE_GROUPS)],
            axis=1)
        ys = sb_ref[rows, :].astype(F32) * acc
        ms = jnp.mean(ys * ys, axis=-1, keepdims=True)
        o_ref[rows, 0:SCONV_W] = (ys * lax.rsqrt(ms + EPS) * ggs_ref[...]).astype(o_ref.dtype)

        u = db_ref[...] + jnp.concatenate(
            [_window_conv(p2_ref, c + 1, dw_ref, g, hc - CONF_K // 2, CONF_K, ch) for g in range(N_LANE_GROUPS)],
            axis=1)
        mu = jnp.mean(u, axis=-1, keepdims=True)
        uc = u - mu
        var = jnp.mean(uc * uc, axis=-1, keepdims=True)
        v = uc * lax.rsqrt(var + EPS) * lng_ref[...] + lnb_ref[...]
        a_ref[rows, :] = (v * _sigmoid(v)).astype(a_ref.dtype)
        return carry

    lax.fori_loop(0, n_chunks, conv, 0)

    yc = jnp.dot(a_ref[...], pw_ref[...], preferred_element_type=F32) + pb_ref[...]
    ms = jnp.mean(yc * yc, axis=-1, keepdims=True)
    o_ref[:, SCONV_W:] = (yc * lax.rsqrt(ms + EPS) * ggc_ref[...]).astype(o_ref.dtype)


def _local_mixers(src, sconv_w, conf_dw, conf_db, ln_g, ln_b, pw, pb, ggs, ggc, *, n_seq, seq):
    w = SCONV_W
    col = lambda k: pl.BlockSpec((seq, w), lambda b: (b, k))
    vec = lambda n: pl.BlockSpec((1, n), lambda b: (0, 0))
    return pl.pallas_call(
        _mix_kernel,
        grid=(n_seq,),
        in_specs=[col(3), col(4), col(5), col(6), col(7),
                  pl.BlockSpec((SCONV_K, w), lambda b: (0, 0)),
                  pl.BlockSpec((CONF_K, w), lambda b: (0, 0)),
                  vec(w), vec(w), vec(w),
                  pl.BlockSpec((w, w), lambda b: (0, 0)),
                  vec(w), vec(w), vec(w)],
        out_specs=pl.BlockSpec((seq, 2 * w), lambda b: (b, 0)),
        out_shape=jax.ShapeDtypeStruct((src.shape[0], 2 * w), BF16),
        scratch_shapes=[pltpu.VMEM((seq // MIX_CHUNK + 2, MIX_CHUNK + 2 * SCONV_PAD, w), F32),
                        pltpu.VMEM((seq // MIX_CHUNK + 2, MIX_CHUNK + 2 * CONF_PAD, w), F32),
                        pltpu.VMEM((seq, w), BF16)],
        compiler_params=_cparams(1),
        name="local_mixers",
    )(src, src, src, src, src, sconv_w, conf_dw, conf_db, ln_g, ln_b, pw, pb, ggs, ggc)


OUT_TN = 512


def _out_kernel(a_ref, m_ref, wa_ref, wm_ref, x_ref, gate_ref, o_ref):
    a = a_ref[...]
    mx = m_ref[...]
    for t in range(o_ref.shape[1] // OUT_TN):
        sl = slice(t * OUT_TN, (t + 1) * OUT_TN)
        acc = (jnp.dot(a, wa_ref[:, sl], preferred_element_type=F32)
               + jnp.dot(mx, wm_ref[:, sl], preferred_element_type=F32))
        o_ref[:, sl] = x_ref[:, sl] + gate_ref[0, :, sl] * acc


def _out_proj(att, mix, w, x2, mod3, mod_row, *, tm):
    m, d = x2.shape
    half = att.shape[1]
    return pl.pallas_call(
        _out_kernel,
        grid=(m // tm,),
        in_specs=[pl.BlockSpec((tm, half), lambda i: (i, 0)),
                  pl.BlockSpec((tm, half), lambda i: (i, 0)),
                  _const_spec((half, d), (0, 0)),
                  _const_spec((half, d), (1, 0)),
                  pl.BlockSpec((tm, d), lambda i: (i, 0)),
                  pl.BlockSpec((1, 1, d), lambda i: (mod_row(i), 0, 2))],
        out_specs=pl.BlockSpec((tm, d), lambda i: (i, 0)),
        out_shape=jax.ShapeDtypeStruct((m, d), F32),
        compiler_params=_cparams(1),
        name="out_proj",
    )(att, mix, w, w, x2, mod3)


MXU_WIDTH = 256


def _swiglu_accumulate(h, wg_ref, wu_ref, wd_refs, o_ref, assign=False):
    tm, tf = h.shape[0], wg_ref.shape[1]
    if tf % (2 * MXU_WIDTH) == 0:
        splits = [(slice(0, tm), slice(p * tf // 2, (p + 1) * tf // 2)) for p in range(2)]
    else:
        splits = [(slice(p * tm // 2, (p + 1) * tm // 2), slice(0, tf)) for p in range(2)]
    ups = [(jnp.dot(h[rows], wg_ref[:, cols], preferred_element_type=F32),
            jnp.dot(h[rows], wu_ref[:, cols], preferred_element_type=F32)) for rows, cols in splits]
    acts = [(gv * _sigmoid(gv) * uv).astype(BF16) for gv, uv in ups]
    same_rows = splits[0][0] == splits[1][0]
    col = 0
    for wd_ref in wd_refs:
        width = wd_ref.shape[1]
        downs = [jnp.dot(act, wd_ref[cols, :], preferred_element_type=F32) for act, (_, cols) in zip(acts, splits)]
        pieces = [(splits[0][0], downs[0] + downs[1])] if same_rows else list(zip((s[0] for s in splits), downs))
        for rows, part in pieces:
            if assign:
                o_ref[rows, col:col + width] = part
            else:
                o_ref[rows, col:col + width] += part
        col += width


def _ffn_kernel(x_ref, shift_ref, scale_ref, gate_ref, g_ref, wg_ref, wu_ref, wd_ref, *rest, n_cast):
    cast_in, o_ref, cast_out, h_ref = rest[:n_cast], rest[n_cast], rest[n_cast + 1:2 * n_cast + 1], rest[-1]
    j = pl.program_id(1)

    _cast_slabs(cast_in, cast_out)

    @pl.when(j == 0)
    def _():
        _norm_modulate_rows(x_ref, h_ref, g_ref[...], shift_ref[0], scale_ref[0])
        o_ref[...] = jnp.zeros(o_ref.shape, o_ref.dtype)

    _swiglu_accumulate(h_ref[...], wg_ref, wu_ref, (wd_ref,), o_ref)

    @pl.when(j == pl.num_programs(1) - 1)
    def _():
        o_ref[...] = x_ref[...] + gate_ref[0] * o_ref[...]


BF16_SUBLANE_TILE = 16


def _slab_rows(rows, n_lead, n_steps):
    for height in range(BF16_SUBLANE_TILE, rows + 1, BF16_SUBLANE_TILE):
        if rows % height == 0 and n_lead * (rows // height) <= n_steps:
            return height
    raise ValueError(f"no slab height for rows={rows}, n_lead={n_lead}, n_steps={n_steps}")


def _cast_plan(jobs, n_steps, step_of):
    arrays, in_specs, out_specs, out_shapes = [], [], [], []
    for arr, col_block, n_col_blocks in jobs:
        n_lead, rows, cols = arr.shape
        width = cols // n_col_blocks
        height = _slab_rows(rows, n_lead, n_steps)
        per_lead = rows // height
        last = n_lead * per_lead - 1

        def index(*grid_idx, per_lead=per_lead, last=last, col=0):
            s = jnp.minimum(step_of(*grid_idx), last)
            return (s // per_lead, s % per_lead, col)

        arrays.append(arr)
        in_specs.append(pl.BlockSpec((1, height, width), functools.partial(index, col=col_block)))
        out_specs.append(pl.BlockSpec((1, height, width), index))
        out_shapes.append(jax.ShapeDtypeStruct((n_lead, rows, width), BF16))
    return arrays, in_specs, out_specs, out_shapes


def _cast_slabs(cast_in, cast_out):
    for src, dst in zip(cast_in, cast_out):
        dst[...] = src[...].astype(dst.dtype)


def _dense_ffn(x2, mod3, mod_row, g, wg, wu, wd, *, tm, tf, cast=()):
    m, d = x2.shape
    f = wg.shape[2]
    nf = f // tf
    n_steps = (m // tm) * nf
    modspec = lambda k: pl.BlockSpec((1, 1, d), lambda i, j: (mod_row(i), 0, k))
    cast_arrays, cast_in, cast_out, cast_shapes = _cast_plan(cast, n_steps, lambda i, j: i * nf + j)
    outs = pl.pallas_call(
        functools.partial(_ffn_kernel, n_cast=len(cast)),
        grid=(m // tm, nf),
        in_specs=[pl.BlockSpec((tm, d), lambda i, j: (i, 0)),
                  modspec(3), modspec(4), modspec(5),
                  pl.BlockSpec((1, d), lambda i, j: (0, 0)),
                  pl.BlockSpec((None, d, tf), lambda i, j: (0, 0, j)),
                  pl.BlockSpec((None, d, tf), lambda i, j: (0, 0, j)),
                  pl.BlockSpec((None, tf, d), lambda i, j: (0, j, 0))] + cast_in,
        out_specs=[pl.BlockSpec((tm, d), lambda i, j: (i, 0))] + cast_out,
        out_shape=[jax.ShapeDtypeStruct((m, d), F32)] + cast_shapes,
        scratch_shapes=[pltpu.VMEM((tm, d), BF16)],
        compiler_params=_cparams(2),
        name="dense_ffn",
    )(x2, mod3, mod3, mod3, g, wg, wu, wd, *cast_arrays)
    return outs[0], tuple(outs[1:])


ROUTER_ROWS = 16


U32 = jnp.uint32


def _pack_bf16_pair(lo, hi):
    return (pltpu.bitcast(lo, U32) >> 16) | pltpu.bitcast(hi, U32)


def _unpack_bf16_pair(packed):
    lo = pltpu.bitcast(packed << 16, F32)
    hi = pltpu.bitcast(packed & jnp.uint32(0xFFFF0000), F32)
    return lo.astype(BF16), hi.astype(BF16)


def _router_kernel(x_ref, shift_ref, scale_ref, g_ref, rw_ref, tp_ref, idx_ref, gate_ref, cnt_ref,
                   t_ref, tri_ref, base_ref):
    tm = x_ref.shape[0]

    @pl.when(pl.program_id(0) == 0)
    def _():
        r = lax.broadcasted_iota(jnp.int32, (tm, tm), 0)
        c = lax.broadcasted_iota(jnp.int32, (tm, tm), 1)
        tri_ref[...] = jnp.where(r < c, 1.0, 0.0).astype(BF16)
        base_ref[...] = jnp.zeros(base_ref.shape, F32)

    _norm_modulate_rows(x_ref, t_ref, g_ref[...], shift_ref[0], scale_ref[0])

    t = t_ref[...]
    t_hi = t.astype(BF16)
    t_hi32 = t_hi.astype(F32)
    t_lo = (t - t_hi32).astype(BF16)
    half = t.shape[1] // 2
    tp_ref[...] = _pack_bf16_pair(t_hi32[:, :half], t_hi32[:, half:])
    w = rw_ref[...]
    w_hi = w.astype(BF16)
    w_lo = (w - w_hi.astype(F32)).astype(BF16)
    dn = (((1,), (1,)), ((), ()))
    logits = (lax.dot_general(w_hi, t_hi, dn, preferred_element_type=F32)
              + lax.dot_general(w_lo, t_hi, dn, preferred_element_type=F32)
              + lax.dot_general(w_hi, t_lo, dn, preferred_element_type=F32))

    e = lax.broadcasted_iota(jnp.int32, (ROUTER_ROWS, tm), 0).astype(F32)
    neg = jnp.float32(-jnp.inf)
    lg = jnp.where(e < N_EXPERTS, logits, neg)
    m1 = jnp.max(lg, axis=0, keepdims=True)
    i1 = jnp.min(jnp.where(lg == m1, e, float(ROUTER_ROWS)), axis=0, keepdims=True)
    lg2 = jnp.where(e == i1, neg, lg)
    m2 = jnp.max(lg2, axis=0, keepdims=True)
    i2 = jnp.min(jnp.where(lg2 == m2, e, float(ROUTER_ROWS)), axis=0, keepdims=True)
    ex = jnp.exp(m2 - m1)
    den = 1.0 + ex
    gate_ref[0:1, :] = 1.0 / den
    gate_ref[1:2, :] = ex / den

    hit1 = e == i1
    hit2 = e == i2
    onehot = jnp.where(hit1 | hit2, 1.0, 0.0)
    prefix = jnp.dot(onehot.astype(BF16), tri_ref[...], preferred_element_type=F32) + base_ref[:, 0:1]
    r1 = jnp.sum(jnp.where(hit1, prefix, 0.0), axis=0, keepdims=True)
    r2 = jnp.sum(jnp.where(hit2, prefix, 0.0), axis=0, keepdims=True)
    idx_ref[0:1, :] = i1.astype(jnp.int32)
    idx_ref[1:2, :] = i2.astype(jnp.int32)
    idx_ref[2:3, :] = r1.astype(jnp.int32)
    idx_ref[3:4, :] = r2.astype(jnp.int32)
    base_ref[...] = base_ref[...] + jnp.sum(onehot, axis=1, keepdims=True)
    cnt_ref[...] = base_ref[...].astype(jnp.int32)


def _router(x2, mod3, mod_row, g, rw16, *, tm):
    m, d = x2.shape
    modspec = lambda k: pl.BlockSpec((1, 1, d), lambda i: (mod_row(i), 0, k))
    return pl.pallas_call(
        _router_kernel,
        grid=(m // tm,),
        in_specs=[pl.BlockSpec((tm, d), lambda i: (i, 0)),
                  modspec(3), modspec(4),
                  pl.BlockSpec((1, d), lambda i: (0, 0)),
                  pl.BlockSpec((ROUTER_ROWS, d), lambda i: (0, 0))],
        out_specs=[pl.BlockSpec((tm, d // 2), lambda i: (i, 0)),
                   pl.BlockSpec((4, tm), lambda i: (0, i)),
                   pl.BlockSpec((2, tm), lambda i: (0, i)),
                   pl.BlockSpec((ROUTER_ROWS, 128), lambda i: (0, 0))],
        out_shape=[jax.ShapeDtypeStruct((m, d // 2), U32),
                   jax.ShapeDtypeStruct((4, m), jnp.int32),
                   jax.ShapeDtypeStruct((2, m), F32),
                   jax.ShapeDtypeStruct((ROUTER_ROWS, 128), jnp.int32)],
        scratch_shapes=[pltpu.VMEM((tm, d), F32), pltpu.VMEM((tm, tm), BF16),
                        pltpu.VMEM((ROUTER_ROWS, 128), F32)],
        compiler_params=_cparams(1),
        name="router",
    )(x2, mod3, mod3, g, rw16)


def _row_copy(src_ref, src_row, dst_ref, dst_row, sem):
    return pltpu.make_async_copy(src_ref.at[pl.ds(src_row, 1), :], dst_ref.at[pl.ds(dst_row, 1), :], sem)


DMA_ISSUE_UNROLL = 8


def _scatter_kernel(dest_ref, pad_ref, t_ref, buf_ref, zero_ref, sem, zsem, *, n_tok, n_tiles):
    tm = t_ref.shape[0]
    base = pl.program_id(0) * tm

    @pl.when(pl.program_id(0) == 0)
    def _():
        zero_ref[...] = jnp.zeros(zero_ref.shape, zero_ref.dtype)
        for e in range(N_EXPERTS):
            start, length = pad_ref[e], pad_ref[N_EXPERTS + e]

            def fill(r, carry, start=start):
                _row_copy(zero_ref, 0, buf_ref, start + r, zsem).start()
                return carry

            def drain(r, carry):
                _row_copy(zero_ref, 0, buf_ref, 0, zsem).wait()
                return carry

            lax.fori_loop(0, length, fill, 0)
            lax.fori_loop(0, length, drain, 0)

        def tile_copy(tile):
            row0 = pl.multiple_of(tile * EXPERT_TM, EXPERT_TM)
            return pltpu.make_async_copy(zero_ref, buf_ref.at[pl.ds(row0, EXPERT_TM), :], zsem)

        def fill_tile(tile, carry):
            tile_copy(tile).start()
            return carry

        def drain_tile(tile, carry):
            tile_copy(tile).wait()
            return carry

        lax.fori_loop(pad_ref[2 * N_EXPERTS], n_tiles, fill_tile, 0)
        lax.fori_loop(pad_ref[2 * N_EXPERTS], n_tiles, drain_tile, 0)

    def start(r, carry):
        for k in range(2):
            _row_copy(t_ref, r, buf_ref, dest_ref[k * n_tok + base + r], sem).start()
        return carry

    lax.fori_loop(0, tm, start, 0, unroll=DMA_ISSUE_UNROLL)
    for k in range(2):
        pltpu.make_async_copy(t_ref, buf_ref.at[pl.ds(0, tm), :], sem).wait()


def _scatter_rows(dest, pad_info, t, *, tm, n_tiles):
    m, d = t.shape
    grid_spec = pltpu.PrefetchScalarGridSpec(
        num_scalar_prefetch=2,
        grid=(m // tm,),
        in_specs=[pl.BlockSpec((tm, d), lambda i, dest, pad: (i, 0))],
        out_specs=pl.BlockSpec(memory_space=pl.ANY),
        scratch_shapes=[pltpu.VMEM((EXPERT_TM, d), t.dtype),
                        pltpu.SemaphoreType.DMA(()), pltpu.SemaphoreType.DMA(())],
    )
    return pl.pallas_call(
        functools.partial(_scatter_kernel, n_tok=m, n_tiles=n_tiles),
        grid_spec=grid_spec,
        out_shape=jax.ShapeDtypeStruct((n_tiles * EXPERT_TM, d), t.dtype),
        compiler_params=_cparams(1),
        name="moe_scatter",
    )(dest, pad_info, t)


N_WEIGHT_SLOTS = 3


def _expert_kernel(te_ref, tv_ref, tx_ref, x_ref, *rest, n_wd, tf, nf):
    del tx_ref
    w_hbm = rest[:2 + n_wd]
    o_ref, xb_ref = rest[2 + n_wd], rest[3 + n_wd]
    w_buf = rest[4 + n_wd:6 + 2 * n_wd]
    sem = rest[-1]
    i = pl.program_id(0)
    n = pl.num_programs(0)

    def copies(tile, j, slot):
        e = te_ref[tile]
        cols = pl.ds(j * tf, tf)
        srcs = [w_hbm[0].at[e, :, cols], w_hbm[1].at[e, :, cols]] + [w.at[e, cols, :] for w in w_hbm[2:]]
        return [pltpu.make_async_copy(src, buf.at[slot], sem.at[slot, k])
                for k, (src, buf) in enumerate(zip(srcs, w_buf))]

    valid = tv_ref[i] == 1
    nxt = jnp.minimum(i + 1, n - 1)
    next_live = jnp.logical_and(i + 1 < n, tv_ref[nxt] == 1)

    @pl.when(jnp.logical_not(valid))
    def _():
        o_ref[...] = jnp.zeros(o_ref.shape, o_ref.dtype)

    @pl.when(jnp.logical_and(valid, i == 0))
    def _():
        for c in copies(i, 0, 0):
            c.start()

    @pl.when(valid)
    def _():
        half = x_ref.shape[1]
        xb_ref[:, :half], xb_ref[:, half:] = _unpack_bf16_pair(x_ref[...])
        xb = xb_ref[...]
        for j in range(nf):
            slot = j % N_WEIGHT_SLOTS
            ahead = copies(i, j + 1, (j + 1) % N_WEIGHT_SLOTS) if j + 1 < nf else copies(nxt, 0, 0)
            for c in ahead:
                c.start()
            for c in copies(i, j, slot):
                c.wait()
            _swiglu_accumulate(xb, w_buf[0].at[slot], w_buf[1].at[slot], [b.at[slot] for b in w_buf[2:]],
                               o_ref, assign=j == 0)

    @pl.when(jnp.logical_and(valid, jnp.logical_not(next_live)))
    def _():
        for c in copies(nxt, 0, 0):
            c.wait()


def _experts(tile_e, tile_v, tile_x, buf, wg, wu, wds, *, tf):
    rows, d_packed = buf.shape
    d = wg.shape[1]
    f = wg.shape[2]
    nf = f // tf
    assert nf % N_WEIGHT_SLOTS != 1, "slot 0 must be free while the last hidden tile computes"
    any_spec = pl.BlockSpec(memory_space=pl.ANY)
    grid_spec = pltpu.PrefetchScalarGridSpec(
        num_scalar_prefetch=3,
        grid=(rows // EXPERT_TM,),
        in_specs=[pl.BlockSpec((EXPERT_TM, d_packed), lambda i, te, tv, tx: (tx[i], 0))]
                 + [any_spec] * (2 + len(wds)),
        out_specs=pl.BlockSpec((EXPERT_TM, d), lambda i, te, tv, tx: (i, 0)),
        scratch_shapes=[pltpu.VMEM((EXPERT_TM, d), BF16),
                        pltpu.VMEM((N_WEIGHT_SLOTS, d, tf), BF16),
                        pltpu.VMEM((N_WEIGHT_SLOTS, d, tf), BF16)]
                       + [pltpu.VMEM((N_WEIGHT_SLOTS, tf, wd.shape[2]), BF16) for wd in wds]
                       + [pltpu.SemaphoreType.DMA((N_WEIGHT_SLOTS, 2 + len(wds)))],
    )
    return pl.pallas_call(
        functools.partial(_expert_kernel, n_wd=len(wds), tf=tf, nf=nf),
        grid_spec=grid_spec,
        out_shape=jax.ShapeDtypeStruct((rows, d), F32),
        compiler_params=_cparams(1),
        name="moe_experts",
    )(tile_e, tile_v, tile_x, buf, wg, wu, *wds)


def _combine_kernel(dest_ref, x_ref, gate5_ref, gates_ref, gf_ref, ybuf_ref, o_ref, rows_ref, sem, *, n_tok):
    tm = x_ref.shape[0]
    i = pl.program_id(0)

    def issue(tile, slot):
        base = tile * tm

        def start(r, carry):
            for k in range(2):
                _row_copy(ybuf_ref, dest_ref[k * n_tok + base + r], rows_ref.at[slot, k], r, sem.at[slot]).start()
            return carry

        lax.fori_loop(0, tm, start, 0, unroll=DMA_ISSUE_UNROLL)

    @pl.when(i == 0)
    def _():
        issue(0, 0)

    @pl.when(i + 1 < pl.num_programs(0))
    def _():
        issue(i + 1, (i + 1) % 2)

    slot = i % 2
    for k in range(2):
        pltpu.make_async_copy(ybuf_ref.at[pl.ds(0, tm), :], rows_ref.at[slot, k], sem.at[slot]).wait()

    eye = (lax.broadcasted_iota(jnp.int32, (tm, tm), 0) == lax.broadcasted_iota(jnp.int32, (tm, tm), 1))
    g0 = jnp.sum(jnp.where(eye, gates_ref[0:1, :], 0.0), axis=1, keepdims=True)
    g1 = jnp.sum(jnp.where(eye, gates_ref[1:2, :], 0.0), axis=1, keepdims=True)
    y = g0 * rows_ref[slot, 0] + g1 * rows_ref[slot, 1]
    xn = x_ref[...] + gate5_ref[0] * y
    ms = jnp.mean(xn * xn, axis=-1, keepdims=True)
    o_ref[...] = xn * lax.rsqrt(ms + EPS) * gf_ref[...]


def _combine(dest, x2, mod3, mod_row, gates, g_final, ybuf, *, tm):
    m, d = x2.shape
    grid_spec = pltpu.PrefetchScalarGridSpec(
        num_scalar_prefetch=1,
        grid=(m // tm,),
        in_specs=[pl.BlockSpec((tm, d), lambda i, dest: (i, 0)),
                  pl.BlockSpec((1, 1, d), lambda i, dest: (mod_row(i), 0, 5)),
                  pl.BlockSpec((2, tm), lambda i, dest: (0, i)),
                  pl.BlockSpec((1, d), lambda i, dest: (0, 0)),
                  pl.BlockSpec(memory_space=pl.ANY)],
        out_specs=pl.BlockSpec((tm, d), lambda i, dest: (i, 0)),
        scratch_shapes=[pltpu.VMEM((2, 2, tm, d), F32), pltpu.SemaphoreType.DMA((2,))],
    )
    return pl.pallas_call(
        functools.partial(_combine_kernel, n_tok=m),
        grid_spec=grid_spec,
        out_shape=jax.ShapeDtypeStruct((m, d), F32),
        compiler_params=_cparams(1),
        name="moe_combine",
    )(dest, x2, mod3, gates, g_final, ybuf)


def _rope_tables(seq):
    rows = seq // GRID_W
    row = jnp.repeat(jnp.arange(rows), GRID_W).astype(F32)
    col = jnp.tile(jnp.arange(GRID_W), rows).astype(F32)
    axis_dim = HEAD_DIM // 2
    inv_freq = ROPE_THETA ** (-jnp.arange(0, axis_dim, 2, dtype=F32) / axis_dim)
    ang_r = row[:, None] * inv_freq
    ang_c = col[:, None] * inv_freq
    cr, sr, cc, sc = jnp.cos(ang_r), jnp.sin(ang_r), jnp.cos(ang_c), jnp.sin(ang_c)
    return (jnp.concatenate([cr, cr, cc, cc], axis=-1),
            jnp.concatenate([-sr, sr, -sc, sc], axis=-1))


def kernel(x, c, ctx, c_ctx, w_ada, b_ada, g_mix, w_in, g_q, g_k, sconv_w, conf_dw, conf_db, conf_ln_g,
           conf_ln_b, conf_pw, conf_pb, g_group, w_o, g_ffn, dense_wg, dense_wu, dense_wd, router_w,
           moe_wg, moe_wu, moe_wd, g_final):
    b, s, d = x.shape
    n_ctx = ctx.shape[1]
    depth = w_ada.shape[0]
    assert b + 1 <= MOD_ROWS and depth == 2
    m_lat, m_ctx = b * s, b * n_ctx
    ctx_row = b

    cin = jnp.concatenate([c, c_ctx[None, :], jnp.zeros((MOD_ROWS - b - 1, d), F32)], axis=0)
    mod = _ada(cin, w_ada, b_ada)
    cos_t, sin_t = _rope_tables(s)
    row2 = lambda v: v.reshape(1, -1)

    lat_tm = ROW_TM
    lat_row_for = lambda tm: (lambda i: i // (s // tm))
    lat_row = lat_row_for(lat_tm)
    ctx_tm = ROW_TM
    ctx_mod_row = lambda i: ctx_row

    x2 = x.reshape(m_lat, d)
    xc2 = ctx.reshape(m_ctx, d)
    out = None
    moe_wd_b = ()
    for l in range(depth):
        last = l == depth - 1
        mod3 = mod[l].reshape(MOD_ROWS, 1, 6 * d)
        w_in_b = w_in[l].astype(BF16)
        w_o_b = w_o[l].astype(BF16)
        pw_b = conf_pw[l].astype(BF16)
        gg = g_group[l]
        gga, ggs, ggc = row2(gg[:ATTN_W]), row2(gg[ATTN_W:ATTN_W + SCONV_W]), row2(gg[ATTN_W + SCONV_W:])
        mixer_args = (sconv_w[l], conf_dw[l], row2(conf_db[l]), row2(conf_ln_g[l]), row2(conf_ln_b[l]),
                      pw_b, row2(conf_pb[l]), ggs, ggc)
        in_args = (row2(g_mix[l]), w_in_b, row2(g_q[l]), row2(g_k[l]), cos_t, sin_t)

        dense_here = l % 2 == 0
        k_dense = l // 2
        cast = (tuple((w[k_dense:k_dense + 1], 0, 1) for w in (dense_wg, dense_wu, dense_wd))
                if dense_here else ())
        k_moe = (l + 1) // 2 if dense_here else l // 2
        moe_wd_job = ((moe_wd[k_moe], l % 2, 2),) if k_moe < moe_wd.shape[0] else ()
        p_lat, dense_w_b = _in_proj(x2, mod3, lat_row, *in_args, tm=lat_tm, rope=True, j0=0, nj=8, seq=s,
                                    cast=cast)
        if last:
            p_ctx, _ = _in_proj(xc2, mod3, ctx_mod_row, *in_args, tm=ctx_tm, rope=False, j0=2, nj=1, seq=ctx_tm)
            kc_blk = 0
        else:
            p_ctx, _ = _in_proj(xc2, mod3, ctx_mod_row, *in_args, tm=ctx_tm, rope=False, j0=0, nj=8, seq=ctx_tm)
            kc_blk = ATTN_W // KV_W
        att, moe_wd_half = _attention(p_lat, p_ctx, p_lat, gga, n_batch=b, q_len=s, tq=ATTN_TQ, kc_blk=kc_blk,
                                      has_latent=True, n_ctx=n_ctx, seq=s, cast=moe_wd_job)
        moe_wd_b = moe_wd_b + moe_wd_half
        mix = _local_mixers(p_lat, *mixer_args, n_seq=b, seq=s)
        x2 = _out_proj(att, mix, w_o_b, x2, mod3, lat_row, tm=lat_tm)
        if not last:
            att_c, _ = _attention(p_ctx, p_ctx, p_ctx, gga, n_batch=b, q_len=n_ctx, tq=n_ctx, kc_blk=kc_blk,
                                  has_latent=False, n_ctx=n_ctx, seq=n_ctx)
            mix_c = _local_mixers(p_ctx, *mixer_args, n_seq=b, seq=n_ctx)
            xc2 = _out_proj(att_c, mix_c, w_o_b, xc2, mod3, ctx_mod_row, tm=ctx_tm)

        if l % 2 == 0:
            wg_b, wu_b, wd_b = dense_w_b
            ffn = functools.partial(_dense_ffn, g=row2(g_ffn[l]), wg=wg_b, wu=wu_b, wd=wd_b,
                                    tf=FFN_TF)
            routed_next = l + 1 < depth and (l + 1) % 2 == 1
            cast = ((moe_wg[(l + 1) // 2], 0, 1), (moe_wu[(l + 1) // 2], 0, 1)) if routed_next else ()
            x2, moe_up_b = ffn(x2, mod3, lat_row, tm=lat_tm, cast=cast)
            if not last:
                xc2, _ = ffn(xc2, mod3, ctx_mod_row, tm=ctx_tm)
        else:
            assert last, "routed layer is implemented for the final layer (latent tokens only)"
            rw16 = jnp.zeros((ROUTER_ROWS, d), F32).at[:N_EXPERTS].set(router_w[l // 2].T)
            t, idx, gates, cnt = _router(x2, mod3, lat_row, row2(g_ffn[l]), rw16, tm=lat_tm)
            counts = cnt[:N_EXPERTS, 0]
            padded = (counts + EXPERT_TM - 1) // EXPERT_TM * EXPERT_TM
            pends = jnp.cumsum(padded)
            pstarts = pends - padded
            slot_e = idx[0:2]
            slot_start = sum(jnp.where(slot_e == e, pstarts[e], 0) for e in range(N_EXPERTS))
            dest = (slot_start + idx[2:4]).reshape(-1).astype(jnp.int32)
            n_tiles = (2 * m_lat) // EXPERT_TM + N_EXPERTS
            tile_start = jnp.arange(n_tiles, dtype=jnp.int32) * EXPERT_TM
            tile_v = (tile_start < pends[-1]).astype(jnp.int32)
            last_tile = pends[-1] // EXPERT_TM - 1
            tile_x = jnp.minimum(jnp.arange(n_tiles, dtype=jnp.int32), last_tile).astype(jnp.int32)
            tile_e = jnp.minimum(jnp.sum((tile_x * EXPERT_TM)[:, None] >= pends[None, :], axis=1),
                                 N_EXPERTS - 1).astype(jnp.int32)
            pad_info = jnp.concatenate([pstarts + counts, padded - counts,
                                        (pends[-1:] // EXPERT_TM)]).astype(jnp.int32)
            buf = _scatter_rows(dest, pad_info, t, tm=lat_tm, n_tiles=n_tiles)
            ybuf = _experts(tile_e, tile_v, tile_x, buf, *moe_up_b, moe_wd_b, tf=FFN_TF)
            out = _combine(dest, x2, mod3, lat_row_for(COMBINE_TM), gates, row2(g_final), ybuf, tm=COMBINE_TM)
    return out.reshape(b, s, d)
```

```python
import functools

import jax
import jax.numpy as jnp
from jax import lax
from jax.experimental import pallas as pl
from jax.experimental.pallas import tpu as pltpu

F32 = jnp.float32
BF16 = jnp.bfloat16

GRID_W = 64
HEAD_DIM = 128
N_Q_HEADS = 8
N_KV_HEADS = 2
GQA_GROUP = N_Q_HEADS // N_KV_HEADS
ATTN_W = N_Q_HEADS * HEAD_DIM
KV_W = N_KV_HEADS * HEAD_DIM
ROPE_THETA = 10000.0
SCONV_W = 512
SCONV_K = 3
CONF_W = 512
CONF_K = 31
N_EXPERTS = 8
EPS = 1e-6
LOG2E = 1.4426950408889634

V7X_VMEM_LIMIT_BYTES = 56 * 1024 * 1024
MOD_ROWS = 16
IN_TN = 512
EXPERT_TM = 512
ROW_TM = 512
FFN_TF = 512
ATTN_TQ = 512
COMBINE_TM = 256
ADA_TN = 1536


def _cparams(n_axes):
    return pltpu.CompilerParams(dimension_semantics=("arbitrary",) * n_axes,
                                vmem_limit_bytes=V7X_VMEM_LIMIT_BYTES)


def _sigmoid(x):
    return 1.0 / (1.0 + jnp.exp(-x))


NORM_CHUNK = 16
NORM_UNROLL = 4


def _norm_modulate_rows(x_ref, h_ref, g, shift, scale, static=False):
    rows = x_ref.shape[0]
    gain = g * (1.0 + scale)

    def body(c, carry):
        r = c * NORM_CHUNK if static else pl.multiple_of(c * NORM_CHUNK, NORM_CHUNK)
        x = x_ref[pl.ds(r, NORM_CHUNK), :]
        inv = lax.rsqrt(jnp.mean(x * x, axis=-1, keepdims=True) + EPS)
        h_ref[pl.ds(r, NORM_CHUNK), :] = (x * inv * gain + shift).astype(h_ref.dtype)
        return carry

    if static:
        for c in range(rows // NORM_CHUNK):
            body(c, 0)
        return

    lax.fori_loop(0, rows // NORM_CHUNK, body, 0, unroll=NORM_UNROLL)


def _ada_kernel(c_ref, w_ref, b_ref, o_ref):
    a = c_ref[...]
    a = (a * _sigmoid(a)).astype(BF16)
    o_ref[0] = jnp.dot(a, w_ref[0].astype(BF16), preferred_element_type=F32) + b_ref[0]


def _ada(cin, w_ada, b_ada):
    depth, d, n = w_ada.shape
    tn = ADA_TN
    return pl.pallas_call(
        _ada_kernel,
        grid=(depth, n // tn),
        in_specs=[pl.BlockSpec((MOD_ROWS, d), lambda l, j: (0, 0)),
                  pl.BlockSpec((1, d, tn), lambda l, j: (l, 0, j)),
                  pl.BlockSpec((1, 1, tn), lambda l, j: (l, 0, j))],
        out_specs=pl.BlockSpec((1, MOD_ROWS, tn), lambda l, j: (l, 0, j)),
        out_shape=jax.ShapeDtypeStruct((depth, MOD_ROWS, n), F32),
        compiler_params=_cparams(2),
        name="ada",
    )(cin, w_ada, b_ada.reshape(depth, 1, n))


def _head_norm(seg, gain):
    ms = jnp.mean(seg * seg, axis=-1, keepdims=True)
    return seg * lax.rsqrt(ms + EPS) * gain


def _rope(n, cos_t, sin_t):
    lane = lax.broadcasted_iota(jnp.int32, n.shape, 1)
    fwd = pltpu.roll(n, 32, 1)
    bwd = pltpu.roll(n, 96, 1)
    partner = jnp.where((lane // 32) % 2 == 0, bwd, fwd)
    return n * cos_t + partner * sin_t


def _in_kernel(x_ref, shift_ref, scale_ref, g_ref, w_ref, gq_ref, gk_ref, cos_ref, sin_ref,
               *rest, rope, j0, nj, q_scale, n_cast):
    cast_in, o_ref, cast_out, h_ref = rest[:n_cast], rest[n_cast], rest[n_cast + 1:2 * n_cast + 1], rest[-1]
    _norm_modulate_rows(x_ref, h_ref, g_ref[...], shift_ref[0], scale_ref[0], static=True)
    _cast_slabs(cast_in, cast_out)
    h = h_ref[...]

    def head(seg, gain, scale):
        n = _head_norm(seg, gain)
        if rope:
            n = _rope(n, cos_ref[...], sin_ref[...])
        if scale != 1.0:
            n = n * scale
        return n

    for t in range(nj):
        j = t + j0
        c0 = t * IN_TN
        acc = jnp.dot(h, w_ref[:, c0:c0 + IN_TN], preferred_element_type=F32)
        if j < 2:
            for hd in range(IN_TN // HEAD_DIM):
                sl = slice(hd * HEAD_DIM, (hd + 1) * HEAD_DIM)
                osl = slice(c0 + hd * HEAD_DIM, c0 + (hd + 1) * HEAD_DIM)
                o_ref[:, osl] = head(acc[:, sl], gq_ref[...], q_scale).astype(o_ref.dtype)
        elif j == 2:
            for hd in range(N_KV_HEADS):
                sl = slice(hd * HEAD_DIM, (hd + 1) * HEAD_DIM)
                osl = slice(c0 + hd * HEAD_DIM, c0 + (hd + 1) * HEAD_DIM)
                o_ref[:, osl] = head(acc[:, sl], gk_ref[...], 1.0).astype(o_ref.dtype)
            o_ref[:, c0 + KV_W:c0 + IN_TN] = acc[:, KV_W:].astype(o_ref.dtype)
        else:
            o_ref[:, c0:c0 + IN_TN] = acc.astype(o_ref.dtype)


def _const_spec(shape, idx):
    return pl.BlockSpec(shape, lambda i: idx, pipeline_mode=pl.Buffered(1))


def _in_proj(x2, mod3, mod_row, g, w, gq, gk, cos_t, sin_t, *, tm, rope, j0, nj, seq, cast=()):
    m, d = x2.shape
    tiles_per_seq = seq // tm
    ncols = nj * IN_TN
    assert j0 % nj == 0
    q_scale = HEAD_DIM ** -0.5 * LOG2E
    kern = functools.partial(_in_kernel, rope=rope, j0=j0, nj=nj, q_scale=q_scale, n_cast=len(cast))
    cast_arrays, cast_in, cast_out, cast_shapes = _cast_plan(cast, m // tm, lambda i: i)
    outs = pl.pallas_call(
        kern,
        grid=(m // tm,),
        in_specs=[pl.BlockSpec((tm, d), lambda i: (i, 0)),
                  pl.BlockSpec((1, 1, d), lambda i: (mod_row(i), 0, 0)),
                  pl.BlockSpec((1, 1, d), lambda i: (mod_row(i), 0, 1)),
                  _const_spec((1, d), (0, 0)),
                  _const_spec((d, ncols), (0, j0 // nj)),
                  _const_spec((1, HEAD_DIM), (0, 0)),
                  _const_spec((1, HEAD_DIM), (0, 0)),
                  pl.BlockSpec((tm, HEAD_DIM), lambda i: (i % tiles_per_seq, 0)),
                  pl.BlockSpec((tm, HEAD_DIM), lambda i: (i % tiles_per_seq, 0))] + cast_in,
        out_specs=[pl.BlockSpec((tm, ncols), lambda i: (i, 0))] + cast_out,
        out_shape=[jax.ShapeDtypeStruct((m, ncols), BF16)] + cast_shapes,
        scratch_shapes=[pltpu.VMEM((tm, d), BF16)],
        compiler_params=_cparams(1),
        name="in_proj",
    )(x2, mod3, mod3, g, w, gq, gk, cos_t, sin_t, *cast_arrays)
    return outs[0], tuple(outs[1:])


def _attn_kernel(q_ref, kc_ref, vc_ref, kl_ref, vl_ref, gg_ref, *rest, has_latent, n_cast):
    cast_in, o_ref, cast_out, acc_ref = rest[:n_cast], rest[n_cast], rest[n_cast + 1:2 * n_cast + 1], rest[-1]
    _cast_slabs(cast_in, cast_out)
    tq = q_ref.shape[0]
    dn = (((1,), (1,)), ((), ()))
    ssq = jnp.zeros((tq, 1), F32)
    for hd in range(N_Q_HEADS):
        kv = hd // GQA_GROUP
        ksl = slice(kv * HEAD_DIM, (kv + 1) * HEAD_DIM)
        q = q_ref[:, hd * HEAD_DIM:(hd + 1) * HEAD_DIM]
        s_c = lax.dot_general(q, kc_ref[:, ksl], dn, preferred_element_type=F32)
        mx = jnp.max(s_c, axis=-1, keepdims=True)
        if has_latent:
            s_l = lax.dot_general(q, kl_ref[:, ksl], dn, preferred_element_type=F32)
            mx = jnp.maximum(mx, jnp.max(s_l, axis=-1, keepdims=True))
        p_c = jnp.exp2(s_c - mx)
        den = jnp.sum(p_c, axis=-1, keepdims=True)
        o = jnp.dot(p_c.astype(BF16), vc_ref[:, ksl], preferred_element_type=F32)
        if has_latent:
            p_l = jnp.exp2(s_l - mx)
            den = den + jnp.sum(p_l, axis=-1, keepdims=True)
            o = o + jnp.dot(p_l.astype(BF16), vl_ref[:, ksl], preferred_element_type=F32)
        o = o * (1.0 / den)
        ssq = ssq + jnp.sum(o * o, axis=-1, keepdims=True)
        acc_ref[:, hd * HEAD_DIM:(hd + 1) * HEAD_DIM] = o
    inv = lax.rsqrt(ssq * (1.0 / ATTN_W) + EPS)
    o_ref[...] = (acc_ref[...] * inv * gg_ref[...]).astype(o_ref.dtype)


def _attention(qsrc, csrc, lsrc, gg, *, n_batch, q_len, tq, kc_blk, has_latent, n_ctx, seq, cast=()):
    m = qsrc.shape[0]
    tiles = q_len // tq
    kern = functools.partial(_attn_kernel, has_latent=has_latent, n_cast=len(cast))
    cast_arrays, cast_in, cast_out, cast_shapes = _cast_plan(cast, n_batch * tiles, lambda b, i: b * tiles + i)
    outs = pl.pallas_call(
        kern,
        grid=(n_batch, tiles),
        in_specs=[pl.BlockSpec((tq, ATTN_W), lambda b, i: (b * tiles + i, 0)),
                  pl.BlockSpec((n_ctx, KV_W), lambda b, i: (b, kc_blk)),
                  pl.BlockSpec((n_ctx, KV_W), lambda b, i: (b, kc_blk + 1)),
                  pl.BlockSpec((seq, KV_W), lambda b, i: (b, ATTN_W // KV_W)),
                  pl.BlockSpec((seq, KV_W), lambda b, i: (b, ATTN_W // KV_W + 1)),
                  pl.BlockSpec((1, ATTN_W), lambda b, i: (0, 0))] + cast_in,
        out_specs=[pl.BlockSpec((tq, ATTN_W), lambda b, i: (b * tiles + i, 0))] + cast_out,
        out_shape=[jax.ShapeDtypeStruct((m, ATTN_W), BF16)] + cast_shapes,
        scratch_shapes=[pltpu.VMEM((tq, ATTN_W), F32)],
        compiler_params=_cparams(2),
        name="attention",
    )(qsrc, csrc, csrc, lsrc, lsrc, gg, *cast_arrays)
    return outs[0], tuple(outs[1:])


MIX_CHUNK = 128
SCONV_PAD = 8
CONF_PAD = 16


LANES = 128
SUBLANES = 8
N_LANE_GROUPS = SCONV_W // LANES


def _window_conv(p_ref, slot, w_ref, group, first, n_taps, ch):
    lanes = slice(group * LANES, (group + 1) * LANES)
    win = p_ref[slot, :, lanes]
    n_rows = win.shape[0]
    acc = None
    for phase in range(SUBLANES):
        taps = [k for k in range(n_taps) if (first + k) % SUBLANES == phase]
        if not taps:
            continue
        shifted = win if phase == 0 else pltpu.roll(win, n_rows - phase, 0)
        for k in taps:
            base = first + k - phase
            term = w_ref[k:k + 1, lanes] * shifted[base:base + ch]
            acc = term if acc is None else acc + term
    return acc


def _mix_kernel(sb_ref, sg_ref, sx_ref, ga_ref, gb_ref, sw_ref, dw_ref, db_ref, lng_ref, lnb_ref,
                pw_ref, pb_ref, ggs_ref, ggc_ref, o_ref, p1_ref, p2_ref, a_ref):
    seq = sb_ref.shape[0]
    ch = MIX_CHUNK
    n_chunks = seq // ch
    hs, hc = SCONV_PAD, CONF_PAD
    p1_ref[1, 0:hs, :] = jnp.zeros((hs, SCONV_W), F32)
    p1_ref[n_chunks, ch + hs:ch + 2 * hs, :] = jnp.zeros((hs, SCONV_W), F32)
    p2_ref[1, 0:hc, :] = jnp.zeros((hc, CONF_W), F32)
    p2_ref[n_chunks, ch + hc:ch + 2 * hc, :] = jnp.zeros((hc, CONF_W), F32)

    def fill(c, carry):
        r = pl.multiple_of(c * ch, ch)
        rows = pl.ds(r, ch)
        v = sg_ref[rows, :].astype(F32) * sx_ref[rows, :].astype(F32)
        p1_ref[c + 1, hs:hs + ch, :] = v
        p1_ref[c, ch + hs:ch + 2 * hs, :] = v[0:hs]
        p1_ref[c + 2, 0:hs, :] = v[ch - hs:ch]
        u = ga_ref[rows, :].astype(F32) * _sigmoid(gb_ref[rows, :].astype(F32))
        p2_ref[c + 1, hc:hc + ch, :] = u
        p2_ref[c, ch + hc:ch + 2 * hc, :] = u[0:hc]
        p2_ref[c + 2, 0:hc, :] = u[ch - hc:ch]
        return carry

    lax.fori_loop(0, n_chunks, fill, 0)

    def conv(c, carry):
        r = pl.multiple_of(c * ch, ch)
        rows = pl.ds(r, ch)
        acc = jnp.concatenate(
            [_window_conv(p1_ref, c + 1, sw_ref, g, hs - SCONV_K // 2, SCONV_K, ch) for g in range(N_LANE_GROUPS)],
            axis=1)
        ys = sb_ref[rows, :].astype(F32) * acc
        ms = jnp.mean(ys * ys, axis=-1, keepdims=True)
        o_ref[rows, 0:SCONV_W] = (ys * lax.rsqrt(ms + EPS) * ggs_ref[...]).astype(o_ref.dtype)

        u = db_ref[...] + jnp.concatenate(
            [_window_conv(p2_ref, c + 1, dw_ref, g, hc - CONF_K // 2, CONF_K, ch) for g in range(N_LANE_GROUPS)],
            axis=1)
        mu = jnp.mean(u, axis=-1, keepdims=True)
        uc = u - mu
        var = jnp.mean(uc * uc, axis=-1, keepdims=True)
        v = uc * lax.rsqrt(var + EPS) * lng_ref[...] + lnb_ref[...]
        a_ref[rows, :] = (v * _sigmoid(v)).astype(a_ref.dtype)
        return carry

    lax.fori_loop(0, n_chunks, conv, 0)

    yc = jnp.dot(a_ref[...], pw_ref[...], preferred_element_type=F32) + pb_ref[...]
    ms = jnp.mean(yc * yc, axis=-1, keepdims=True)
    o_ref[:, SCONV_W:] = (yc * lax.rsqrt(ms + EPS) * ggc_ref[...]).astype(o_ref.dtype)


def _local_mixers(src, sconv_w, conf_dw, conf_db, ln_g, ln_b, pw, pb, ggs, ggc, *, n_seq, seq):
    w = SCONV_W
    col = lambda k: pl.BlockSpec((seq, w), lambda b: (b, k))
    vec = lambda n: pl.BlockSpec((1, n), lambda b: (0, 0))
    return pl.pallas_call(
        _mix_kernel,
        grid=(n_seq,),
        in_specs=[col(3), col(4), col(5), col(6), col(7),
                  pl.BlockSpec((SCONV_K, w), lambda b: (0, 0)),
                  pl.BlockSpec((CONF_K, w), lambda b: (0, 0)),
                  vec(w), vec(w), vec(w),
                  pl.BlockSpec((w, w), lambda b: (0, 0)),
                  vec(w), vec(w), vec(w)],
        out_specs=pl.BlockSpec((seq, 2 * w), lambda b: (b, 0)),
        out_shape=jax.ShapeDtypeStruct((src.shape[0], 2 * w), BF16),
        scratch_shapes=[pltpu.VMEM((seq // MIX_CHUNK + 2, MIX_CHUNK + 2 * SCONV_PAD, w), F32),
                        pltpu.VMEM((seq // MIX_CHUNK + 2, MIX_CHUNK + 2 * CONF_PAD, w), F32),
                        pltpu.VMEM((seq, w), BF16)],
        compiler_params=_cparams(1),
        name="local_mixers",
    )(src, src, src, src, src, sconv_w, conf_dw, conf_db, ln_g, ln_b, pw, pb, ggs, ggc)


OUT_TN = 512


def _out_kernel(a_ref, m_ref, wa_ref, wm_ref, x_ref, gate_ref, o_ref):
    a = a_ref[...]
    mx = m_ref[...]
    for t in range(o_ref.shape[1] // OUT_TN):
        sl = slice(t * OUT_TN, (t + 1) * OUT_TN)
        acc = (jnp.dot(a, wa_ref[:, sl], preferred_element_type=F32)
               + jnp.dot(mx, wm_ref[:, sl], preferred_element_type=F32))
        o_ref[:, sl] = x_ref[:, sl] + gate_ref[0, :, sl] * acc


def _out_proj(att, mix, w, x2, mod3, mod_row, *, tm):
    m, d = x2.shape
    half = att.shape[1]
    return pl.pallas_call(
        _out_kernel,
        grid=(m // tm,),
        in_specs=[pl.BlockSpec((tm, half), lambda i: (i, 0)),
                  pl.BlockSpec((tm, half), lambda i: (i, 0)),
                  _const_spec((half, d), (0, 0)),
                  _const_spec((half, d), (1, 0)),
                  pl.BlockSpec((tm, d), lambda i: (i, 0)),
                  pl.BlockSpec((1, 1, d), lambda i: (mod_row(i), 0, 2))],
        out_specs=pl.BlockSpec((tm, d), lambda i: (i, 0)),
        out_shape=jax.ShapeDtypeStruct((m, d), F32),
        compiler_params=_cparams(1),
        name="out_proj",
    )(att, mix, w, w, x2, mod3)


MXU_WIDTH = 256


def _swiglu_accumulate(h, wg_ref, wu_ref, wd_refs, o_ref, assign=False):
    tm, tf = h.shape[0], wg_ref.shape[1]
    if tf % (2 * MXU_WIDTH) == 0:
        splits = [(slice(0, tm), slice(p * tf // 2, (p + 1) * tf // 2)) for p in range(2)]
    else:
        splits = [(slice(p * tm // 2, (p + 1) * tm // 2), slice(0, tf)) for p in range(2)]
    ups = [(jnp.dot(h[rows], wg_ref[:, cols], preferred_element_type=F32),
            jnp.dot(h[rows], wu_ref[:, cols], preferred_element_type=F32)) for rows, cols in splits]
    acts = [(gv * _sigmoid(gv) * uv).astype(BF16) for gv, uv in ups]
    same_rows = splits[0][0] == splits[1][0]
    col = 0
    for wd_ref in wd_refs:
        width = wd_ref.shape[1]
        downs = [jnp.dot(act, wd_ref[cols, :], preferred_element_type=F32) for act, (_, cols) in zip(acts, splits)]
        pieces = [(splits[0][0], downs[0] + downs[1])] if same_rows else list(zip((s[0] for s in splits), downs))
        for rows, part in pieces:
            if assign:
                o_ref[rows, col:col + width] = part
            else:
                o_ref[rows, col:col + width] += part
        col += width


def _ffn_kernel(x_ref, shift_ref, scale_ref, gate_ref, g_ref, wg_ref, wu_ref, wd_ref, *rest, n_cast):
    cast_in, o_ref, cast_out, h_ref = rest[:n_cast], rest[n_cast], rest[n_cast + 1:2 * n_cast + 1], rest[-1]
    j = pl.program_id(1)

    @pl.when(j == 0)
    def _():
        _cast_slabs(cast_in, cast_out)
        _norm_modulate_rows(x_ref, h_ref, g_ref[...], shift_ref[0], scale_ref[0], static=True)
        _swiglu_accumulate(h_ref[...], wg_ref, wu_ref, (wd_ref,), o_ref, assign=True)

    @pl.when(j > 0)
    def _():
        _cast_slabs(cast_in, cast_out)
        _swiglu_accumulate(h_ref[...], wg_ref, wu_ref, (wd_ref,), o_ref)

    @pl.when(j == pl.num_programs(1) - 1)
    def _():
        o_ref[...] = x_ref[...] + gate_ref[0] * o_ref[...]


BF16_SUBLANE_TILE = 16


def _slab_rows(rows, n_lead, n_steps):
    for height in range(BF16_SUBLANE_TILE, rows + 1, BF16_SUBLANE_TILE):
        if rows % height == 0 and n_lead * (rows // height) <= n_steps:
            return height
    raise ValueError(f"no slab height for rows={rows}, n_lead={n_lead}, n_steps={n_steps}")


def _cast_plan(jobs, n_steps, step_of):
    arrays, in_specs, out_specs, out_shapes = [], [], [], []
    for arr, col_block, n_col_blocks in jobs:
        n_lead, rows, cols = arr.shape
        width = cols // n_col_blocks
        height = _slab_rows(rows, n_lead, n_steps)
        per_lead = rows // height
        last = n_lead * per_lead - 1

        def index(*grid_idx, per_lead=per_lead, last=last, col=0):
            s = jnp.minimum(step_of(*grid_idx), last)
            return (s // per_lead, s % per_lead, col)

        arrays.append(arr)
        in_specs.append(pl.BlockSpec((1, height, width), functools.partial(index, col=col_block)))
        out_specs.append(pl.BlockSpec((1, height, width), index))
        out_shapes.append(jax.ShapeDtypeStruct((n_lead, rows, width), BF16))
    return arrays, in_specs, out_specs, out_shapes


def _cast_slabs(cast_in, cast_out):
    for src, dst in zip(cast_in, cast_out):
        dst[...] = src[...].astype(dst.dtype)


def _dense_ffn(x2, mod3, mod_row, g, wg, wu, wd, *, tm, tf, cast=()):
    m, d = x2.shape
    f = wg.shape[2]
    nf = f // tf
    n_steps = (m // tm) * nf
    modspec = lambda k: pl.BlockSpec((1, 1, d), lambda i, j: (mod_row(i), 0, k))
    cast_arrays, cast_in, cast_out, cast_shapes = _cast_plan(cast, n_steps, lambda i, j: i * nf + j)
    outs = pl.pallas_call(
        functools.partial(_ffn_kernel, n_cast=len(cast)),
        grid=(m // tm, nf),
        in_specs=[pl.BlockSpec((tm, d), lambda i, j: (i, 0)),
                  modspec(3), modspec(4), modspec(5),
                  pl.BlockSpec((1, d), lambda i, j: (0, 0)),
                  pl.BlockSpec((None, d, tf), lambda i, j: (0, 0, j)),
                  pl.BlockSpec((None, d, tf), lambda i, j: (0, 0, j)),
                  pl.BlockSpec((None, tf, d), lambda i, j: (0, j, 0))] + cast_in,
        out_specs=[pl.BlockSpec((tm, d), lambda i, j: (i, 0))] + cast_out,
        out_shape=[jax.ShapeDtypeStruct((m, d), F32)] + cast_shapes,
        scratch_shapes=[pltpu.VMEM((tm, d), BF16)],
        compiler_params=_cparams(2),
        name="dense_ffn",
    )(x2, mod3, mod3, mod3, g, wg, wu, wd, *cast_arrays)
    return outs[0], tuple(outs[1:])


ROUTER_ROWS = 16


U32 = jnp.uint32


def _pack_bf16_pair(lo, hi):
    return (pltpu.bitcast(lo, U32) >> 16) | pltpu.bitcast(hi, U32)


def _unpack_bf16_pair(packed):
    lo = pltpu.bitcast(packed << 16, F32)
    hi = pltpu.bitcast(packed & jnp.uint32(0xFFFF0000), F32)
    return lo.astype(BF16), hi.astype(BF16)


def _router_kernel(x_ref, shift_ref, scale_ref, g_ref, rw_ref, tp_ref, idx_ref, gate_ref, cnt_ref,
                   t_ref, tri_ref, base_ref):
    tm = x_ref.shape[0]

    @pl.when(pl.program_id(0) == 0)
    def _():
        r = lax.broadcasted_iota(jnp.int32, (tm, tm), 0)
        c = lax.broadcasted_iota(jnp.int32, (tm, tm), 1)
        tri_ref[...] = jnp.where(r < c, 1.0, 0.0).astype(BF16)
        base_ref[...] = jnp.zeros(base_ref.shape, F32)

    _norm_modulate_rows(x_ref, t_ref, g_ref[...], shift_ref[0], scale_ref[0], static=True)

    t = t_ref[...]
    t_hi = t.astype(BF16)
    t_hi32 = t_hi.astype(F32)
    t_lo = (t - t_hi32).astype(BF16)
    half = t.shape[1] // 2
    tp_ref[...] = _pack_bf16_pair(t_hi32[:, :half], t_hi32[:, half:])
    w = rw_ref[...]
    w_hi = w.astype(BF16)
    w_lo = (w - w_hi.astype(F32)).astype(BF16)
    dn = (((1,), (1,)), ((), ()))
    logits = (lax.dot_general(w_hi, t_hi, dn, preferred_element_type=F32)
              + lax.dot_general(w_lo, t_hi, dn, preferred_element_type=F32)
              + lax.dot_general(w_hi, t_lo, dn, preferred_element_type=F32))

    e = lax.broadcasted_iota(jnp.int32, (ROUTER_ROWS, tm), 0).astype(F32)
    neg = jnp.float32(-jnp.inf)
    lg = jnp.where(e < N_EXPERTS, logits, neg)
    m1 = jnp.max(lg, axis=0, keepdims=True)
    i1 = jnp.min(jnp.where(lg == m1, e, float(ROUTER_ROWS)), axis=0, keepdims=True)
    lg2 = jnp.where(e == i1, neg, lg)
    m2 = jnp.max(lg2, axis=0, keepdims=True)
    i2 = jnp.min(jnp.where(lg2 == m2, e, float(ROUTER_ROWS)), axis=0, keepdims=True)
    ex = jnp.exp(m2 - m1)
    den = 1.0 + ex
    gate_ref[0:1, :] = 1.0 / den
    gate_ref[1:2, :] = ex / den

    hit1 = e == i1
    hit2 = e == i2
    onehot = jnp.where(hit1 | hit2, 1.0, 0.0)
    prefix = jnp.dot(onehot.astype(BF16), tri_ref[...], preferred_element_type=F32) + base_ref[:, 0:1]
    r1 = jnp.sum(jnp.where(hit1, prefix, 0.0), axis=0, keepdims=True)
    r2 = jnp.sum(jnp.where(hit2, prefix, 0.0), axis=0, keepdims=True)
    idx_ref[0:1, :] = i1.astype(jnp.int32)
    idx_ref[1:2, :] = i2.astype(jnp.int32)
    idx_ref[2:3, :] = r1.astype(jnp.int32)
    idx_ref[3:4, :] = r2.astype(jnp.int32)
    base_ref[...] = base_ref[...] + jnp.sum(onehot, axis=1, keepdims=True)
    cnt_ref[...] = base_ref[...].astype(jnp.int32)


def _router(x2, mod3, mod_row, g, rw16, *, tm):
    m, d = x2.shape
    modspec = lambda k: pl.BlockSpec((1, 1, d), lambda i: (mod_row(i), 0, k))
    return pl.pallas_call(
        _router_kernel,
        grid=(m // tm,),
        in_specs=[pl.BlockSpec((tm, d), lambda i: (i, 0)),
                  modspec(3), modspec(4),
                  pl.BlockSpec((1, d), lambda i: (0, 0)),
                  pl.BlockSpec((ROUTER_ROWS, d), lambda i: (0, 0))],
        out_specs=[pl.BlockSpec((tm, d // 2), lambda i: (i, 0)),
                   pl.BlockSpec((4, tm), lambda i: (0, i)),
                   pl.BlockSpec((2, tm), lambda i: (0, i)),
                   pl.BlockSpec((ROUTER_ROWS, 128), lambda i: (0, 0))],
        out_shape=[jax.ShapeDtypeStruct((m, d // 2), U32),
                   jax.ShapeDtypeStruct((4, m), jnp.int32),
                   jax.ShapeDtypeStruct((2, m), F32),
                   jax.ShapeDtypeStruct((ROUTER_ROWS, 128), jnp.int32)],
        scratch_shapes=[pltpu.VMEM((tm, d), F32), pltpu.VMEM((tm, tm), BF16),
                        pltpu.VMEM((ROUTER_ROWS, 128), F32)],
        compiler_params=_cparams(1),
        name="router",
    )(x2, mod3, mod3, g, rw16)


def _row_copy(src_ref, src_row, dst_ref, dst_row, sem):
    return pltpu.make_async_copy(src_ref.at[pl.ds(src_row, 1), :], dst_ref.at[pl.ds(dst_row, 1), :], sem)


DMA_ISSUE_UNROLL = 8


def _scatter_kernel(dest_ref, pad_ref, t_ref, buf_ref, zero_ref, sem, zsem, *, n_tok, n_tiles):
    tm = t_ref.shape[0]
    base = pl.program_id(0) * tm

    @pl.when(pl.program_id(0) == 0)
    def _():
        zero_ref[...] = jnp.zeros(zero_ref.shape, zero_ref.dtype)
        for e in range(N_EXPERTS):
            start, length = pad_ref[e], pad_ref[N_EXPERTS + e]

            def fill(r, carry, start=start):
                _row_copy(zero_ref, 0, buf_ref, start + r, zsem).start()
                return carry

            def drain(r, carry):
                _row_copy(zero_ref, 0, buf_ref, 0, zsem).wait()
                return carry

            lax.fori_loop(0, length, fill, 0)
            lax.fori_loop(0, length, drain, 0)

        def tile_copy(tile):
            row0 = pl.multiple_of(tile * EXPERT_TM, EXPERT_TM)
            return pltpu.make_async_copy(zero_ref, buf_ref.at[pl.ds(row0, EXPERT_TM), :], zsem)

        def fill_tile(tile, carry):
            tile_copy(tile).start()
            return carry

        def drain_tile(tile, carry):
            tile_copy(tile).wait()
            return carry

        lax.fori_loop(pad_ref[2 * N_EXPERTS], n_tiles, fill_tile, 0)
        lax.fori_loop(pad_ref[2 * N_EXPERTS], n_tiles, drain_tile, 0)

    def start(r, carry):
        for k in range(2):
            _row_copy(t_ref, r, buf_ref, dest_ref[k * n_tok + base + r], sem).start()
        return carry

    lax.fori_loop(0, tm, start, 0, unroll=DMA_ISSUE_UNROLL)
    for k in range(2):
        pltpu.make_async_copy(t_ref, buf_ref.at[pl.ds(0, tm), :], sem).wait()


def _scatter_rows(dest, pad_info, t, *, tm, n_tiles):
    m, d = t.shape
    grid_spec = pltpu.PrefetchScalarGridSpec(
        num_scalar_prefetch=2,
        grid=(m // tm,),
        in_specs=[pl.BlockSpec((tm, d), lambda i, dest, pad: (i, 0))],
        out_specs=pl.BlockSpec(memory_space=pl.ANY),
        scratch_shapes=[pltpu.VMEM((EXPERT_TM, d), t.dtype),
                        pltpu.SemaphoreType.DMA(()), pltpu.SemaphoreType.DMA(())],
    )
    return pl.pallas_call(
        functools.partial(_scatter_kernel, n_tok=m, n_tiles=n_tiles),
        grid_spec=grid_spec,
        out_shape=jax.ShapeDtypeStruct((n_tiles * EXPERT_TM, d), t.dtype),
        compiler_params=_cparams(1),
        name="moe_scatter",
    )(dest, pad_info, t)


N_WEIGHT_SLOTS = 3


def _expert_kernel(te_ref, tv_ref, tx_ref, x_ref, *rest, n_wd, tf, nf):
    del tx_ref
    w_hbm = rest[:2 + n_wd]
    o_ref, xb_ref = rest[2 + n_wd], rest[3 + n_wd]
    w_buf = rest[4 + n_wd:6 + 2 * n_wd]
    sem = rest[-1]
    i = pl.program_id(0)
    n = pl.num_programs(0)

    def copies(tile, j, slot):
        e = te_ref[tile]
        cols = pl.ds(j * tf, tf)
        srcs = [w_hbm[0].at[e, :, cols], w_hbm[1].at[e, :, cols]] + [w.at[e, cols, :] for w in w_hbm[2:]]
        return [pltpu.make_async_copy(src, buf.at[slot], sem.at[slot, k])
                for k, (src, buf) in enumerate(zip(srcs, w_buf))]

    valid = tv_ref[i] == 1
    nxt = jnp.minimum(i + 1, n - 1)
    next_live = jnp.logical_and(i + 1 < n, tv_ref[nxt] == 1)

    @pl.when(jnp.logical_not(valid))
    def _():
        o_ref[...] = jnp.zeros(o_ref.shape, o_ref.dtype)

    @pl.when(jnp.logical_and(valid, i == 0))
    def _():
        for c in copies(i, 0, 0):
            c.start()

    @pl.when(valid)
    def _():
        half = x_ref.shape[1]
        xb_ref[:, :half], xb_ref[:, half:] = _unpack_bf16_pair(x_ref[...])
        xb = xb_ref[...]
        for j in range(nf):
            slot = j % N_WEIGHT_SLOTS
            ahead = copies(i, j + 1, (j + 1) % N_WEIGHT_SLOTS) if j + 1 < nf else copies(nxt, 0, 0)
            for c in ahead:
                c.start()
            for c in copies(i, j, slot):
                c.wait()
            _swiglu_accumulate(xb, w_buf[0].at[slot], w_buf[1].at[slot], [b.at[slot] for b in w_buf[2:]],
                               o_ref, assign=j == 0)

    @pl.when(jnp.logical_and(valid, jnp.logical_not(next_live)))
    def _():
        for c in copies(nxt, 0, 0):
            c.wait()


def _experts(tile_e, tile_v, tile_x, buf, wg, wu, wds, *, tf):
    rows, d_packed = buf.shape
    d = wg.shape[1]
    f = wg.shape[2]
    nf = f // tf
    assert nf % N_WEIGHT_SLOTS != 1, "slot 0 must be free while the last hidden tile computes"
    any_spec = pl.BlockSpec(memory_space=pl.ANY)
    grid_spec = pltpu.PrefetchScalarGridSpec(
        num_scalar_prefetch=3,
        grid=(rows // EXPERT_TM,),
        in_specs=[pl.BlockSpec((EXPERT_TM, d_packed), lambda i, te, tv, tx: (tx[i], 0))]
                 + [any_spec] * (2 + len(wds)),
        out_specs=pl.BlockSpec((EXPERT_TM, d), lambda i, te, tv, tx: (i, 0)),
        scratch_shapes=[pltpu.VMEM((EXPERT_TM, d), BF16),
                        pltpu.VMEM((N_WEIGHT_SLOTS, d, tf), BF16),
                        pltpu.VMEM((N_WEIGHT_SLOTS, d, tf), BF16)]
                       + [pltpu.VMEM((N_WEIGHT_SLOTS, tf, wd.shape[2]), BF16) for wd in wds]
                       + [pltpu.SemaphoreType.DMA((N_WEIGHT_SLOTS, 2 + len(wds)))],
    )
    return pl.pallas_call(
        functools.partial(_expert_kernel, n_wd=len(wds), tf=tf, nf=nf),
        grid_spec=grid_spec,
        out_shape=jax.ShapeDtypeStruct((rows, d), F32),
        compiler_params=_cparams(1),
        name="moe_experts",
    )(tile_e, tile_v, tile_x, buf, wg, wu, *wds)


def _combine_kernel(dest_ref, x_ref, gate5_ref, gates_ref, gf_ref, ybuf_ref, o_ref, rows_ref, sem, *, n_tok):
    tm = x_ref.shape[0]
    i = pl.program_id(0)

    def issue(tile, slot):
        base = tile * tm

        def start(r, carry):
            for k in range(2):
                _row_copy(ybuf_ref, dest_ref[k * n_tok + base + r], rows_ref.at[slot, k], r, sem.at[slot]).start()
            return carry

        lax.fori_loop(0, tm, start, 0, unroll=DMA_ISSUE_UNROLL)

    @pl.when(i == 0)
    def _():
        issue(0, 0)

    @pl.when(i + 1 < pl.num_programs(0))
    def _():
        issue(i + 1, (i + 1) % 2)

    slot = i % 2
    for k in range(2):
        pltpu.make_async_copy(ybuf_ref.at[pl.ds(0, tm), :], rows_ref.at[slot, k], sem.at[slot]).wait()

    eye = (lax.broadcasted_iota(jnp.int32, (tm, tm), 0) == lax.broadcasted_iota(jnp.int32, (tm, tm), 1))
    g0 = jnp.sum(jnp.where(eye, gates_ref[0:1, :], 0.0), axis=1, keepdims=True)
    g1 = jnp.sum(jnp.where(eye, gates_ref[1:2, :], 0.0), axis=1, keepdims=True)
    y = g0 * rows_ref[slot, 0] + g1 * rows_ref[slot, 1]
    xn = x_ref[...] + gate5_ref[0] * y
    ms = jnp.mean(xn * xn, axis=-1, keepdims=True)
    o_ref[...] = xn * lax.rsqrt(ms + EPS) * gf_ref[...]


def _combine(dest, x2, mod3, mod_row, gates, g_final, ybuf, *, tm):
    m, d = x2.shape
    grid_spec = pltpu.PrefetchScalarGridSpec(
        num_scalar_prefetch=1,
        grid=(m // tm,),
        in_specs=[pl.BlockSpec((tm, d), lambda i, dest: (i, 0)),
                  pl.BlockSpec((1, 1, d), lambda i, dest: (mod_row(i), 0, 5)),
                  pl.BlockSpec((2, tm), lambda i, dest: (0, i)),
                  pl.BlockSpec((1, d), lambda i, dest: (0, 0)),
                  pl.BlockSpec(memory_space=pl.ANY)],
        out_specs=pl.BlockSpec((tm, d), lambda i, dest: (i, 0)),
        scratch_shapes=[pltpu.VMEM((2, 2, tm, d), F32), pltpu.SemaphoreType.DMA((2,))],
    )
    return pl.pallas_call(
        functools.partial(_combine_kernel, n_tok=m),
        grid_spec=grid_spec,
        out_shape=jax.ShapeDtypeStruct((m, d), F32),
        compiler_params=_cparams(1),
        name="moe_combine",
    )(dest, x2, mod3, gates, g_final, ybuf)


def _rope_tables(seq):
    rows = seq // GRID_W
    row = jnp.repeat(jnp.arange(rows), GRID_W).astype(F32)
    col = jnp.tile(jnp.arange(GRID_W), rows).astype(F32)
    axis_dim = HEAD_DIM // 2
    inv_freq = ROPE_THETA ** (-jnp.arange(0, axis_dim, 2, dtype=F32) / axis_dim)
    ang_r = row[:, None] * inv_freq
    ang_c = col[:, None] * inv_freq
    cr, sr, cc, sc = jnp.cos(ang_r), jnp.sin(ang_r), jnp.cos(ang_c), jnp.sin(ang_c)
    return (jnp.concatenate([cr, cr, cc, cc], axis=-1),
            jnp.concatenate([-sr, sr, -sc, sc], axis=-1))


def kernel(x, c, ctx, c_ctx, w_ada, b_ada, g_mix, w_in, g_q, g_k, sconv_w, conf_dw, conf_db, conf_ln_g,
           conf_ln_b, conf_pw, conf_pb, g_group, w_o, g_ffn, dense_wg, dense_wu, dense_wd, router_w,
           moe_wg, moe_wu, moe_wd, g_final):
    b, s, d = x.shape
    n_ctx = ctx.shape[1]
    depth = w_ada.shape[0]
    assert b + 1 <= MOD_ROWS and depth == 2
    m_lat, m_ctx = b * s, b * n_ctx
    ctx_row = b

    cin = jnp.concatenate([c, c_ctx[None, :], jnp.zeros((MOD_ROWS - b - 1, d), F32)], axis=0)
    mod = _ada(cin, w_ada, b_ada)
    cos_t, sin_t = _rope_tables(s)
    row2 = lambda v: v.reshape(1, -1)

    lat_tm = ROW_TM
    lat_row_for = lambda tm: (lambda i: i // (s // tm))
    lat_row = lat_row_for(lat_tm)
    ctx_tm = ROW_TM
    ctx_mod_row = lambda i: ctx_row

    x2 = x.reshape(m_lat, d)
    xc2 = ctx.reshape(m_ctx, d)
    out = None
    moe_wd_b = ()
    for l in range(depth):
        last = l == depth - 1
        mod3 = mod[l].reshape(MOD_ROWS, 1, 6 * d)
        w_in_b = w_in[l].astype(BF16)
        w_o_b = w_o[l].astype(BF16)
        pw_b = conf_pw[l].astype(BF16)
        gg = g_group[l]
        gga, ggs, ggc = row2(gg[:ATTN_W]), row2(gg[ATTN_W:ATTN_W + SCONV_W]), row2(gg[ATTN_W + SCONV_W:])
        mixer_args = (sconv_w[l], conf_dw[l], row2(conf_db[l]), row2(conf_ln_g[l]), row2(conf_ln_b[l]),
                      pw_b, row2(conf_pb[l]), ggs, ggc)
        in_args = (row2(g_mix[l]), w_in_b, row2(g_q[l]), row2(g_k[l]), cos_t, sin_t)

        dense_here = l % 2 == 0
        k_dense = l // 2
        cast = (tuple((w[k_dense:k_dense + 1], 0, 1) for w in (dense_wg, dense_wu, dense_wd))
                if dense_here else ())
        k_moe = (l + 1) // 2 if dense_here else l // 2
        moe_wd_job = ((moe_wd[k_moe], l % 2, 2),) if k_moe < moe_wd.shape[0] else ()
        p_lat, dense_w_b = _in_proj(x2, mod3, lat_row, *in_args, tm=lat_tm, rope=True, j0=0, nj=8, seq=s,
                                    cast=cast)
        if last:
            p_ctx, _ = _in_proj(xc2, mod3, ctx_mod_row, *in_args, tm=ctx_tm, rope=False, j0=2, nj=1, seq=ctx_tm)
            kc_blk = 0
        else:
            p_ctx, _ = _in_proj(xc2, mod3, ctx_mod_row, *in_args, tm=ctx_tm, rope=False, j0=0, nj=8, seq=ctx_tm)
            kc_blk = ATTN_W // KV_W
        att, moe_wd_half = _attention(p_lat, p_ctx, p_lat, gga, n_batch=b, q_len=s, tq=ATTN_TQ, kc_blk=kc_blk,
                                      has_latent=True, n_ctx=n_ctx, seq=s, cast=moe_wd_job)
        moe_wd_b = moe_wd_b + moe_wd_half
        mix = _local_mixers(p_lat, *mixer_args, n_seq=b, seq=s)
        x2 = _out_proj(att, mix, w_o_b, x2, mod3, lat_row, tm=lat_tm)
        if not last:
            att_c, _ = _attention(p_ctx, p_ctx, p_ctx, gga, n_batch=b, q_len=n_ctx, tq=n_ctx, kc_blk=kc_blk,
                                  has_latent=False, n_ctx=n_ctx, seq=n_ctx)
            mix_c = _local_mixers(p_ctx, *mixer_args, n_seq=b, seq=n_ctx)
            xc2 = _out_proj(att_c, mix_c, w_o_b, xc2, mod3, ctx_mod_row, tm=ctx_tm)

        if l % 2 == 0:
            wg_b, wu_b, wd_b = dense_w_b
            ffn = functools.partial(_dense_ffn, g=row2(g_ffn[l]), wg=wg_b, wu=wu_b, wd=wd_b,
                                    tf=FFN_TF)
            routed_next = l + 1 < depth and (l + 1) % 2 == 1
            cast = ((moe_wg[(l + 1) // 2], 0, 1), (moe_wu[(l + 1) // 2], 0, 1)) if routed_next else ()
            x2, moe_up_b = ffn(x2, mod3, lat_row, tm=lat_tm, cast=cast)
            if not last:
                xc2, _ = ffn(xc2, mod3, ctx_mod_row, tm=ctx_tm)
        else:
            assert last, "routed layer is implemented for the final layer (latent tokens only)"
            rw16 = jnp.zeros((ROUTER_ROWS, d), F32).at[:N_EXPERTS].set(router_w[l // 2].T)
            t, idx, gates, cnt = _router(x2, mod3, lat_row, row2(g_ffn[l]), rw16, tm=lat_tm)
            counts = cnt[:N_EXPERTS, 0]
            padded = (counts + EXPERT_TM - 1) // EXPERT_TM * EXPERT_TM
            pends = jnp.cumsum(padded)
            pstarts = pends - padded
            slot_e = idx[0:2]
            slot_start = sum(jnp.where(slot_e == e, pstarts[e], 0) for e in range(N_EXPERTS))
            dest = (slot_start + idx[2:4]).reshape(-1).astype(jnp.int32)
            n_tiles = (2 * m_lat) // EXPERT_TM + N_EXPERTS
            tile_start = jnp.arange(n_tiles, dtype=jnp.int32) * EXPERT_TM
            tile_v = (tile_start < pends[-1]).astype(jnp.int32)
            last_tile = pends[-1] // EXPERT_TM - 1
            tile_x = jnp.minimum(jnp.arange(n_tiles, dtype=jnp.int32), last_tile).astype(jnp.int32)
            tile_e = jnp.minimum(jnp.sum((tile_x * EXPERT_TM)[:, None] >= pends[None, :], axis=1),
                                 N_EXPERTS - 1).astype(jnp.int32)
            pad_info = jnp.concatenate([pstarts + counts, padded - counts,
                                        (pends[-1:] // EXPERT_TM)]).astype(jnp.int32)
            buf = _scatter_rows(dest, pad_info, t, tm=lat_tm, n_tiles=n_tiles)
            ybuf = _experts(tile_e, tile_v, tile_x, buf, *moe_up_b, moe_wd_b, tf=FFN_TF)
            out = _combine(dest, x2, mod3, lat_row_for(COMBINE_TM), gates, row2(g_final), ybuf, tm=COMBINE_TM)
    return out.reshape(b, s, d)
```

```python
import functools

import jax
import jax.numpy as jnp
from jax import lax
from jax.experimental import pallas as pl
from jax.experimental.pallas import tpu as pltpu

F32 = jnp.float32
BF16 = jnp.bfloat16

GRID_W = 64
HEAD_DIM = 128
N_Q_HEADS = 8
N_KV_HEADS = 2
GQA_GROUP = N_Q_HEADS // N_KV_HEADS
ATTN_W = N_Q_HEADS * HEAD_DIM
KV_W = N_KV_HEADS * HEAD_DIM
ROPE_THETA = 10000.0
SCONV_W = 512
SCONV_K = 3
CONF_W = 512
CONF_K = 31
N_EXPERTS = 8
EPS = 1e-6
LOG2E = 1.4426950408889634

V7X_VMEM_LIMIT_BYTES = 56 * 1024 * 1024
MOD_ROWS = 16
IN_TN = 512
EXPERT_TM = 512
ROW_TM = 512
FFN_TF = 512
ATTN_TQ = 512
COMBINE_TM = 256
ADA_TN = 1536


def _cparams(n_axes):
    return pltpu.CompilerParams(dimension_semantics=("arbitrary",) * n_axes,
                                vmem_limit_bytes=V7X_VMEM_LIMIT_BYTES)


def _sigmoid(x):
    return 1.0 / (1.0 + jnp.exp(-x))


NORM_CHUNK = 16
NORM_UNROLL = 4


def _norm_modulate_rows(x_ref, h_ref, g, shift, scale, static=False):
    rows = x_ref.shape[0]
    gain = g * (1.0 + scale)

    def body(c, carry):
        r = c * NORM_CHUNK if static else pl.multiple_of(c * NORM_CHUNK, NORM_CHUNK)
        x = x_ref[pl.ds(r, NORM_CHUNK), :]
        inv = lax.rsqrt(jnp.mean(x * x, axis=-1, keepdims=True) + EPS)
        h_ref[pl.ds(r, NORM_CHUNK), :] = (x * inv * gain + shift).astype(h_ref.dtype)
        return carry

    if static:
        for c in range(rows // NORM_CHUNK):
            body(c, 0)
        return

    lax.fori_loop(0, rows // NORM_CHUNK, body, 0, unroll=NORM_UNROLL)


def _ada_kernel(c_ref, w_ref, b_ref, o_ref):
    a = c_ref[...]
    a = (a * _sigmoid(a)).astype(BF16)
    o_ref[0] = jnp.dot(a, w_ref[0].astype(BF16), preferred_element_type=F32) + b_ref[0]


def _ada(cin, w_ada, b_ada):
    depth, d, n = w_ada.shape
    tn = ADA_TN
    return pl.pallas_call(
        _ada_kernel,
        grid=(depth, n // tn),
        in_specs=[pl.BlockSpec((MOD_ROWS, d), lambda l, j: (0, 0)),
                  pl.BlockSpec((1, d, tn), lambda l, j: (l, 0, j)),
                  pl.BlockSpec((1, 1, tn), lambda l, j: (l, 0, j))],
        out_specs=pl.BlockSpec((1, MOD_ROWS, tn), lambda l, j: (l, 0, j)),
        out_shape=jax.ShapeDtypeStruct((depth, MOD_ROWS, n), F32),
        compiler_params=_cparams(2),
        name="ada",
    )(cin, w_ada, b_ada.reshape(depth, 1, n))


def _head_norm(seg, gain):
    ms = jnp.mean(seg * seg, axis=-1, keepdims=True)
    return seg * lax.rsqrt(ms + EPS) * gain


def _rope(n, cos_t, sin_t):
    lane = lax.broadcasted_iota(jnp.int32, n.shape, 1)
    fwd = pltpu.roll(n, 32, 1)
    bwd = pltpu.roll(n, 96, 1)
    partner = jnp.where((lane // 32) % 2 == 0, bwd, fwd)
    return n * cos_t + partner * sin_t


def _in_kernel(x_ref, shift_ref, scale_ref, g_ref, w_ref, gq_ref, gk_ref, cos_ref, sin_ref,
               *rest, rope, j0, nj, q_scale, n_cast):
    cast_in, o_ref, cast_out, h_ref = rest[:n_cast], rest[n_cast], rest[n_cast + 1:2 * n_cast + 1], rest[-1]
    _norm_modulate_rows(x_ref, h_ref, g_ref[...], shift_ref[0], scale_ref[0], static=True)
    _cast_slabs(cast_in, cast_out)
    h = h_ref[...]

    def head(seg, gain, scale):
        n = _head_norm(seg, gain)
        if rope:
            n = _rope(n, cos_ref[...], sin_ref[...])
        if scale != 1.0:
            n = n * scale
        return n

    for t in range(nj):
        j = t + j0
        c0 = t * IN_TN
        acc = jnp.dot(h, w_ref[:, c0:c0 + IN_TN], preferred_element_type=F32)
        if j < 2:
            for hd in range(IN_TN // HEAD_DIM):
                sl = slice(hd * HEAD_DIM, (hd + 1) * HEAD_DIM)
                osl = slice(c0 + hd * HEAD_DIM, c0 + (hd + 1) * HEAD_DIM)
                o_ref[:, osl] = head(acc[:, sl], gq_ref[...], q_scale).astype(o_ref.dtype)
        elif j == 2:
            for hd in range(N_KV_HEADS):
                sl = slice(hd * HEAD_DIM, (hd + 1) * HEAD_DIM)
                osl = slice(c0 + hd * HEAD_DIM, c0 + (hd + 1) * HEAD_DIM)
                o_ref[:, osl] = head(acc[:, sl], gk_ref[...], 1.0).astype(o_ref.dtype)
            o_ref[:, c0 + KV_W:c0 + IN_TN] = acc[:, KV_W:].astype(o_ref.dtype)
        else:
            o_ref[:, c0:c0 + IN_TN] = acc.astype(o_ref.dtype)


def _const_spec(shape, idx):
    return pl.BlockSpec(shape, lambda i: idx, pipeline_mode=pl.Buffered(1))


def _in_proj(x2, mod3, mod_row, g, w, gq, gk, cos_t, sin_t, *, tm, rope, j0, nj, seq, cast=()):
    m, d = x2.shape
    tiles_per_seq = seq // tm
    ncols = nj * IN_TN
    assert j0 % nj == 0
    q_scale = HEAD_DIM ** -0.5 * LOG2E
    kern = functools.partial(_in_kernel, rope=rope, j0=j0, nj=nj, q_scale=q_scale, n_cast=len(cast))
    cast_arrays, cast_in, cast_out, cast_shapes = _cast_plan(cast, m // tm, lambda i: i)
    outs = pl.pallas_call(
        kern,
        grid=(m // tm,),
        in_specs=[pl.BlockSpec((tm, d), lambda i: (i, 0)),
                  pl.BlockSpec((1, 1, d), lambda i: (mod_row(i), 0, 0)),
                  pl.BlockSpec((1, 1, d), lambda i: (mod_row(i), 0, 1)),
                  _const_spec((1, d), (0, 0)),
                  _const_spec((d, ncols), (0, j0 // nj)),
                  _const_spec((1, HEAD_DIM), (0, 0)),
                  _const_spec((1, HEAD_DIM), (0, 0)),
                  pl.BlockSpec((tm, HEAD_DIM), lambda i: (i % tiles_per_seq, 0)),
                  pl.BlockSpec((tm, HEAD_DIM), lambda i: (i % tiles_per_seq, 0))] + cast_in,
        out_specs=[pl.BlockSpec((tm, ncols), lambda i: (i, 0))] + cast_out,
        out_shape=[jax.ShapeDtypeStruct((m, ncols), BF16)] + cast_shapes,
        scratch_shapes=[pltpu.VMEM((tm, d), BF16)],
        compiler_params=_cparams(1),
        name="in_proj",
    )(x2, mod3, mod3, g, w, gq, gk, cos_t, sin_t, *cast_arrays)
    return outs[0], tuple(outs[1:])


def _attn_kernel(q_ref, kc_ref, vc_ref, kl_ref, vl_ref, gg_ref, *rest, has_latent, n_cast):
    cast_in, o_ref, cast_out, acc_ref = rest[:n_cast], rest[n_cast], rest[n_cast + 1:2 * n_cast + 1], rest[-1]
    _cast_slabs(cast_in, cast_out)
    tq = q_ref.shape[0]
    dn = (((1,), (1,)), ((), ()))
    ssq = jnp.zeros((tq, 1), F32)
    for hd in range(N_Q_HEADS):
        kv = hd // GQA_GROUP
        ksl = slice(kv * HEAD_DIM, (kv + 1) * HEAD_DIM)
        q = q_ref[:, hd * HEAD_DIM:(hd + 1) * HEAD_DIM]
        s_c = lax.dot_general(q, kc_ref[:, ksl], dn, preferred_element_type=F32)
        mx = jnp.max(s_c, axis=-1, keepdims=True)
        if has_latent:
            s_l = lax.dot_general(q, kl_ref[:, ksl], dn, preferred_element_type=F32)
            mx = jnp.maximum(mx, jnp.max(s_l, axis=-1, keepdims=True))
        p_c = jnp.exp2(s_c - mx)
        den = jnp.sum(p_c, axis=-1, keepdims=True)
        o = jnp.dot(p_c.astype(BF16), vc_ref[:, ksl], preferred_element_type=F32)
        if has_latent:
            p_l = jnp.exp2(s_l - mx)
            den = den + jnp.sum(p_l, axis=-1, keepdims=True)
            o = o + jnp.dot(p_l.astype(BF16), vl_ref[:, ksl], preferred_element_type=F32)
        o = o * (1.0 / den)
        ssq = ssq + jnp.sum(o * o, axis=-1, keepdims=True)
        acc_ref[:, hd * HEAD_DIM:(hd + 1) * HEAD_DIM] = o
    inv = lax.rsqrt(ssq * (1.0 / ATTN_W) + EPS)
    o_ref[...] = (acc_ref[...] * inv * gg_ref[...]).astype(o_ref.dtype)


def _attention(qsrc, csrc, lsrc, gg, *, n_batch, q_len, tq, kc_blk, has_latent, n_ctx, seq, cast=()):
    m = qsrc.shape[0]
    tiles = q_len // tq
    kern = functools.partial(_attn_kernel, has_latent=has_latent, n_cast=len(cast))
    cast_arrays, cast_in, cast_out, cast_shapes = _cast_plan(cast, n_batch * tiles, lambda b, i: b * tiles + i)
    outs = pl.pallas_call(
        kern,
        grid=(n_batch, tiles),
        in_specs=[pl.BlockSpec((tq, ATTN_W), lambda b, i: (b * tiles + i, 0)),
                  pl.BlockSpec((n_ctx, KV_W), lambda b, i: (b, kc_blk)),
                  pl.BlockSpec((n_ctx, KV_W), lambda b, i: (b, kc_blk + 1)),
                  pl.BlockSpec((seq, KV_W), lambda b, i: (b, ATTN_W // KV_W)),
                  pl.BlockSpec((seq, KV_W), lambda b, i: (b, ATTN_W // KV_W + 1)),
                  pl.BlockSpec((1, ATTN_W), lambda b, i: (0, 0))] + cast_in,
        out_specs=[pl.BlockSpec((tq, ATTN_W), lambda b, i: (b * tiles + i, 0))] + cast_out,
        out_shape=[jax.ShapeDtypeStruct((m, ATTN_W), BF16)] + cast_shapes,
        scratch_shapes=[pltpu.VMEM((tq, ATTN_W), F32)],
        compiler_params=_cparams(2),
        name="attention",
    )(qsrc, csrc, csrc, lsrc, lsrc, gg, *cast_arrays)
    return outs[0], tuple(outs[1:])


MIX_CHUNK = 128
SCONV_PAD = 8
CONF_PAD = 16


LANES = 128
SUBLANES = 8
N_LANE_GROUPS = SCONV_W // LANES


def _window_conv(p_ref, slot, w_ref, group, first, n_taps, ch):
    lanes = slice(group * LANES, (group + 1) * LANES)
    win = p_ref[slot, :, lanes]
    n_rows = win.shape[0]
    acc = None
    for phase in range(SUBLANES):
        taps = [k for k in range(n_taps) if (first + k) % SUBLANES == phase]
        if not taps:
            continue
        shifted = win if phase == 0 else pltpu.roll(win, n_rows - phase, 0)
        for k in taps:
            base = first + k - phase
            term = w_ref[k:k + 1, lanes] * shifted[base:base + ch]
            acc = term if acc is None else acc + term
    return acc


def _mix_kernel(sb_ref, sg_ref, sx_ref, ga_ref, gb_ref, sw_ref, dw_ref, db_ref, lng_ref, lnb_ref,
                pw_ref, pb_ref, ggs_ref, ggc_ref, o_ref, p1_ref, p2_ref, a_ref):
    seq = sb_ref.shape[0]
    ch = MIX_CHUNK
    n_chunks = seq // ch
    hs, hc = SCONV_PAD, CONF_PAD
    p1_ref[1, 0:hs, :] = jnp.zeros((hs, SCONV_W), F32)
    p1_ref[n_chunks, ch + hs:ch + 2 * hs, :] = jnp.zeros((hs, SCONV_W), F32)
    p2_ref[1, 0:hc, :] = jnp.zeros((hc, CONF_W), F32)
    p2_ref[n_chunks, ch + hc:ch + 2 * hc, :] = jnp.zeros((hc, CONF_W), F32)

    def fill(c, carry):
        r = pl.multiple_of(c * ch, ch)
        rows = pl.ds(r, ch)
        v = sg_ref[rows, :].astype(F32) * sx_ref[rows, :].astype(F32)
        p1_ref[c + 1, hs:hs + ch, :] = v
        p1_ref[c, ch + hs:ch + 2 * hs, :] = v[0:hs]
        p1_ref[c + 2, 0:hs, :] = v[ch - hs:ch]
        u = ga_ref[rows, :].astype(F32) * _sigmoid(gb_ref[rows, :].astype(F32))
        p2_ref[c + 1, hc:hc + ch, :] = u
        p2_ref[c, ch + hc:ch + 2 * hc, :] = u[0:hc]
        p2_ref[c + 2, 0:hc, :] = u[ch - hc:ch]
        return carry

    lax.fori_loop(0, n_chunks, fill, 0)

    def conv(c, carry):
        r = pl.multiple_of(c * ch, ch)
        rows = pl.ds(r, ch)
        acc = jnp.concatenate(
            [_window_conv(p1_ref, c + 1, sw_ref, g, hs - SCONV_K // 2, SCONV_K, ch) for g in range(N_LANE_GROUPS)],
            axis=1)
        ys = sb_ref[rows, :].astype(F32) * acc
        ms = jnp.mean(ys * ys, axis=-1, keepdims=True)
        o_ref[rows, 0:SCONV_W] = (ys * lax.rsqrt(ms + EPS) * ggs_ref[...]).astype(o_ref.dtype)

        u = db_ref[...] + jnp.concatenate(
            [_window_conv(p2_ref, c + 1, dw_ref, g, hc - CONF_K // 2, CONF_K, ch) for g in range(N_LANE_GROUPS)],
            axis=1)
        mu = jnp.mean(u, axis=-1, keepdims=True)
        uc = u - mu
        var = jnp.mean(uc * uc, axis=-1, keepdims=True)
        v = uc * lax.rsqrt(var + EPS) * lng_ref[...] + lnb_ref[...]
        a_ref[rows, :] = (v * _sigmoid(v)).astype(a_ref.dtype)
        return carry

    lax.fori_loop(0, n_chunks, conv, 0)

    yc = jnp.dot(a_ref[...], pw_ref[...], preferred_element_type=F32) + pb_ref[...]
    ms = jnp.mean(yc * yc, axis=-1, keepdims=True)
    o_ref[:, SCONV_W:] = (yc * lax.rsqrt(ms + EPS) * ggc_ref[...]).astype(o_ref.dtype)


def _local_mixers(src, sconv_w, conf_dw, conf_db, ln_g, ln_b, pw, pb, ggs, ggc, *, n_seq, seq):
    w = SCONV_W
    col = lambda k: pl.BlockSpec((seq, w), lambda b: (b, k))
    vec = lambda n: pl.BlockSpec((1, n), lambda b: (0, 0))
    return pl.pallas_call(
        _mix_kernel,
        grid=(n_seq,),
        in_specs=[col(3), col(4), col(5), col(6), col(7),
                  pl.BlockSpec((SCONV_K, w), lambda b: (0, 0)),
                  pl.BlockSpec((CONF_K, w), lambda b: (0, 0)),
                  vec(w), vec(w), vec(w),
                  pl.BlockSpec((w, w), lambda b: (0, 0)),
                  vec(w), vec(w), vec(w)],
        out_specs=pl.BlockSpec((seq, 2 * w), lambda b: (b, 0)),
        out_shape=jax.ShapeDtypeStruct((src.shape[0], 2 * w), BF16),
        scratch_shapes=[pltpu.VMEM((seq // MIX_CHUNK + 2, MIX_CHUNK + 2 * SCONV_PAD, w), F32),
                        pltpu.VMEM((seq // MIX_CHUNK + 2, MIX_CHUNK + 2 * CONF_PAD, w), F32),
                        pltpu.VMEM((seq, w), BF16)],
        compiler_params=_cparams(1),
        name="local_mixers",
    )(src, src, src, src, src, sconv_w, conf_dw, conf_db, ln_g, ln_b, pw, pb, ggs, ggc)


OUT_TN = 512


def _out_kernel(a_ref, m_ref, wa_ref, wm_ref, x_ref, gate_ref, o_ref):
    a = a_ref[...]
    mx = m_ref[...]
    for t in range(o_ref.shape[1] // OUT_TN):
        sl = slice(t * OUT_TN, (t + 1) * OUT_TN)
        acc = (jnp.dot(a, wa_ref[:, sl], preferred_element_type=F32)
               + jnp.dot(mx, wm_ref[:, sl], preferred_element_type=F32))
        o_ref[:, sl] = x_ref[:, sl] + gate_ref[0, :, sl] * acc


def _out_proj(att, mix, w, x2, mod3, mod_row, *, tm):
    m, d = x2.shape
    half = att.shape[1]
    return pl.pallas_call(
        _out_kernel,
        grid=(m // tm,),
        in_specs=[pl.BlockSpec((tm, half), lambda i: (i, 0)),
                  pl.BlockSpec((tm, half), lambda i: (i, 0)),
                  _const_spec((half, d), (0, 0)),
                  _const_spec((half, d), (1, 0)),
                  pl.BlockSpec((tm, d), lambda i: (i, 0)),
                  pl.BlockSpec((1, 1, d), lambda i: (mod_row(i), 0, 2))],
        out_specs=pl.BlockSpec((tm, d), lambda i: (i, 0)),
        out_shape=jax.ShapeDtypeStruct((m, d), F32),
        compiler_params=_cparams(1),
        name="out_proj",
    )(att, mix, w, w, x2, mod3)


MXU_WIDTH = 256


def _swiglu_accumulate(h, wg_ref, wu_ref, wd_refs, o_ref, assign=False):
    tm, tf = h.shape[0], wg_ref.shape[1]
    if tf % (2 * MXU_WIDTH) == 0:
        splits = [(slice(0, tm), slice(p * tf // 2, (p + 1) * tf // 2)) for p in range(2)]
    else:
        splits = [(slice(p * tm // 2, (p + 1) * tm // 2), slice(0, tf)) for p in range(2)]
    ups = [(jnp.dot(h[rows], wg_ref[:, cols], preferred_element_type=F32),
            jnp.dot(h[rows], wu_ref[:, cols], preferred_element_type=F32)) for rows, cols in splits]
    acts = [(gv * _sigmoid(gv) * uv).astype(BF16) for gv, uv in ups]
    same_rows = splits[0][0] == splits[1][0]
    col = 0
    for wd_ref in wd_refs:
        width = wd_ref.shape[1]
        downs = [jnp.dot(act, wd_ref[cols, :], preferred_element_type=F32) for act, (_, cols) in zip(acts, splits)]
        pieces = [(splits[0][0], downs[0] + downs[1])] if same_rows else list(zip((s[0] for s in splits), downs))
        for rows, part in pieces:
            if assign:
                o_ref[rows, col:col + width] = part
            else:
                o_ref[rows, col:col + width] += part
        col += width


def _ffn_kernel(x_ref, shift_ref, scale_ref, gate_ref, g_ref, wg_ref, wu_ref, wd_ref, *rest, n_cast):
    cast_in, o_ref, cast_out, h_ref = rest[:n_cast], rest[n_cast], rest[n_cast + 1:2 * n_cast + 1], rest[-1]
    j = pl.program_id(1)

    @pl.when(j == 0)
    def _():
        _cast_slabs(cast_in, cast_out)
        _norm_modulate_rows(x_ref, h_ref, g_ref[...], shift_ref[0], scale_ref[0], static=True)
        _swiglu_accumulate(h_ref[...], wg_ref, wu_ref, (wd_ref,), o_ref, assign=True)

    last = pl.num_programs(1) - 1

    @pl.when(jnp.logical_and(j > 0, j < last))
    def _():
        _cast_slabs(cast_in, cast_out)
        _swiglu_accumulate(h_ref[...], wg_ref, wu_ref, (wd_ref,), o_ref)

    @pl.when(j == last)
    def _():
        _cast_slabs(cast_in, cast_out)
        _swiglu_accumulate(h_ref[...], wg_ref, wu_ref, (wd_ref,), o_ref)
        o_ref[...] = x_ref[...] + gate_ref[0] * o_ref[...]


BF16_SUBLANE_TILE = 16


def _slab_rows(rows, n_lead, n_steps):
    for height in range(BF16_SUBLANE_TILE, rows + 1, BF16_SUBLANE_TILE):
        if rows % height == 0 and n_lead * (rows // height) <= n_steps:
            return height
    raise ValueError(f"no slab height for rows={rows}, n_lead={n_lead}, n_steps={n_steps}")


def _cast_plan(jobs, n_steps, step_of):
    arrays, in_specs, out_specs, out_shapes = [], [], [], []
    for arr, col_block, n_col_blocks in jobs:
        n_lead, rows, cols = arr.shape
        width = cols // n_col_blocks
        height = _slab_rows(rows, n_lead, n_steps)
        per_lead = rows // height
        last = n_lead * per_lead - 1

        def index(*grid_idx, per_lead=per_lead, last=last, col=0):
            s = jnp.minimum(step_of(*grid_idx), last)
            return (s // per_lead, s % per_lead, col)

        arrays.append(arr)
        in_specs.append(pl.BlockSpec((1, height, width), functools.partial(index, col=col_block)))
        out_specs.append(pl.BlockSpec((1, height, width), index))
        out_shapes.append(jax.ShapeDtypeStruct((n_lead, rows, width), BF16))
    return arrays, in_specs, out_specs, out_shapes


def _cast_slabs(cast_in, cast_out):
    for src, dst in zip(cast_in, cast_out):
        dst[...] = src[...].astype(dst.dtype)


def _dense_ffn(x2, mod3, mod_row, g, wg, wu, wd, *, tm, tf, cast=()):
    m, d = x2.shape
    f = wg.shape[2]
    nf = f // tf
    assert nf >= 2, "the first and the last hidden tile run different blocks"
    n_steps = (m // tm) * nf
    modspec = lambda k: pl.BlockSpec((1, 1, d), lambda i, j: (mod_row(i), 0, k))
    cast_arrays, cast_in, cast_out, cast_shapes = _cast_plan(cast, n_steps, lambda i, j: i * nf + j)
    outs = pl.pallas_call(
        functools.partial(_ffn_kernel, n_cast=len(cast)),
        grid=(m // tm, nf),
        in_specs=[pl.BlockSpec((tm, d), lambda i, j: (i, 0)),
                  modspec(3), modspec(4), modspec(5),
                  pl.BlockSpec((1, d), lambda i, j: (0, 0)),
                  pl.BlockSpec((None, d, tf), lambda i, j: (0, 0, j)),
                  pl.BlockSpec((None, d, tf), lambda i, j: (0, 0, j)),
                  pl.BlockSpec((None, tf, d), lambda i, j: (0, j, 0))] + cast_in,
        out_specs=[pl.BlockSpec((tm, d), lambda i, j: (i, 0))] + cast_out,
        out_shape=[jax.ShapeDtypeStruct((m, d), F32)] + cast_shapes,
        scratch_shapes=[pltpu.VMEM((tm, d), BF16)],
        compiler_params=_cparams(2),
        name="dense_ffn",
    )(x2, mod3, mod3, mod3, g, wg, wu, wd, *cast_arrays)
    return outs[0], tuple(outs[1:])


ROUTER_ROWS = 16


U32 = jnp.uint32


def _pack_bf16_pair(lo, hi):
    return (pltpu.bitcast(lo, U32) >> 16) | pltpu.bitcast(hi, U32)


def _unpack_bf16_pair(packed):
    lo = pltpu.bitcast(packed << 16, F32)
    hi = pltpu.bitcast(packed & jnp.uint32(0xFFFF0000), F32)
    return lo.astype(BF16), hi.astype(BF16)


def _router_kernel(x_ref, shift_ref, scale_ref, g_ref, rw_ref, tp_ref, idx_ref, gate_ref, cnt_ref,
                   t_ref, tri_ref, base_ref):
    tm = x_ref.shape[0]

    @pl.when(pl.program_id(0) == 0)
    def _():
        r = lax.broadcasted_iota(jnp.int32, (tm, tm), 0)
        c = lax.broadcasted_iota(jnp.int32, (tm, tm), 1)
        tri_ref[...] = jnp.where(r < c, 1.0, 0.0).astype(BF16)
        base_ref[...] = jnp.zeros(base_ref.shape, F32)

    _norm_modulate_rows(x_ref, t_ref, g_ref[...], shift_ref[0], scale_ref[0], static=True)

    t = t_ref[...]
    t_hi = t.astype(BF16)
    t_hi32 = t_hi.astype(F32)
    t_lo = (t - t_hi32).astype(BF16)
    half = t.shape[1] // 2
    tp_ref[...] = _pack_bf16_pair(t_hi32[:, :half], t_hi32[:, half:])
    w = rw_ref[...]
    w_hi = w.astype(BF16)
    w_lo = (w - w_hi.astype(F32)).astype(BF16)
    dn = (((1,), (1,)), ((), ()))
    logits = (lax.dot_general(w_hi, t_hi, dn, preferred_element_type=F32)
              + lax.dot_general(w_lo, t_hi, dn, preferred_element_type=F32)
              + lax.dot_general(w_hi, t_lo, dn, preferred_element_type=F32))

    e = lax.broadcasted_iota(jnp.int32, (ROUTER_ROWS, tm), 0).astype(F32)
    neg = jnp.float32(-jnp.inf)
    lg = jnp.where(e < N_EXPERTS, logits, neg)
    m1 = jnp.max(lg, axis=0, keepdims=True)
    i1 = jnp.min(jnp.where(lg == m1, e, float(ROUTER_ROWS)), axis=0, keepdims=True)
    lg2 = jnp.where(e == i1, neg, lg)
    m2 = jnp.max(lg2, axis=0, keepdims=True)
    i2 = jnp.min(jnp.where(lg2 == m2, e, float(ROUTER_ROWS)), axis=0, keepdims=True)
    ex = jnp.exp(m2 - m1)
    den = 1.0 + ex
    gate_ref[0:1, :] = 1.0 / den
    gate_ref[1:2, :] = ex / den

    hit1 = e == i1
    hit2 = e == i2
    onehot = jnp.where(hit1 | hit2, 1.0, 0.0)
    prefix = jnp.dot(onehot.astype(BF16), tri_ref[...], preferred_element_type=F32) + base_ref[:, 0:1]
    r1 = jnp.sum(jnp.where(hit1, prefix, 0.0), axis=0, keepdims=True)
    r2 = jnp.sum(jnp.where(hit2, prefix, 0.0), axis=0, keepdims=True)
    idx_ref[0:1, :] = i1.astype(jnp.int32)
    idx_ref[1:2, :] = i2.astype(jnp.int32)
    idx_ref[2:3, :] = r1.astype(jnp.int32)
    idx_ref[3:4, :] = r2.astype(jnp.int32)
    base_ref[...] = base_ref[...] + jnp.sum(onehot, axis=1, keepdims=True)
    cnt_ref[...] = base_ref[...].astype(jnp.int32)


def _router(x2, mod3, mod_row, g, rw16, *, tm):
    m, d = x2.shape
    modspec = lambda k: pl.BlockSpec((1, 1, d), lambda i: (mod_row(i), 0, k))
    return pl.pallas_call(
        _router_kernel,
        grid=(m // tm,),
        in_specs=[pl.BlockSpec((tm, d), lambda i: (i, 0)),
                  modspec(3), modspec(4),
                  pl.BlockSpec((1, d), lambda i: (0, 0)),
                  pl.BlockSpec((ROUTER_ROWS, d), lambda i: (0, 0))],
        out_specs=[pl.BlockSpec((tm, d // 2), lambda i: (i, 0)),
                   pl.BlockSpec((4, tm), lambda i: (0, i)),
                   pl.BlockSpec((2, tm), lambda i: (0, i)),
                   pl.BlockSpec((ROUTER_ROWS, 128), lambda i: (0, 0))],
        out_shape=[jax.ShapeDtypeStruct((m, d // 2), U32),
                   jax.ShapeDtypeStruct((4, m), jnp.int32),
                   jax.ShapeDtypeStruct((2, m), F32),
                   jax.ShapeDtypeStruct((ROUTER_ROWS, 128), jnp.int32)],
        scratch_shapes=[pltpu.VMEM((tm, d), F32), pltpu.VMEM((tm, tm), BF16),
                        pltpu.VMEM((ROUTER_ROWS, 128), F32)],
        compiler_params=_cparams(1),
        name="router",
    )(x2, mod3, mod3, g, rw16)


def _row_copy(src_ref, src_row, dst_ref, dst_row, sem):
    return pltpu.make_async_copy(src_ref.at[pl.ds(src_row, 1), :], dst_ref.at[pl.ds(dst_row, 1), :], sem)


DMA_ISSUE_UNROLL = 8


def _scatter_kernel(dest_ref, pad_ref, t_ref, buf_ref, zero_ref, sem, zsem, *, n_tok, n_tiles):
    tm = t_ref.shape[0]
    base = pl.program_id(0) * tm

    @pl.when(pl.program_id(0) == 0)
    def _():
        zero_ref[...] = jnp.zeros(zero_ref.shape, zero_ref.dtype)
        for e in range(N_EXPERTS):
            start, length = pad_ref[e], pad_ref[N_EXPERTS + e]

            def fill(r, carry, start=start):
                _row_copy(zero_ref, 0, buf_ref, start + r, zsem).start()
                return carry

            def drain(r, carry):
                _row_copy(zero_ref, 0, buf_ref, 0, zsem).wait()
                return carry

            lax.fori_loop(0, length, fill, 0)
            lax.fori_loop(0, length, drain, 0)

        def tile_copy(tile):
            row0 = pl.multiple_of(tile * EXPERT_TM, EXPERT_TM)
            return pltpu.make_async_copy(zero_ref, buf_ref.at[pl.ds(row0, EXPERT_TM), :], zsem)

        def fill_tile(tile, carry):
            tile_copy(tile).start()
            return carry

        def drain_tile(tile, carry):
            tile_copy(tile).wait()
            return carry

        lax.fori_loop(pad_ref[2 * N_EXPERTS], n_tiles, fill_tile, 0)
        lax.fori_loop(pad_ref[2 * N_EXPERTS], n_tiles, drain_tile, 0)

    def start(r, carry):
        for k in range(2):
            _row_copy(t_ref, r, buf_ref, dest_ref[k * n_tok + base + r], sem).start()
        return carry

    lax.fori_loop(0, tm, start, 0, unroll=DMA_ISSUE_UNROLL)
    for k in range(2):
        pltpu.make_async_copy(t_ref, buf_ref.at[pl.ds(0, tm), :], sem).wait()


def _scatter_rows(dest, pad_info, t, *, tm, n_tiles):
    m, d = t.shape
    grid_spec = pltpu.PrefetchScalarGridSpec(
        num_scalar_prefetch=2,
        grid=(m // tm,),
        in_specs=[pl.BlockSpec((tm, d), lambda i, dest, pad: (i, 0))],
        out_specs=pl.BlockSpec(memory_space=pl.ANY),
        scratch_shapes=[pltpu.VMEM((EXPERT_TM, d), t.dtype),
                        pltpu.SemaphoreType.DMA(()), pltpu.SemaphoreType.DMA(())],
    )
    return pl.pallas_call(
        functools.partial(_scatter_kernel, n_tok=m, n_tiles=n_tiles),
        grid_spec=grid_spec,
        out_shape=jax.ShapeDtypeStruct((n_tiles * EXPERT_TM, d), t.dtype),
        compiler_params=_cparams(1),
        name="moe_scatter",
    )(dest, pad_info, t)


N_WEIGHT_SLOTS = 3


def _expert_kernel(te_ref, tv_ref, tx_ref, x_ref, *rest, n_wd, tf, nf):
    del tx_ref
    w_hbm = rest[:2 + n_wd]
    o_ref, xb_ref = rest[2 + n_wd], rest[3 + n_wd]
    w_buf = rest[4 + n_wd:6 + 2 * n_wd]
    sem = rest[-1]
    i = pl.program_id(0)
    n = pl.num_programs(0)

    def copies(tile, j, slot):
        e = te_ref[tile]
        cols = pl.ds(j * tf, tf)
        srcs = [w_hbm[0].at[e, :, cols], w_hbm[1].at[e, :, cols]] + [w.at[e, cols, :] for w in w_hbm[2:]]
        return [pltpu.make_async_copy(src, buf.at[slot], sem.at[slot, k])
                for k, (src, buf) in enumerate(zip(srcs, w_buf))]

    valid = tv_ref[i] == 1
    nxt = jnp.minimum(i + 1, n - 1)
    next_live = jnp.logical_and(i + 1 < n, tv_ref[nxt] == 1)

    @pl.when(jnp.logical_not(valid))
    def _():
        o_ref[...] = jnp.zeros(o_ref.shape, o_ref.dtype)

    @pl.when(jnp.logical_and(valid, i == 0))
    def _():
        for c in copies(i, 0, 0):
            c.start()

    @pl.when(valid)
    def _():
        half = x_ref.shape[1]
        xb_ref[:, :half], xb_ref[:, half:] = _unpack_bf16_pair(x_ref[...])
        xb = xb_ref[...]
        for j in range(nf):
            slot = j % N_WEIGHT_SLOTS
            ahead = copies(i, j + 1, (j + 1) % N_WEIGHT_SLOTS) if j + 1 < nf else copies(nxt, 0, 0)
            for c in ahead:
                c.start()
            for c in copies(i, j, slot):
                c.wait()
            _swiglu_accumulate(xb, w_buf[0].at[slot], w_buf[1].at[slot], [b.at[slot] for b in w_buf[2:]],
                               o_ref, assign=j == 0)

    @pl.when(jnp.logical_and(valid, jnp.logical_not(next_live)))
    def _():
        for c in copies(nxt, 0, 0):
            c.wait()


def _experts(tile_e, tile_v, tile_x, buf, wg, wu, wds, *, tf):
    rows, d_packed = buf.shape
    d = wg.shape[1]
    f = wg.shape[2]
    nf = f // tf
    assert nf % N_WEIGHT_SLOTS != 1, "slot 0 must be free while the last hidden tile computes"
    any_spec = pl.BlockSpec(memory_space=pl.ANY)
    grid_spec = pltpu.PrefetchScalarGridSpec(
        num_scalar_prefetch=3,
        grid=(rows // EXPERT_TM,),
        in_specs=[pl.BlockSpec((EXPERT_TM, d_packed), lambda i, te, tv, tx: (tx[i], 0))]
                 + [any_spec] * (2 + len(wds)),
        out_specs=pl.BlockSpec((EXPERT_TM, d), lambda i, te, tv, tx: (i, 0)),
        scratch_shapes=[pltpu.VMEM((EXPERT_TM, d), BF16),
                        pltpu.VMEM((N_WEIGHT_SLOTS, d, tf), BF16),
                        pltpu.VMEM((N_WEIGHT_SLOTS, d, tf), BF16)]
                       + [pltpu.VMEM((N_WEIGHT_SLOTS, tf, wd.shape[2]), BF16) for wd in wds]
                       + [pltpu.SemaphoreType.DMA((N_WEIGHT_SLOTS, 2 + len(wds)))],
    )
    return pl.pallas_call(
        functools.partial(_expert_kernel, n_wd=len(wds), tf=tf, nf=nf),
        grid_spec=grid_spec,
        out_shape=jax.ShapeDtypeStruct((rows, d), F32),
        compiler_params=_cparams(1),
        name="moe_experts",
    )(tile_e, tile_v, tile_x, buf, wg, wu, *wds)


def _combine_kernel(dest_ref, x_ref, gate5_ref, gates_ref, gf_ref, ybuf_ref, o_ref, rows_ref, sem, *, n_tok):
    tm = x_ref.shape[0]
    i = pl.program_id(0)

    def issue(tile, slot):
        base = tile * tm

        def start(r, carry):
            for k in range(2):
                _row_copy(ybuf_ref, dest_ref[k * n_tok + base + r], rows_ref.at[slot, k], r, sem.at[slot]).start()
            return carry

        lax.fori_loop(0, tm, start, 0, unroll=DMA_ISSUE_UNROLL)

    @pl.when(i == 0)
    def _():
        issue(0, 0)

    @pl.when(i + 1 < pl.num_programs(0))
    def _():
        issue(i + 1, (i + 1) % 2)

    slot = i % 2
    for k in range(2):
        pltpu.make_async_copy(ybuf_ref.at[pl.ds(0, tm), :], rows_ref.at[slot, k], sem.at[slot]).wait()

    eye = (lax.broadcasted_iota(jnp.int32, (tm, tm), 0) == lax.broadcasted_iota(jnp.int32, (tm, tm), 1))
    g0 = jnp.sum(jnp.where(eye, gates_ref[0:1, :], 0.0), axis=1, keepdims=True)
    g1 = jnp.sum(jnp.where(eye, gates_ref[1:2, :], 0.0), axis=1, keepdims=True)
    y = g0 * rows_ref[slot, 0] + g1 * rows_ref[slot, 1]
    xn = x_ref[...] + gate5_ref[0] * y
    ms = jnp.mean(xn * xn, axis=-1, keepdims=True)
    o_ref[...] = xn * lax.rsqrt(ms + EPS) * gf_ref[...]


def _combine(dest, x2, mod3, mod_row, gates, g_final, ybuf, *, tm):
    m, d = x2.shape
    grid_spec = pltpu.PrefetchScalarGridSpec(
        num_scalar_prefetch=1,
        grid=(m // tm,),
        in_specs=[pl.BlockSpec((tm, d), lambda i, dest: (i, 0)),
                  pl.BlockSpec((1, 1, d), lambda i, dest: (mod_row(i), 0, 5)),
                  pl.BlockSpec((2, tm), lambda i, dest: (0, i)),
                  pl.BlockSpec((1, d), lambda i, dest: (0, 0)),
                  pl.BlockSpec(memory_space=pl.ANY)],
        out_specs=pl.BlockSpec((tm, d), lambda i, dest: (i, 0)),
        scratch_shapes=[pltpu.VMEM((2, 2, tm, d), F32), pltpu.SemaphoreType.DMA((2,))],
    )
    return pl.pallas_call(
        functools.partial(_combine_kernel, n_tok=m),
        grid_spec=grid_spec,
        out_shape=jax.ShapeDtypeStruct((m, d), F32),
        compiler_params=_cparams(1),
        name="moe_combine",
    )(dest, x2, mod3, gates, g_final, ybuf)


def _rope_tables(seq):
    rows = seq // GRID_W
    row = jnp.repeat(jnp.arange(rows), GRID_W).astype(F32)
    col = jnp.tile(jnp.arange(GRID_W), rows).astype(F32)
    axis_dim = HEAD_DIM // 2
    inv_freq = ROPE_THETA ** (-jnp.arange(0, axis_dim, 2, dtype=F32) / axis_dim)
    ang_r = row[:, None] * inv_freq
    ang_c = col[:, None] * inv_freq
    cr, sr, cc, sc = jnp.cos(ang_r), jnp.sin(ang_r), jnp.cos(ang_c), jnp.sin(ang_c)
    return (jnp.concatenate([cr, cr, cc, cc], axis=-1),
            jnp.concatenate([-sr, sr, -sc, sc], axis=-1))


def kernel(x, c, ctx, c_ctx, w_ada, b_ada, g_mix, w_in, g_q, g_k, sconv_w, conf_dw, conf_db, conf_ln_g,
           conf_ln_b, conf_pw, conf_pb, g_group, w_o, g_ffn, dense_wg, dense_wu, dense_wd, router_w,
           moe_wg, moe_wu, moe_wd, g_final):
    b, s, d = x.shape
    n_ctx = ctx.shape[1]
    depth = w_ada.shape[0]
    assert b + 1 <= MOD_ROWS and depth == 2
    m_lat, m_ctx = b * s, b * n_ctx
    ctx_row = b

    cin = jnp.concatenate([c, c_ctx[None, :], jnp.zeros((MOD_ROWS - b - 1, d), F32)], axis=0)
    mod = _ada(cin, w_ada, b_ada)
    cos_t, sin_t = _rope_tables(s)
    row2 = lambda v: v.reshape(1, -1)

    lat_tm = ROW_TM
    lat_row_for = lambda tm: (lambda i: i // (s // tm))
    lat_row = lat_row_for(lat_tm)
    ctx_tm = ROW_TM
    ctx_mod_row = lambda i: ctx_row

    x2 = x.reshape(m_lat, d)
    xc2 = ctx.reshape(m_ctx, d)
    out = None
    moe_wd_b = ()
    next_proj_b = ()
    for l in range(depth):
        last = l == depth - 1
        mod3 = mod[l].reshape(MOD_ROWS, 1, 6 * d)
        if next_proj_b:
            w_in_b, w_o_b = (w.reshape(w.shape[1:]) for w in next_proj_b)
            next_proj_b = ()
        else:
            w_in_b, w_o_b = w_in[l].astype(BF16), w_o[l].astype(BF16)
        pw_b = conf_pw[l].astype(BF16)
        gg = g_group[l]
        gga, ggs, ggc = row2(gg[:ATTN_W]), row2(gg[ATTN_W:ATTN_W + SCONV_W]), row2(gg[ATTN_W + SCONV_W:])
        mixer_args = (sconv_w[l], conf_dw[l], row2(conf_db[l]), row2(conf_ln_g[l]), row2(conf_ln_b[l]),
                      pw_b, row2(conf_pb[l]), ggs, ggc)
        in_args = (row2(g_mix[l]), w_in_b, row2(g_q[l]), row2(g_k[l]), cos_t, sin_t)

        dense_here = l % 2 == 0
        k_dense = l // 2
        cast = (tuple((w[k_dense:k_dense + 1], 0, 1) for w in (dense_wg, dense_wu, dense_wd))
                if dense_here else ())
        k_moe = (l + 1) // 2 if dense_here else l // 2
        moe_wd_job = ((moe_wd[k_moe], l % 2, 2),) if k_moe < moe_wd.shape[0] else ()
        p_lat, dense_w_b = _in_proj(x2, mod3, lat_row, *in_args, tm=lat_tm, rope=True, j0=0, nj=8, seq=s,
                                    cast=cast)
        if last:
            p_ctx, _ = _in_proj(xc2, mod3, ctx_mod_row, *in_args, tm=ctx_tm, rope=False, j0=2, nj=1, seq=ctx_tm)
            kc_blk = 0
        else:
            p_ctx, _ = _in_proj(xc2, mod3, ctx_mod_row, *in_args, tm=ctx_tm, rope=False, j0=0, nj=8, seq=ctx_tm)
            kc_blk = ATTN_W // KV_W
        att, moe_wd_half = _attention(p_lat, p_ctx, p_lat, gga, n_batch=b, q_len=s, tq=ATTN_TQ, kc_blk=kc_blk,
                                      has_latent=True, n_ctx=n_ctx, seq=s, cast=moe_wd_job)
        moe_wd_b = moe_wd_b + moe_wd_half
        mix = _local_mixers(p_lat, *mixer_args, n_seq=b, seq=s)
        x2 = _out_proj(att, mix, w_o_b, x2, mod3, lat_row, tm=lat_tm)
        if not last:
            att_c, _ = _attention(p_ctx, p_ctx, p_ctx, gga, n_batch=b, q_len=n_ctx, tq=n_ctx, kc_blk=kc_blk,
                                  has_latent=False, n_ctx=n_ctx, seq=n_ctx)
            mix_c = _local_mixers(p_ctx, *mixer_args, n_seq=b, seq=n_ctx)
            xc2 = _out_proj(att_c, mix_c, w_o_b, xc2, mod3, ctx_mod_row, tm=ctx_tm)

        if l % 2 == 0:
            wg_b, wu_b, wd_b = dense_w_b
            ffn = functools.partial(_dense_ffn, g=row2(g_ffn[l]), wg=wg_b, wu=wu_b, wd=wd_b,
                                    tf=FFN_TF)
            routed_next = l + 1 < depth and (l + 1) % 2 == 1
            cast = ((moe_wg[(l + 1) // 2], 0, 1), (moe_wu[(l + 1) // 2], 0, 1)) if routed_next else ()
            x2, moe_up_b = ffn(x2, mod3, lat_row, tm=lat_tm, cast=cast)
            if not last:
                proj_jobs = ((w_in[l + 1:l + 2], 0, 1), (w_o[l + 1:l + 2], 0, 1))
                xc2, next_proj_b = ffn(xc2, mod3, ctx_mod_row, tm=ctx_tm, cast=proj_jobs)
        else:
            assert last, "routed layer is implemented for the final layer (latent tokens only)"
            rw16 = jnp.zeros((ROUTER_ROWS, d), F32).at[:N_EXPERTS].set(router_w[l // 2].T)
            t, idx, gates, cnt = _router(x2, mod3, lat_row, row2(g_ffn[l]), rw16, tm=lat_tm)
            counts = cnt[:N_EXPERTS, 0]
            padded = (counts + EXPERT_TM - 1) // EXPERT_TM * EXPERT_TM
            pends = jnp.cumsum(padded)
            pstarts = pends - padded
            slot_e = idx[0:2]
            slot_start = sum(jnp.where(slot_e == e, pstarts[e], 0) for e in range(N_EXPERTS))
            dest = (slot_start + idx[2:4]).reshape(-1).astype(jnp.int32)
            n_tiles = (2 * m_lat) // EXPERT_TM + N_EXPERTS
            tile_start = jnp.arange(n_tiles, dtype=jnp.int32) * EXPERT_TM
            tile_v = (tile_start < pends[-1]).astype(jnp.int32)
            last_tile = pends[-1] // EXPERT_TM - 1
            tile_x = jnp.minimum(jnp.arange(n_tiles, dtype=jnp.int32), last_tile).astype(jnp.int32)
            tile_e = jnp.minimum(jnp.sum((tile_x * EXPERT_TM)[:, None] >= pends[None, :], axis=1),
                                 N_EXPERTS - 1).astype(jnp.int32)
            pad_info = jnp.concatenate([pstarts + counts, padded - counts,
                                        (pends[-1:] // EXPERT_TM)]).astype(jnp.int32)
            buf = _scatter_rows(dest, pad_info, t, tm=lat_tm, n_tiles=n_tiles)
            ybuf = _experts(tile_e, tile_v, tile_x, buf, *moe_up_b, moe_wd_b, tf=FFN_TF)
            out = _combine(dest, x2, mod3, lat_row_for(COMBINE_TM), gates, row2(g_final), ybuf, tm=COMBINE_TM)
    return out.reshape(b, s, d)
```

```python
import functools

import jax
import jax.numpy as jnp
from jax import lax
from jax.experimental import pallas as pl
from jax.experimental.pallas import tpu as pltpu

F32 = jnp.float32
BF16 = jnp.bfloat16

GRID_W = 64
HEAD_DIM = 128
N_Q_HEADS = 8
N_KV_HEADS = 2
GQA_GROUP = N_Q_HEADS // N_KV_HEADS
ATTN_W = N_Q_HEADS * HEAD_DIM
KV_W = N_KV_HEADS * HEAD_DIM
ROPE_THETA = 10000.0
SCONV_W = 512
SCONV_K = 3
CONF_W = 512
CONF_K = 31
N_EXPERTS = 8
EPS = 1e-6
LOG2E = 1.4426950408889634

V7X_VMEM_LIMIT_BYTES = 56 * 1024 * 1024
MOD_ROWS = 16
IN_TN = 512
EXPERT_TM = 512
ROW_TM = 512
FFN_TF = 512
ATTN_TQ = 512
COMBINE_TM = 256
ADA_TN = 1536


def _cparams(n_axes):
    return pltpu.CompilerParams(dimension_semantics=("arbitrary",) * n_axes,
                                vmem_limit_bytes=V7X_VMEM_LIMIT_BYTES)


def _sigmoid(x):
    return 1.0 / (1.0 + jnp.exp(-x))


NORM_CHUNK = 16
NORM_UNROLL = 4


def _norm_modulate_rows(x_ref, h_ref, g, shift, scale, static=False):
    rows = x_ref.shape[0]
    gain = g * (1.0 + scale)

    def body(c, carry):
        r = c * NORM_CHUNK if static else pl.multiple_of(c * NORM_CHUNK, NORM_CHUNK)
        x = x_ref[pl.ds(r, NORM_CHUNK), :]
        inv = lax.rsqrt(jnp.mean(x * x, axis=-1, keepdims=True) + EPS)
        h_ref[pl.ds(r, NORM_CHUNK), :] = (x * inv * gain + shift).astype(h_ref.dtype)
        return carry

    if static:
        for c in range(rows // NORM_CHUNK):
            body(c, 0)
        return

    lax.fori_loop(0, rows // NORM_CHUNK, body, 0, unroll=NORM_UNROLL)


def _ada_kernel(c_ref, w_ref, b_ref, o_ref):
    a = c_ref[...]
    a = (a * _sigmoid(a)).astype(BF16)
    o_ref[0] = jnp.dot(a, w_ref[0].astype(BF16), preferred_element_type=F32) + b_ref[0]


def _ada(cin, w_ada, b_ada):
    depth, d, n = w_ada.shape
    tn = ADA_TN
    return pl.pallas_call(
        _ada_kernel,
        grid=(depth, n // tn),
        in_specs=[pl.BlockSpec((MOD_ROWS, d), lambda l, j: (0, 0)),
                  pl.BlockSpec((1, d, tn), lambda l, j: (l, 0, j)),
                  pl.BlockSpec((1, 1, tn), lambda l, j: (l, 0, j))],
        out_specs=pl.BlockSpec((1, MOD_ROWS, tn), lambda l, j: (l, 0, j)),
        out_shape=jax.ShapeDtypeStruct((depth, MOD_ROWS, n), F32),
        compiler_params=_cparams(2),
        name="ada",
    )(cin, w_ada, b_ada.reshape(depth, 1, n))


def _head_norm(seg, gain):
    ms = jnp.mean(seg * seg, axis=-1, keepdims=True)
    return seg * lax.rsqrt(ms + EPS) * gain


def _rope(n, cos_t, sin_t):
    lane = lax.broadcasted_iota(jnp.int32, n.shape, 1)
    fwd = pltpu.roll(n, 32, 1)
    bwd = pltpu.roll(n, 96, 1)
    partner = jnp.where((lane // 32) % 2 == 0, bwd, fwd)
    return n * cos_t + partner * sin_t


def _in_kernel(x_ref, shift_ref, scale_ref, g_ref, w_ref, gq_ref, gk_ref, cos_ref, sin_ref,
               *rest, rope, j0, nj, q_scale, n_cast):
    cast_in, o_ref, cast_out, h_ref = rest[:n_cast], rest[n_cast], rest[n_cast + 1:2 * n_cast + 1], rest[-1]
    _norm_modulate_rows(x_ref, h_ref, g_ref[...], shift_ref[0], scale_ref[0], static=True)
    _cast_slabs(cast_in, cast_out)
    h = h_ref[...]

    def head(seg, gain, scale):
        n = _head_norm(seg, gain)
        if rope:
            n = _rope(n, cos_ref[...], sin_ref[...])
        if scale != 1.0:
            n = n * scale
        return n

    for t in range(nj):
        j = t + j0
        c0 = t * IN_TN
        acc = jnp.dot(h, w_ref[:, c0:c0 + IN_TN], preferred_element_type=F32)
        if j < 2:
            for hd in range(IN_TN // HEAD_DIM):
                sl = slice(hd * HEAD_DIM, (hd + 1) * HEAD_DIM)
                osl = slice(c0 + hd * HEAD_DIM, c0 + (hd + 1) * HEAD_DIM)
                o_ref[:, osl] = head(acc[:, sl], gq_ref[...], q_scale).astype(o_ref.dtype)
        elif j == 2:
            for hd in range(N_KV_HEADS):
                sl = slice(hd * HEAD_DIM, (hd + 1) * HEAD_DIM)
                osl = slice(c0 + hd * HEAD_DIM, c0 + (hd + 1) * HEAD_DIM)
                o_ref[:, osl] = head(acc[:, sl], gk_ref[...], 1.0).astype(o_ref.dtype)
            o_ref[:, c0 + KV_W:c0 + IN_TN] = acc[:, KV_W:].astype(o_ref.dtype)
        else:
            o_ref[:, c0:c0 + IN_TN] = acc.astype(o_ref.dtype)


def _const_spec(shape, idx):
    return pl.BlockSpec(shape, lambda i: idx, pipeline_mode=pl.Buffered(1))


def _in_proj(x2, mod3, mod_row, g, w, gq, gk, cos_t, sin_t, *, tm, rope, j0, nj, seq, cast=()):
    m, d = x2.shape
    tiles_per_seq = seq // tm
    ncols = nj * IN_TN
    assert j0 % nj == 0
    q_scale = HEAD_DIM ** -0.5 * LOG2E
    kern = functools.partial(_in_kernel, rope=rope, j0=j0, nj=nj, q_scale=q_scale, n_cast=len(cast))
    cast_arrays, cast_in, cast_out, cast_shapes = _cast_plan(cast, m // tm, lambda i: i)
    outs = pl.pallas_call(
        kern,
        grid=(m // tm,),
        in_specs=[pl.BlockSpec((tm, d), lambda i: (i, 0)),
                  pl.BlockSpec((1, 1, d), lambda i: (mod_row(i), 0, 0)),
                  pl.BlockSpec((1, 1, d), lambda i: (mod_row(i), 0, 1)),
                  _const_spec((1, d), (0, 0)),
                  _const_spec((d, ncols), (0, j0 // nj)),
                  _const_spec((1, HEAD_DIM), (0, 0)),
                  _const_spec((1, HEAD_DIM), (0, 0)),
                  pl.BlockSpec((tm, HEAD_DIM), lambda i: (i % tiles_per_seq, 0)),
                  pl.BlockSpec((tm, HEAD_DIM), lambda i: (i % tiles_per_seq, 0))] + cast_in,
        out_specs=[pl.BlockSpec((tm, ncols), lambda i: (i, 0))] + cast_out,
        out_shape=[jax.ShapeDtypeStruct((m, ncols), BF16)] + cast_shapes,
        scratch_shapes=[pltpu.VMEM((tm, d), BF16)],
        compiler_params=_cparams(1),
        name="in_proj",
    )(x2, mod3, mod3, g, w, gq, gk, cos_t, sin_t, *cast_arrays)
    return outs[0], tuple(outs[1:])


def _attn_kernel(q_ref, kc_ref, vc_ref, kl_ref, vl_ref, gg_ref, *rest, has_latent, n_cast):
    cast_in, o_ref, cast_out, acc_ref = rest[:n_cast], rest[n_cast], rest[n_cast + 1:2 * n_cast + 1], rest[-1]
    _cast_slabs(cast_in, cast_out)
    tq = q_ref.shape[0]
    dn = (((1,), (1,)), ((), ()))
    ssq = jnp.zeros((tq, 1), F32)
    for hd in range(N_Q_HEADS):
        kv = hd // GQA_GROUP
        ksl = slice(kv * HEAD_DIM, (kv + 1) * HEAD_DIM)
        q = q_ref[:, hd * HEAD_DIM:(hd + 1) * HEAD_DIM]
        s_c = lax.dot_general(q, kc_ref[:, ksl], dn, preferred_element_type=F32)
        mx = jnp.max(s_c, axis=-1, keepdims=True)
        if has_latent:
            s_l = lax.dot_general(q, kl_ref[:, ksl], dn, preferred_element_type=F32)
            mx = jnp.maximum(mx, jnp.max(s_l, axis=-1, keepdims=True))
        p_c = jnp.exp2(s_c - mx)
        den = jnp.sum(p_c, axis=-1, keepdims=True)
        o = jnp.dot(p_c.astype(BF16), vc_ref[:, ksl], preferred_element_type=F32)
        if has_latent:
            p_l = jnp.exp2(s_l - mx)
            den = den + jnp.sum(p_l, axis=-1, keepdims=True)
            o = o + jnp.dot(p_l.astype(BF16), vl_ref[:, ksl], preferred_element_type=F32)
        o = o * (1.0 / den)
        ssq = ssq + jnp.sum(o * o, axis=-1, keepdims=True)
        acc_ref[:, hd * HEAD_DIM:(hd + 1) * HEAD_DIM] = o
    inv = lax.rsqrt(ssq * (1.0 / ATTN_W) + EPS)
    o_ref[...] = (acc_ref[...] * inv * gg_ref[...]).astype(o_ref.dtype)


def _attention(qsrc, csrc, lsrc, gg, *, n_batch, q_len, tq, kc_blk, has_latent, n_ctx, seq, cast=()):
    m = qsrc.shape[0]
    tiles = q_len // tq
    kern = functools.partial(_attn_kernel, has_latent=has_latent, n_cast=len(cast))
    cast_arrays, cast_in, cast_out, cast_shapes = _cast_plan(cast, n_batch * tiles, lambda b, i: b * tiles + i)
    outs = pl.pallas_call(
        kern,
        grid=(n_batch, tiles),
        in_specs=[pl.BlockSpec((tq, ATTN_W), lambda b, i: (b * tiles + i, 0)),
                  pl.BlockSpec((n_ctx, KV_W), lambda b, i: (b, kc_blk)),
                  pl.BlockSpec((n_ctx, KV_W), lambda b, i: (b, kc_blk + 1)),
                  pl.BlockSpec((seq, KV_W), lambda b, i: (b, ATTN_W // KV_W)),
                  pl.BlockSpec((seq, KV_W), lambda b, i: (b, ATTN_W // KV_W + 1)),
                  pl.BlockSpec((1, ATTN_W), lambda b, i: (0, 0))] + cast_in,
        out_specs=[pl.BlockSpec((tq, ATTN_W), lambda b, i: (b * tiles + i, 0))] + cast_out,
        out_shape=[jax.ShapeDtypeStruct((m, ATTN_W), BF16)] + cast_shapes,
        scratch_shapes=[pltpu.VMEM((tq, ATTN_W), F32)],
        compiler_params=_cparams(2),
        name="attention",
    )(qsrc, csrc, csrc, lsrc, lsrc, gg, *cast_arrays)
    return outs[0], tuple(outs[1:])


MIX_CHUNK = 128
SCONV_PAD = 8
CONF_PAD = 16


LANES = 128
SUBLANES = 8
N_LANE_GROUPS = SCONV_W // LANES


def _window_conv(p_ref, slot, w_ref, group, first, n_taps, ch):
    lanes = slice(group * LANES, (group + 1) * LANES)
    win = p_ref[slot, :, lanes]
    n_rows = win.shape[0]
    acc = None
    for phase in range(SUBLANES):
        taps = [k for k in range(n_taps) if (first + k) % SUBLANES == phase]
        if not taps:
            continue
        shifted = win if phase == 0 else pltpu.roll(win, n_rows - phase, 0)
        for k in taps:
            base = first + k - phase
            term = w_ref[k:k + 1, lanes] * shifted[base:base + ch]
            acc = term if acc is None else acc + term
    return acc


def _mix_kernel(sb_ref, sg_ref, sx_ref, ga_ref, gb_ref, sw_ref, dw_ref, db_ref, lng_ref, lnb_ref,
                pw_ref, pb_ref, ggs_ref, ggc_ref, o_ref, p1_ref, p2_ref, a_ref):
    seq = sb_ref.shape[0]
    ch = MIX_CHUNK
    n_chunks = seq // ch
    hs, hc = SCONV_PAD, CONF_PAD
    p1_ref[1, 0:hs, :] = jnp.zeros((hs, SCONV_W), F32)
    p1_ref[n_chunks, ch + hs:ch + 2 * hs, :] = jnp.zeros((hs, SCONV_W), F32)
    p2_ref[1, 0:hc, :] = jnp.zeros((hc, CONF_W), F32)
    p2_ref[n_chunks, ch + hc:ch + 2 * hc, :] = jnp.zeros((hc, CONF_W), F32)

    def fill(c, carry):
        r = pl.multiple_of(c * ch, ch)
        rows = pl.ds(r, ch)
        v = sg_ref[rows, :].astype(F32) * sx_ref[rows, :].astype(F32)
        p1_ref[c + 1, hs:hs + ch, :] = v
        p1_ref[c, ch + hs:ch + 2 * hs, :] = v[0:hs]
        p1_ref[c + 2, 0:hs, :] = v[ch - hs:ch]
        u = ga_ref[rows, :].astype(F32) * _sigmoid(gb_ref[rows, :].astype(F32))
        p2_ref[c + 1, hc:hc + ch, :] = u
        p2_ref[c, ch + hc:ch + 2 * hc, :] = u[0:hc]
        p2_ref[c + 2, 0:hc, :] = u[ch - hc:ch]
        return carry

    lax.fori_loop(0, n_chunks, fill, 0)

    def conv(c, carry):
        r = pl.multiple_of(c * ch, ch)
        rows = pl.ds(r, ch)
        acc = jnp.concatenate(
            [_window_conv(p1_ref, c + 1, sw_ref, g, hs - SCONV_K // 2, SCONV_K, ch) for g in range(N_LANE_GROUPS)],
            axis=1)
        ys = sb_ref[rows, :].astype(F32) * acc
        ms = jnp.mean(ys * ys, axis=-1, keepdims=True)
        o_ref[rows, 0:SCONV_W] = (ys * lax.rsqrt(ms + EPS) * ggs_ref[...]).astype(o_ref.dtype)

        u = db_ref[...] + jnp.concatenate(
            [_window_conv(p2_ref, c + 1, dw_ref, g, hc - CONF_K // 2, CONF_K, ch) for g in range(N_LANE_GROUPS)],
            axis=1)
        mu = jnp.mean(u, axis=-1, keepdims=True)
        uc = u - mu
        var = jnp.mean(uc * uc, axis=-1, keepdims=True)
        v = uc * lax.rsqrt(var + EPS) * lng_ref[...] + lnb_ref[...]
        a_ref[rows, :] = (v * _sigmoid(v)).astype(a_ref.dtype)
        return carry

    lax.fori_loop(0, n_chunks, conv, 0)

    yc = jnp.dot(a_ref[...], pw_ref[...], preferred_element_type=F32) + pb_ref[...]
    ms = jnp.mean(yc * yc, axis=-1, keepdims=True)
    o_ref[:, SCONV_W:] = (yc * lax.rsqrt(ms + EPS) * ggc_ref[...]).astype(o_ref.dtype)


def _local_mixers(src, sconv_w, conf_dw, conf_db, ln_g, ln_b, pw, pb, ggs, ggc, *, n_seq, seq):
    w = SCONV_W
    col = lambda k: pl.BlockSpec((seq, w), lambda b: (b, k))
    vec = lambda n: pl.BlockSpec((1, n), lambda b: (0, 0))
    return pl.pallas_call(
        _mix_kernel,
        grid=(n_seq,),
        in_specs=[col(3), col(4), col(5), col(6), col(7),
                  pl.BlockSpec((SCONV_K, w), lambda b: (0, 0)),
                  pl.BlockSpec((CONF_K, w), lambda b: (0, 0)),
                  vec(w), vec(w), vec(w),
                  pl.BlockSpec((w, w), lambda b: (0, 0)),
                  vec(w), vec(w), vec(w)],
        out_specs=pl.BlockSpec((seq, 2 * w), lambda b: (b, 0)),
        out_shape=jax.ShapeDtypeStruct((src.shape[0], 2 * w), BF16),
        scratch_shapes=[pltpu.VMEM((seq // MIX_CHUNK + 2, MIX_CHUNK + 2 * SCONV_PAD, w), F32),
                        pltpu.VMEM((seq // MIX_CHUNK + 2, MIX_CHUNK + 2 * CONF_PAD, w), F32),
                        pltpu.VMEM((seq, w), BF16)],
        compiler_params=_cparams(1),
        name="local_mixers",
    )(src, src, src, src, src, sconv_w, conf_dw, conf_db, ln_g, ln_b, pw, pb, ggs, ggc)


OUT_TN = 512


def _out_kernel(a_ref, m_ref, wa_ref, wm_ref, x_ref, gate_ref, o_ref):
    a = a_ref[...]
    mx = m_ref[...]
    for t in range(o_ref.shape[1] // OUT_TN):
        sl = slice(t * OUT_TN, (t + 1) * OUT_TN)
        acc = (jnp.dot(a, wa_ref[:, sl], preferred_element_type=F32)
               + jnp.dot(mx, wm_ref[:, sl], preferred_element_type=F32))
        o_ref[:, sl] = x_ref[:, sl] + gate_ref[0, :, sl] * acc


def _out_proj(att, mix, w, x2, mod3, mod_row, *, tm):
    m, d = x2.shape
    half = att.shape[1]
    return pl.pallas_call(
        _out_kernel,
        grid=(m // tm,),
        in_specs=[pl.BlockSpec((tm, half), lambda i: (i, 0)),
                  pl.BlockSpec((tm, half), lambda i: (i, 0)),
                  _const_spec((half, d), (0, 0)),
                  _const_spec((half, d), (1, 0)),
                  pl.BlockSpec((tm, d), lambda i: (i, 0)),
                  pl.BlockSpec((1, 1, d), lambda i: (mod_row(i), 0, 2))],
        out_specs=pl.BlockSpec((tm, d), lambda i: (i, 0)),
        out_shape=jax.ShapeDtypeStruct((m, d), F32),
        compiler_params=_cparams(1),
        name="out_proj",
    )(att, mix, w, w, x2, mod3)


MXU_WIDTH = 256


def _swiglu_accumulate(h, wg_ref, wu_ref, wd_refs, o_ref, assign=False):
    tm, tf = h.shape[0], wg_ref.shape[1]
    if tf % (2 * MXU_WIDTH) == 0:
        splits = [(slice(0, tm), slice(p * tf // 2, (p + 1) * tf // 2)) for p in range(2)]
    else:
        splits = [(slice(p * tm // 2, (p + 1) * tm // 2), slice(0, tf)) for p in range(2)]
    ups = [(jnp.dot(h[rows], wg_ref[:, cols], preferred_element_type=F32),
            jnp.dot(h[rows], wu_ref[:, cols], preferred_element_type=F32)) for rows, cols in splits]
    acts = [(gv * _sigmoid(gv) * uv).astype(BF16) for gv, uv in ups]
    same_rows = splits[0][0] == splits[1][0]
    col = 0
    for wd_ref in wd_refs:
        width = wd_ref.shape[1]
        downs = [jnp.dot(act, wd_ref[cols, :], preferred_element_type=F32) for act, (_, cols) in zip(acts, splits)]
        pieces = [(splits[0][0], downs[0] + downs[1])] if same_rows else list(zip((s[0] for s in splits), downs))
        for rows, part in pieces:
            if assign:
                o_ref[rows, col:col + width] = part
            else:
                o_ref[rows, col:col + width] += part
        col += width


def _ffn_kernel(x_ref, shift_ref, scale_ref, gate_ref, g_ref, wg_ref, wu_ref, wd_ref, *rest, n_cast):
    cast_in, o_ref, cast_out, h_ref = rest[:n_cast], rest[n_cast], rest[n_cast + 1:2 * n_cast + 1], rest[-1]
    j = pl.program_id(1)

    @pl.when(j == 0)
    def _():
        _cast_slabs(cast_in, cast_out)
        _norm_modulate_rows(x_ref, h_ref, g_ref[...], shift_ref[0], scale_ref[0], static=True)
        _swiglu_accumulate(h_ref[...], wg_ref, wu_ref, (wd_ref,), o_ref, assign=True)

    last = pl.num_programs(1) - 1

    @pl.when(jnp.logical_and(j > 0, j < last))
    def _():
        _cast_slabs(cast_in, cast_out)
        _swiglu_accumulate(h_ref[...], wg_ref, wu_ref, (wd_ref,), o_ref)

    @pl.when(j == last)
    def _():
        _cast_slabs(cast_in, cast_out)
        _swiglu_accumulate(h_ref[...], wg_ref, wu_ref, (wd_ref,), o_ref)
        o_ref[...] = x_ref[...] + gate_ref[0] * o_ref[...]


BF16_SUBLANE_TILE = 16


def _slab_rows(rows, n_lead, n_steps):
    for height in range(BF16_SUBLANE_TILE, rows + 1, BF16_SUBLANE_TILE):
        if rows % height == 0 and n_lead * (rows // height) <= n_steps:
            return height
    raise ValueError(f"no slab height for rows={rows}, n_lead={n_lead}, n_steps={n_steps}")


def _cast_plan(jobs, n_steps, step_of):
    arrays, in_specs, out_specs, out_shapes = [], [], [], []
    for arr, col_block, n_col_blocks, *only in jobs:
        _, rows, cols = arr.shape
        lead0, n_lead = (only[0], 1) if only else (0, arr.shape[0])
        width = cols // n_col_blocks
        height = _slab_rows(rows, n_lead, n_steps)
        per_lead = rows // height
        last = n_lead * per_lead - 1

        def index(*grid_idx, per_lead=per_lead, last=last, col=0, lead0=0):
            s = jnp.minimum(step_of(*grid_idx), last)
            return (lead0 + s // per_lead, s % per_lead, col)

        arrays.append(arr)
        in_specs.append(pl.BlockSpec((1, height, width), functools.partial(index, col=col_block, lead0=lead0)))
        out_specs.append(pl.BlockSpec((1, height, width), index))
        out_shapes.append(jax.ShapeDtypeStruct((n_lead, rows, width), BF16))
    return arrays, in_specs, out_specs, out_shapes


def _cast_slabs(cast_in, cast_out):
    for src, dst in zip(cast_in, cast_out):
        dst[...] = src[...].astype(dst.dtype)


def _dense_ffn(x2, mod3, mod_row, g, wg, wu, wd, *, tm, tf, cast=()):
    m, d = x2.shape
    f = wg.shape[2]
    nf = f // tf
    assert nf >= 2, "the first and the last hidden tile run different blocks"
    n_steps = (m // tm) * nf
    modspec = lambda k: pl.BlockSpec((1, 1, d), lambda i, j: (mod_row(i), 0, k))
    cast_arrays, cast_in, cast_out, cast_shapes = _cast_plan(cast, n_steps, lambda i, j: i * nf + j)
    outs = pl.pallas_call(
        functools.partial(_ffn_kernel, n_cast=len(cast)),
        grid=(m // tm, nf),
        in_specs=[pl.BlockSpec((tm, d), lambda i, j: (i, 0)),
                  modspec(3), modspec(4), modspec(5),
                  pl.BlockSpec((1, d), lambda i, j: (0, 0)),
                  pl.BlockSpec((None, d, tf), lambda i, j: (0, 0, j)),
                  pl.BlockSpec((None, d, tf), lambda i, j: (0, 0, j)),
                  pl.BlockSpec((None, tf, d), lambda i, j: (0, j, 0))] + cast_in,
        out_specs=[pl.BlockSpec((tm, d), lambda i, j: (i, 0))] + cast_out,
        out_shape=[jax.ShapeDtypeStruct((m, d), F32)] + cast_shapes,
        scratch_shapes=[pltpu.VMEM((tm, d), BF16)],
        compiler_params=_cparams(2),
        name="dense_ffn",
    )(x2, mod3, mod3, mod3, g, wg, wu, wd, *cast_arrays)
    return outs[0], tuple(outs[1:])


ROUTER_ROWS = 16


def _router_kernel(x_ref, shift_ref, scale_ref, g_ref, rw_ref, t_ref, idx_ref, gate_ref, cnt_ref,
                   tri_ref, base_ref):
    tm = x_ref.shape[0]

    @pl.when(pl.program_id(0) == 0)
    def _():
        r = lax.broadcasted_iota(jnp.int32, (tm, tm), 0)
        c = lax.broadcasted_iota(jnp.int32, (tm, tm), 1)
        tri_ref[...] = jnp.where(r < c, 1.0, 0.0).astype(BF16)
        base_ref[...] = jnp.zeros(base_ref.shape, F32)

    _norm_modulate_rows(x_ref, t_ref, g_ref[...], shift_ref[0], scale_ref[0], static=True)

    t = t_ref[...]
    t_hi = t.astype(BF16)
    t_lo = (t - t_hi.astype(F32)).astype(BF16)
    w = rw_ref[...]
    w_hi = w.astype(BF16)
    w_lo = (w - w_hi.astype(F32)).astype(BF16)
    dn = (((1,), (1,)), ((), ()))
    logits = (lax.dot_general(w_hi, t_hi, dn, preferred_element_type=F32)
              + lax.dot_general(w_lo, t_hi, dn, preferred_element_type=F32)
              + lax.dot_general(w_hi, t_lo, dn, preferred_element_type=F32))

    e = lax.broadcasted_iota(jnp.int32, (ROUTER_ROWS, tm), 0).astype(F32)
    neg = jnp.float32(-jnp.inf)
    lg = jnp.where(e < N_EXPERTS, logits, neg)
    m1 = jnp.max(lg, axis=0, keepdims=True)
    i1 = jnp.min(jnp.where(lg == m1, e, float(ROUTER_ROWS)), axis=0, keepdims=True)
    lg2 = jnp.where(e == i1, neg, lg)
    m2 = jnp.max(lg2, axis=0, keepdims=True)
    i2 = jnp.min(jnp.where(lg2 == m2, e, float(ROUTER_ROWS)), axis=0, keepdims=True)
    ex = jnp.exp(m2 - m1)
    den = 1.0 + ex
    gate_ref[0:1, :] = 1.0 / den
    gate_ref[1:2, :] = ex / den

    hit1 = e == i1
    hit2 = e == i2
    onehot = jnp.where(hit1 | hit2, 1.0, 0.0)
    prefix = jnp.dot(onehot.astype(BF16), tri_ref[...], preferred_element_type=F32) + base_ref[:, 0:1]
    r1 = jnp.sum(jnp.where(hit1, prefix, 0.0), axis=0, keepdims=True)
    r2 = jnp.sum(jnp.where(hit2, prefix, 0.0), axis=0, keepdims=True)
    idx_ref[0:1, :] = i1.astype(jnp.int32)
    idx_ref[1:2, :] = i2.astype(jnp.int32)
    idx_ref[2:3, :] = r1.astype(jnp.int32)
    idx_ref[3:4, :] = r2.astype(jnp.int32)
    base_ref[...] = base_ref[...] + jnp.sum(onehot, axis=1, keepdims=True)
    cnt_ref[...] = base_ref[...].astype(jnp.int32)


def _router(x2, mod3, mod_row, g, rw16, *, tm):
    m, d = x2.shape
    modspec = lambda k: pl.BlockSpec((1, 1, d), lambda i: (mod_row(i), 0, k))
    return pl.pallas_call(
        _router_kernel,
        grid=(m // tm,),
        in_specs=[pl.BlockSpec((tm, d), lambda i: (i, 0)),
                  modspec(3), modspec(4),
                  pl.BlockSpec((1, d), lambda i: (0, 0)),
                  pl.BlockSpec((ROUTER_ROWS, d), lambda i: (0, 0))],
        out_specs=[pl.BlockSpec((tm, d), lambda i: (i, 0)),
                   pl.BlockSpec((4, tm), lambda i: (0, i)),
                   pl.BlockSpec((2, tm), lambda i: (0, i)),
                   pl.BlockSpec((ROUTER_ROWS, 128), lambda i: (0, 0))],
        out_shape=[jax.ShapeDtypeStruct((m, d), F32),
                   jax.ShapeDtypeStruct((4, m), jnp.int32),
                   jax.ShapeDtypeStruct((2, m), F32),
                   jax.ShapeDtypeStruct((ROUTER_ROWS, 128), jnp.int32)],
        scratch_shapes=[pltpu.VMEM((tm, tm), BF16), pltpu.VMEM((ROUTER_ROWS, 128), F32)],
        compiler_params=_cparams(1),
        name="router",
    )(x2, mod3, mod3, g, rw16)


def _row_copy(src_ref, src_row, dst_ref, dst_row, sem):
    return pltpu.make_async_copy(src_ref.at[pl.ds(src_row, 1), :], dst_ref.at[pl.ds(dst_row, 1), :], sem)


DMA_ISSUE_UNROLL = 8


def _scatter_kernel(dest_ref, pad_ref, t_ref, buf_ref, zero_ref, sem, zsem, *, n_tok, n_tiles):
    tm = t_ref.shape[0]
    base = pl.program_id(0) * tm

    @pl.when(pl.program_id(0) == 0)
    def _():
        zero_ref[...] = jnp.zeros(zero_ref.shape, zero_ref.dtype)
        for e in range(N_EXPERTS):
            start, length = pad_ref[e], pad_ref[N_EXPERTS + e]

            def fill(r, carry, start=start):
                _row_copy(zero_ref, 0, buf_ref, start + r, zsem).start()
                return carry

            def drain(r, carry):
                _row_copy(zero_ref, 0, buf_ref, 0, zsem).wait()
                return carry

            lax.fori_loop(0, length, fill, 0)
            lax.fori_loop(0, length, drain, 0)

        def tile_copy(tile):
            row0 = pl.multiple_of(tile * EXPERT_TM, EXPERT_TM)
            return pltpu.make_async_copy(zero_ref, buf_ref.at[pl.ds(row0, EXPERT_TM), :], zsem)

        def fill_tile(tile, carry):
            tile_copy(tile).start()
            return carry

        def drain_tile(tile, carry):
            tile_copy(tile).wait()
            return carry

        lax.fori_loop(pad_ref[2 * N_EXPERTS], n_tiles, fill_tile, 0)
        lax.fori_loop(pad_ref[2 * N_EXPERTS], n_tiles, drain_tile, 0)

    def start(r, carry):
        for k in range(2):
            _row_copy(t_ref, r, buf_ref, dest_ref[k * n_tok + base + r], sem).start()
        return carry

    lax.fori_loop(0, tm, start, 0, unroll=DMA_ISSUE_UNROLL)
    for k in range(2):
        pltpu.make_async_copy(t_ref, buf_ref.at[pl.ds(0, tm), :], sem).wait()


def _scatter_rows(dest, pad_info, t, *, tm, n_tiles):
    m, d = t.shape
    grid_spec = pltpu.PrefetchScalarGridSpec(
        num_scalar_prefetch=2,
        grid=(m // tm,),
        in_specs=[pl.BlockSpec((tm, d), lambda i, dest, pad: (i, 0))],
        out_specs=pl.BlockSpec(memory_space=pl.ANY),
        scratch_shapes=[pltpu.VMEM((EXPERT_TM, d), t.dtype),
                        pltpu.SemaphoreType.DMA(()), pltpu.SemaphoreType.DMA(())],
    )
    return pl.pallas_call(
        functools.partial(_scatter_kernel, n_tok=m, n_tiles=n_tiles),
        grid_spec=grid_spec,
        out_shape=jax.ShapeDtypeStruct((n_tiles * EXPERT_TM, d), t.dtype),
        compiler_params=_cparams(1),
        name="moe_scatter",
    )(dest, pad_info, t)


N_WEIGHT_SLOTS = 3


def _expert_kernel(te_ref, tv_ref, tx_ref, x_ref, *rest, n_wd, tf, nf):
    del tx_ref
    w_hbm = rest[:2 + n_wd]
    o_ref, xb_ref = rest[2 + n_wd], rest[3 + n_wd]
    w_buf = rest[4 + n_wd:6 + 2 * n_wd]
    sem = rest[-1]
    i = pl.program_id(0)
    n = pl.num_programs(0)

    def copies(tile, j, slot):
        e = te_ref[tile]
        cols = pl.ds(j * tf, tf)
        srcs = [w_hbm[0].at[e, :, cols], w_hbm[1].at[e, :, cols]] + [w.at[e, cols, :] for w in w_hbm[2:]]
        return [pltpu.make_async_copy(src, buf.at[slot], sem.at[slot, k])
                for k, (src, buf) in enumerate(zip(srcs, w_buf))]

    valid = tv_ref[i] == 1
    nxt = jnp.minimum(i + 1, n - 1)
    next_live = jnp.logical_and(i + 1 < n, tv_ref[nxt] == 1)

    @pl.when(jnp.logical_not(valid))
    def _():
        o_ref[...] = jnp.zeros(o_ref.shape, o_ref.dtype)

    @pl.when(jnp.logical_and(valid, i == 0))
    def _():
        for c in copies(i, 0, 0):
            c.start()

    @pl.when(valid)
    def _():
        xb_ref[...] = x_ref[...].astype(BF16)
        xb = xb_ref[...]
        for j in range(nf):
            slot = j % N_WEIGHT_SLOTS
            ahead = copies(i, j + 1, (j + 1) % N_WEIGHT_SLOTS) if j + 1 < nf else copies(nxt, 0, 0)
            for c in ahead:
                c.start()
            for c in copies(i, j, slot):
                c.wait()
            _swiglu_accumulate(xb, w_buf[0].at[slot], w_buf[1].at[slot], [b.at[slot] for b in w_buf[2:]],
                               o_ref, assign=j == 0)

    @pl.when(jnp.logical_and(valid, jnp.logical_not(next_live)))
    def _():
        for c in copies(nxt, 0, 0):
            c.wait()


def _experts(tile_e, tile_v, tile_x, buf, wg, wu, wds, *, tf):
    rows, d = buf.shape
    f = wg.shape[2]
    nf = f // tf
    assert nf % N_WEIGHT_SLOTS != 1, "slot 0 must be free while the last hidden tile computes"
    any_spec = pl.BlockSpec(memory_space=pl.ANY)
    grid_spec = pltpu.PrefetchScalarGridSpec(
        num_scalar_prefetch=3,
        grid=(rows // EXPERT_TM,),
        in_specs=[pl.BlockSpec((EXPERT_TM, d), lambda i, te, tv, tx: (tx[i], 0))] + [any_spec] * (2 + len(wds)),
        out_specs=pl.BlockSpec((EXPERT_TM, d), lambda i, te, tv, tx: (i, 0)),
        scratch_shapes=[pltpu.VMEM((EXPERT_TM, d), BF16),
                        pltpu.VMEM((N_WEIGHT_SLOTS, d, tf), BF16),
                        pltpu.VMEM((N_WEIGHT_SLOTS, d, tf), BF16)]
                       + [pltpu.VMEM((N_WEIGHT_SLOTS, tf, wd.shape[2]), BF16) for wd in wds]
                       + [pltpu.SemaphoreType.DMA((N_WEIGHT_SLOTS, 2 + len(wds)))],
    )
    return pl.pallas_call(
        functools.partial(_expert_kernel, n_wd=len(wds), tf=tf, nf=nf),
        grid_spec=grid_spec,
        out_shape=jax.ShapeDtypeStruct((rows, d), F32),
        compiler_params=_cparams(1),
        name="moe_experts",
    )(tile_e, tile_v, tile_x, buf, wg, wu, *wds)


def _combine_kernel(dest_ref, x_ref, gate5_ref, gates_ref, gf_ref, ybuf_ref, o_ref, rows_ref, sem, *, n_tok):
    tm = x_ref.shape[0]
    i = pl.program_id(0)

    def issue(tile, slot):
        base = tile * tm

        def start(r, carry):
            for k in range(2):
                _row_copy(ybuf_ref, dest_ref[k * n_tok + base + r], rows_ref.at[slot, k], r, sem.at[slot]).start()
            return carry

        lax.fori_loop(0, tm, start, 0, unroll=DMA_ISSUE_UNROLL)

    @pl.when(i == 0)
    def _():
        issue(0, 0)

    @pl.when(i + 1 < pl.num_programs(0))
    def _():
        issue(i + 1, (i + 1) % 2)

    slot = i % 2
    for k in range(2):
        pltpu.make_async_copy(ybuf_ref.at[pl.ds(0, tm), :], rows_ref.at[slot, k], sem.at[slot]).wait()

    eye = (lax.broadcasted_iota(jnp.int32, (tm, tm), 0) == lax.broadcasted_iota(jnp.int32, (tm, tm), 1))
    g0 = jnp.sum(jnp.where(eye, gates_ref[0:1, :], 0.0), axis=1, keepdims=True)
    g1 = jnp.sum(jnp.where(eye, gates_ref[1:2, :], 0.0), axis=1, keepdims=True)
    y = g0 * rows_ref[slot, 0] + g1 * rows_ref[slot, 1]
    xn = x_ref[...] + gate5_ref[0] * y
    ms = jnp.mean(xn * xn, axis=-1, keepdims=True)
    o_ref[...] = xn * lax.rsqrt(ms + EPS) * gf_ref[...]


def _combine(dest, x2, mod3, mod_row, gates, g_final, ybuf, *, tm):
    m, d = x2.shape
    grid_spec = pltpu.PrefetchScalarGridSpec(
        num_scalar_prefetch=1,
        grid=(m // tm,),
        in_specs=[pl.BlockSpec((tm, d), lambda i, dest: (i, 0)),
                  pl.BlockSpec((1, 1, d), lambda i, dest: (mod_row(i), 0, 5)),
                  pl.BlockSpec((2, tm), lambda i, dest: (0, i)),
                  pl.BlockSpec((1, d), lambda i, dest: (0, 0)),
                  pl.BlockSpec(memory_space=pl.ANY)],
        out_specs=pl.BlockSpec((tm, d), lambda i, dest: (i, 0)),
        scratch_shapes=[pltpu.VMEM((2, 2, tm, d), F32), pltpu.SemaphoreType.DMA((2,))],
    )
    return pl.pallas_call(
        functools.partial(_combine_kernel, n_tok=m),
        grid_spec=grid_spec,
        out_shape=jax.ShapeDtypeStruct((m, d), F32),
        compiler_params=_cparams(1),
        name="moe_combine",
    )(dest, x2, mod3, gates, g_final, ybuf)


def _rope_tables(seq):
    rows = seq // GRID_W
    row = jnp.repeat(jnp.arange(rows), GRID_W).astype(F32)
    col = jnp.tile(jnp.arange(GRID_W), rows).astype(F32)
    axis_dim = HEAD_DIM // 2
    inv_freq = ROPE_THETA ** (-jnp.arange(0, axis_dim, 2, dtype=F32) / axis_dim)
    ang_r = row[:, None] * inv_freq
    ang_c = col[:, None] * inv_freq
    cr, sr, cc, sc = jnp.cos(ang_r), jnp.sin(ang_r), jnp.cos(ang_c), jnp.sin(ang_c)
    return (jnp.concatenate([cr, cr, cc, cc], axis=-1),
            jnp.concatenate([-sr, sr, -sc, sc], axis=-1))


def kernel(x, c, ctx, c_ctx, w_ada, b_ada, g_mix, w_in, g_q, g_k, sconv_w, conf_dw, conf_db, conf_ln_g,
           conf_ln_b, conf_pw, conf_pb, g_group, w_o, g_ffn, dense_wg, dense_wu, dense_wd, router_w,
           moe_wg, moe_wu, moe_wd, g_final):
    b, s, d = x.shape
    n_ctx = ctx.shape[1]
    depth = w_ada.shape[0]
    assert b + 1 <= MOD_ROWS and depth == 2
    m_lat, m_ctx = b * s, b * n_ctx
    ctx_row = b

    cin = jnp.concatenate([c, c_ctx[None, :], jnp.zeros((MOD_ROWS - b - 1, d), F32)], axis=0)
    mod = _ada(cin, w_ada, b_ada)
    cos_t, sin_t = _rope_tables(s)
    row2 = lambda v: v.reshape(1, -1)

    lat_tm = ROW_TM
    lat_row_for = lambda tm: (lambda i: i // (s // tm))
    lat_row = lat_row_for(lat_tm)
    ctx_tm = ROW_TM
    ctx_mod_row = lambda i: ctx_row

    x2 = x.reshape(m_lat, d)
    xc2 = ctx.reshape(m_ctx, d)
    out = None
    moe_wd_b = ()
    next_proj_b = ()
    for l in range(depth):
        last = l == depth - 1
        mod3 = mod[l].reshape(MOD_ROWS, 1, 6 * d)
        if next_proj_b:
            w_in_b, w_o_b = (w.reshape(w.shape[1:]) for w in next_proj_b)
            next_proj_b = ()
        else:
            w_in_b, w_o_b = w_in[l].astype(BF16), w_o[l].astype(BF16)
        pw_b = conf_pw[l].astype(BF16)
        gg = g_group[l]
        gga, ggs, ggc = row2(gg[:ATTN_W]), row2(gg[ATTN_W:ATTN_W + SCONV_W]), row2(gg[ATTN_W + SCONV_W:])
        mixer_args = (sconv_w[l], conf_dw[l], row2(conf_db[l]), row2(conf_ln_g[l]), row2(conf_ln_b[l]),
                      pw_b, row2(conf_pb[l]), ggs, ggc)
        in_args = (row2(g_mix[l]), w_in_b, row2(g_q[l]), row2(g_k[l]), cos_t, sin_t)

        dense_here = l % 2 == 0
        k_dense = l // 2
        cast = (tuple((w[k_dense:k_dense + 1], 0, 1) for w in (dense_wg, dense_wu, dense_wd))
                if dense_here else ())
        k_moe = (l + 1) // 2 if dense_here else l // 2
        moe_wd_job = ((moe_wd[k_moe], l % 2, 2),) if k_moe < moe_wd.shape[0] else ()
        p_lat, dense_w_b = _in_proj(x2, mod3, lat_row, *in_args, tm=lat_tm, rope=True, j0=0, nj=8, seq=s,
                                    cast=cast)
        if last:
            p_ctx, _ = _in_proj(xc2, mod3, ctx_mod_row, *in_args, tm=ctx_tm, rope=False, j0=2, nj=1, seq=ctx_tm)
            kc_blk = 0
        else:
            p_ctx, _ = _in_proj(xc2, mod3, ctx_mod_row, *in_args, tm=ctx_tm, rope=False, j0=0, nj=8, seq=ctx_tm)
            kc_blk = ATTN_W // KV_W
        att, moe_wd_half = _attention(p_lat, p_ctx, p_lat, gga, n_batch=b, q_len=s, tq=ATTN_TQ, kc_blk=kc_blk,
                                      has_latent=True, n_ctx=n_ctx, seq=s, cast=moe_wd_job)
        moe_wd_b = moe_wd_b + moe_wd_half
        mix = _local_mixers(p_lat, *mixer_args, n_seq=b, seq=s)
        x2 = _out_proj(att, mix, w_o_b, x2, mod3, lat_row, tm=lat_tm)
        if not last:
            att_c, _ = _attention(p_ctx, p_ctx, p_ctx, gga, n_batch=b, q_len=n_ctx, tq=n_ctx, kc_blk=kc_blk,
                                  has_latent=False, n_ctx=n_ctx, seq=n_ctx)
            mix_c = _local_mixers(p_ctx, *mixer_args, n_seq=b, seq=n_ctx)
            xc2 = _out_proj(att_c, mix_c, w_o_b, xc2, mod3, ctx_mod_row, tm=ctx_tm)

        if l % 2 == 0:
            wg_b, wu_b, wd_b = dense_w_b
            ffn = functools.partial(_dense_ffn, g=row2(g_ffn[l]), wg=wg_b, wu=wu_b, wd=wd_b,
                                    tf=FFN_TF)
            routed_next = l + 1 < depth and (l + 1) % 2 == 1
            cast = ((moe_wg[(l + 1) // 2], 0, 1), (moe_wu[(l + 1) // 2], 0, 1)) if routed_next else ()
            x2, moe_up_b = ffn(x2, mod3, lat_row, tm=lat_tm, cast=cast)
            if not last:
                proj_jobs = ((w_in, 0, 1, l + 1), (w_o, 0, 1, l + 1))
                xc2, next_proj_b = ffn(xc2, mod3, ctx_mod_row, tm=ctx_tm, cast=proj_jobs)
        else:
            assert last, "routed layer is implemented for the final layer (latent tokens only)"
            rw16 = jnp.zeros((ROUTER_ROWS, d), F32).at[:N_EXPERTS].set(router_w[l // 2].T)
            t, idx, gates, cnt = _router(x2, mod3, lat_row, row2(g_ffn[l]), rw16, tm=lat_tm)
            counts = cnt[:N_EXPERTS, 0]
            padded = (counts + EXPERT_TM - 1) // EXPERT_TM * EXPERT_TM
            pends = jnp.cumsum(padded)
            pstarts = pends - padded
            slot_e = idx[0:2]
            slot_start = sum(jnp.where(slot_e == e, pstarts[e], 0) for e in range(N_EXPERTS))
            dest = (slot_start + idx[2:4]).reshape(-1).astype(jnp.int32)
            n_tiles = (2 * m_lat) // EXPERT_TM + N_EXPERTS
            tile_start = jnp.arange(n_tiles, dtype=jnp.int32) * EXPERT_TM
            tile_v = (tile_start < pends[-1]).astype(jnp.int32)
            last_tile = pends[-1] // EXPERT_TM - 1
            tile_x = jnp.minimum(jnp.arange(n_tiles, dtype=jnp.int32), last_tile).astype(jnp.int32)
            tile_e = jnp.minimum(jnp.sum((tile_x * EXPERT_TM)[:, None] >= pends[None, :], axis=1),
                                 N_EXPERTS - 1).astype(jnp.int32)
            pad_info = jnp.concatenate([pstarts + counts, padded - counts,
                                        (pends[-1:] // EXPERT_TM)]).astype(jnp.int32)
            buf = _scatter_rows(dest, pad_info, t, tm=lat_tm, n_tiles=n_tiles)
            ybuf = _experts(tile_e, tile_v, tile_x, buf, *moe_up_b, moe_wd_b, tf=FFN_TF)
            out = _combine(dest, x2, mod3, lat_row_for(COMBINE_TM), gates, row2(g_final), ybuf, tm=COMBINE_TM)
    return out.reshape(b, s, d)
```

```python
import functools

import jax
import jax.numpy as jnp
from jax import lax
from jax.experimental import pallas as pl
from jax.experimental.pallas import tpu as pltpu

F32 = jnp.float32
BF16 = jnp.bfloat16

GRID_W = 64
HEAD_DIM = 128
N_Q_HEADS = 8
N_KV_HEADS = 2
GQA_GROUP = N_Q_HEADS // N_KV_HEADS
ATTN_W = N_Q_HEADS * HEAD_DIM
KV_W = N_KV_HEADS * HEAD_DIM
ROPE_THETA = 10000.0
SCONV_W = 512
SCONV_K = 3
CONF_W = 512
CONF_K = 31
N_EXPERTS = 8
EPS = 1e-6
LOG2E = 1.4426950408889634

V7X_VMEM_LIMIT_BYTES = 56 * 1024 * 1024
MOD_ROWS = 16
IN_TN = 512
EXPERT_TM = 512
ROW_TM = 512
FFN_TF = 512
ATTN_TQ = 512
COMBINE_TM = 256
ADA_TN = 1536


def _cparams(n_axes):
    return pltpu.CompilerParams(dimension_semantics=("arbitrary",) * n_axes,
                                vmem_limit_bytes=V7X_VMEM_LIMIT_BYTES)


def _sigmoid(x):
    return 1.0 / (1.0 + jnp.exp(-x))


NORM_CHUNK = 16
NORM_UNROLL = 4


def _norm_modulate_rows(x_ref, h_ref, g, shift, scale, static=False):
    rows = x_ref.shape[0]
    gain = g * (1.0 + scale)

    def body(c, carry):
        r = c * NORM_CHUNK if static else pl.multiple_of(c * NORM_CHUNK, NORM_CHUNK)
        x = x_ref[pl.ds(r, NORM_CHUNK), :]
        inv = lax.rsqrt(jnp.mean(x * x, axis=-1, keepdims=True) + EPS)
        h_ref[pl.ds(r, NORM_CHUNK), :] = (x * inv * gain + shift).astype(h_ref.dtype)
        return carry

    if static:
        for c in range(rows // NORM_CHUNK):
            body(c, 0)
        return

    lax.fori_loop(0, rows // NORM_CHUNK, body, 0, unroll=NORM_UNROLL)


def _ada_kernel(c_ref, w_ref, b_ref, o_ref):
    a = c_ref[...]
    a = (a * _sigmoid(a)).astype(BF16)
    o_ref[0] = jnp.dot(a, w_ref[0].astype(BF16), preferred_element_type=F32) + b_ref[0]


def _ada(cin, w_ada, b_ada):
    depth, d, n = w_ada.shape
    tn = ADA_TN
    return pl.pallas_call(
        _ada_kernel,
        grid=(depth, n // tn),
        in_specs=[pl.BlockSpec((MOD_ROWS, d), lambda l, j: (0, 0)),
                  pl.BlockSpec((1, d, tn), lambda l, j: (l, 0, j)),
                  pl.BlockSpec((1, 1, tn), lambda l, j: (l, 0, j))],
        out_specs=pl.BlockSpec((1, MOD_ROWS, tn), lambda l, j: (l, 0, j)),
        out_shape=jax.ShapeDtypeStruct((depth, MOD_ROWS, n), F32),
        compiler_params=_cparams(2),
        name="ada",
    )(cin, w_ada, b_ada.reshape(depth, 1, n))


def _head_norm(seg, gain):
    ms = jnp.mean(seg * seg, axis=-1, keepdims=True)
    return seg * lax.rsqrt(ms + EPS) * gain


def _rope(n, cos_t, sin_t):
    lane = lax.broadcasted_iota(jnp.int32, n.shape, 1)
    fwd = pltpu.roll(n, 32, 1)
    bwd = pltpu.roll(n, 96, 1)
    partner = jnp.where((lane // 32) % 2 == 0, bwd, fwd)
    return n * cos_t + partner * sin_t


def _in_kernel(x_ref, shift_ref, scale_ref, g_ref, w_ref, gq_ref, gk_ref, cos_ref, sin_ref,
               *rest, rope, j0, nj, q_scale, n_cast):
    cast_in, o_ref, cast_out, h_ref = rest[:n_cast], rest[n_cast], rest[n_cast + 1:2 * n_cast + 1], rest[-1]
    _norm_modulate_rows(x_ref, h_ref, g_ref[...], shift_ref[0], scale_ref[0], static=True)
    _cast_slabs(cast_in, cast_out)
    h = h_ref[...]

    def head(seg, gain, scale):
        n = _head_norm(seg, gain)
        if rope:
            n = _rope(n, cos_ref[...], sin_ref[...])
        if scale != 1.0:
            n = n * scale
        return n

    for t in range(nj):
        j = t + j0
        c0 = t * IN_TN
        acc = jnp.dot(h, w_ref[:, c0:c0 + IN_TN], preferred_element_type=F32)
        if j < 2:
            for hd in range(IN_TN // HEAD_DIM):
                sl = slice(hd * HEAD_DIM, (hd + 1) * HEAD_DIM)
                osl = slice(c0 + hd * HEAD_DIM, c0 + (hd + 1) * HEAD_DIM)
                o_ref[:, osl] = head(acc[:, sl], gq_ref[...], q_scale).astype(o_ref.dtype)
        elif j == 2:
            for hd in range(N_KV_HEADS):
                sl = slice(hd * HEAD_DIM, (hd + 1) * HEAD_DIM)
                osl = slice(c0 + hd * HEAD_DIM, c0 + (hd + 1) * HEAD_DIM)
                o_ref[:, osl] = head(acc[:, sl], gk_ref[...], 1.0).astype(o_ref.dtype)
            o_ref[:, c0 + KV_W:c0 + IN_TN] = acc[:, KV_W:].astype(o_ref.dtype)
        else:
            o_ref[:, c0:c0 + IN_TN] = acc.astype(o_ref.dtype)


def _const_spec(shape, idx):
    return pl.BlockSpec(shape, lambda i: idx, pipeline_mode=pl.Buffered(1))


def _in_proj(x2, mod3, mod_row, g, w, gq, gk, cos_t, sin_t, *, tm, rope, j0, nj, seq, cast=()):
    m, d = x2.shape
    tiles_per_seq = seq // tm
    ncols = nj * IN_TN
    assert j0 % nj == 0
    q_scale = HEAD_DIM ** -0.5 * LOG2E
    kern = functools.partial(_in_kernel, rope=rope, j0=j0, nj=nj, q_scale=q_scale, n_cast=len(cast))
    cast_arrays, cast_in, cast_out, cast_shapes = _cast_plan(cast, m // tm, lambda i: i)
    outs = pl.pallas_call(
        kern,
        grid=(m // tm,),
        in_specs=[pl.BlockSpec((tm, d), lambda i: (i, 0)),
                  pl.BlockSpec((1, 1, d), lambda i: (mod_row(i), 0, 0)),
                  pl.BlockSpec((1, 1, d), lambda i: (mod_row(i), 0, 1)),
                  _const_spec((1, d), (0, 0)),
                  _const_spec((d, ncols), (0, j0 // nj)),
                  _const_spec((1, HEAD_DIM), (0, 0)),
                  _const_spec((1, HEAD_DIM), (0, 0)),
                  pl.BlockSpec((tm, HEAD_DIM), lambda i: (i % tiles_per_seq, 0)),
                  pl.BlockSpec((tm, HEAD_DIM), lambda i: (i % tiles_per_seq, 0))] + cast_in,
        out_specs=[pl.BlockSpec((tm, ncols), lambda i: (i, 0))] + cast_out,
        out_shape=[jax.ShapeDtypeStruct((m, ncols), BF16)] + cast_shapes,
        scratch_shapes=[pltpu.VMEM((tm, d), BF16)],
        compiler_params=_cparams(1),
        name="in_proj",
    )(x2, mod3, mod3, g, w, gq, gk, cos_t, sin_t, *cast_arrays)
    return outs[0], tuple(outs[1:])


def _attn_kernel(q_ref, kc_ref, vc_ref, kl_ref, vl_ref, gg_ref, *rest, has_latent, n_cast):
    cast_in, o_ref, cast_out, acc_ref = rest[:n_cast], rest[n_cast], rest[n_cast + 1:2 * n_cast + 1], rest[-1]
    _cast_slabs(cast_in, cast_out)
    tq = q_ref.shape[0]
    dn = (((1,), (1,)), ((), ()))
    ssq = jnp.zeros((tq, 1), F32)
    for hd in range(N_Q_HEADS):
        kv = hd // GQA_GROUP
        ksl = slice(kv * HEAD_DIM, (kv + 1) * HEAD_DIM)
        q = q_ref[:, hd * HEAD_DIM:(hd + 1) * HEAD_DIM]
        s_c = lax.dot_general(q, kc_ref[:, ksl], dn, preferred_element_type=F32)
        mx = jnp.max(s_c, axis=-1, keepdims=True)
        if has_latent:
            s_l = lax.dot_general(q, kl_ref[:, ksl], dn, preferred_element_type=F32)
            mx = jnp.maximum(mx, jnp.max(s_l, axis=-1, keepdims=True))
        p_c = jnp.exp2(s_c - mx)
        den = jnp.sum(p_c, axis=-1, keepdims=True)
        o = jnp.dot(p_c.astype(BF16), vc_ref[:, ksl], preferred_element_type=F32)
        if has_latent:
            p_l = jnp.exp2(s_l - mx)
            den = den + jnp.sum(p_l, axis=-1, keepdims=True)
            o = o + jnp.dot(p_l.astype(BF16), vl_ref[:, ksl], preferred_element_type=F32)
        o = o * (1.0 / den)
        ssq = ssq + jnp.sum(o * o, axis=-1, keepdims=True)
        acc_ref[:, hd * HEAD_DIM:(hd + 1) * HEAD_DIM] = o
    inv = lax.rsqrt(ssq * (1.0 / ATTN_W) + EPS)
    o_ref[...] = (acc_ref[...] * inv * gg_ref[...]).astype(o_ref.dtype)


def _attention(qsrc, csrc, lsrc, gg, *, n_batch, q_len, tq, kc_blk, has_latent, n_ctx, seq, cast=()):
    m = qsrc.shape[0]
    tiles = q_len // tq
    kern = functools.partial(_attn_kernel, has_latent=has_latent, n_cast=len(cast))
    cast_arrays, cast_in, cast_out, cast_shapes = _cast_plan(cast, n_batch * tiles, lambda b, i: b * tiles + i)
    outs = pl.pallas_call(
        kern,
        grid=(n_batch, tiles),
        in_specs=[pl.BlockSpec((tq, ATTN_W), lambda b, i: (b * tiles + i, 0)),
                  pl.BlockSpec((n_ctx, KV_W), lambda b, i: (b, kc_blk)),
                  pl.BlockSpec((n_ctx, KV_W), lambda b, i: (b, kc_blk + 1)),
                  pl.BlockSpec((seq, KV_W), lambda b, i: (b, ATTN_W // KV_W)),
                  pl.BlockSpec((seq, KV_W), lambda b, i: (b, ATTN_W // KV_W + 1)),
                  pl.BlockSpec((1, ATTN_W), lambda b, i: (0, 0))] + cast_in,
        out_specs=[pl.BlockSpec((tq, ATTN_W), lambda b, i: (b * tiles + i, 0))] + cast_out,
        out_shape=[jax.ShapeDtypeStruct((m, ATTN_W), BF16)] + cast_shapes,
        scratch_shapes=[pltpu.VMEM((tq, ATTN_W), F32)],
        compiler_params=_cparams(2),
        name="attention",
    )(qsrc, csrc, csrc, lsrc, lsrc, gg, *cast_arrays)
    return outs[0], tuple(outs[1:])


MIX_CHUNK = 128
SCONV_PAD = 8
CONF_PAD = 16


LANES = 128
SUBLANES = 8
N_LANE_GROUPS = SCONV_W // LANES


def _window_conv(p_ref, slot, w_ref, group, first, n_taps, ch):
    lanes = slice(group * LANES, (group + 1) * LANES)
    win = p_ref[slot, :, lanes]
    n_rows = win.shape[0]
    acc = None
    for phase in range(SUBLANES):
        taps = [k for k in range(n_taps) if (first + k) % SUBLANES == phase]
        if not taps:
            continue
        shifted = win if phase == 0 else pltpu.roll(win, n_rows - phase, 0)
        for k in taps:
            base = first + k - phase
            term = w_ref[k:k + 1, lanes] * shifted[base:base + ch]
            acc = term if acc is None else acc + term
    return acc


def _mix_kernel(sb_ref, sg_ref, sx_ref, ga_ref, gb_ref, sw_ref, dw_ref, db_ref, lng_ref, lnb_ref,
                pw_ref, pb_ref, ggs_ref, ggc_ref, o_ref, p1_ref, p2_ref, a_ref):
    seq = sb_ref.shape[0]
    ch = MIX_CHUNK
    n_chunks = seq // ch
    hs, hc = SCONV_PAD, CONF_PAD
    p1_ref[1, 0:hs, :] = jnp.zeros((hs, SCONV_W), F32)
    p1_ref[n_chunks, ch + hs:ch + 2 * hs, :] = jnp.zeros((hs, SCONV_W), F32)
    p2_ref[1, 0:hc, :] = jnp.zeros((hc, CONF_W), F32)
    p2_ref[n_chunks, ch + hc:ch + 2 * hc, :] = jnp.zeros((hc, CONF_W), F32)

    def fill(c, carry):
        r = pl.multiple_of(c * ch, ch)
        rows = pl.ds(r, ch)
        v = sg_ref[rows, :].astype(F32) * sx_ref[rows, :].astype(F32)
        p1_ref[c + 1, hs:hs + ch, :] = v
        p1_ref[c, ch + hs:ch + 2 * hs, :] = v[0:hs]
        p1_ref[c + 2, 0:hs, :] = v[ch - hs:ch]
        u = ga_ref[rows, :].astype(F32) * _sigmoid(gb_ref[rows, :].astype(F32))
        p2_ref[c + 1, hc:hc + ch, :] = u
        p2_ref[c, ch + hc:ch + 2 * hc, :] = u[0:hc]
        p2_ref[c + 2, 0:hc, :] = u[ch - hc:ch]
        return carry

    lax.fori_loop(0, n_chunks, fill, 0)

    def conv(c, carry):
        r = pl.multiple_of(c * ch, ch)
        rows = pl.ds(r, ch)
        acc = jnp.concatenate(
            [_window_conv(p1_ref, c + 1, sw_ref, g, hs - SCONV_K // 2, SCONV_K, ch) for g in range(N_LANE_GROUPS)],
            axis=1)
        ys = sb_ref[rows, :].astype(F32) * acc
        ms = jnp.mean(ys * ys, axis=-1, keepdims=True)
        o_ref[rows, 0:SCONV_W] = (ys * lax.rsqrt(ms + EPS) * ggs_ref[...]).astype(o_ref.dtype)

        u = db_ref[...] + jnp.concatenate(
            [_window_conv(p2_ref, c + 1, dw_ref, g, hc - CONF_K // 2, CONF_K, ch) for g in range(N_LANE_GROUPS)],
            axis=1)
        mu = jnp.mean(u, axis=-1, keepdims=True)
        uc = u - mu
        var = jnp.mean(uc * uc, axis=-1, keepdims=True)
        v = uc * lax.rsqrt(var + EPS) * lng_ref[...] + lnb_ref[...]
        a_ref[rows, :] = (v * _sigmoid(v)).astype(a_ref.dtype)
        return carry

    lax.fori_loop(0, n_chunks, conv, 0)

    yc = jnp.dot(a_ref[...], pw_ref[...], preferred_element_type=F32) + pb_ref[...]
    ms = jnp.mean(yc * yc, axis=-1, keepdims=True)
    o_ref[:, SCONV_W:] = (yc * lax.rsqrt(ms + EPS) * ggc_ref[...]).astype(o_ref.dtype)


def _local_mixers(src, sconv_w, conf_dw, conf_db, ln_g, ln_b, pw, pb, ggs, ggc, *, n_seq, seq):
    w = SCONV_W
    col = lambda k: pl.BlockSpec((seq, w), lambda b: (b, k))
    vec = lambda n: pl.BlockSpec((1, n), lambda b: (0, 0))
    return pl.pallas_call(
        _mix_kernel,
        grid=(n_seq,),
        in_specs=[col(3), col(4), col(5), col(6), col(7),
                  pl.BlockSpec((SCONV_K, w), lambda b: (0, 0)),
                  pl.BlockSpec((CONF_K, w), lambda b: (0, 0)),
                  vec(w), vec(w), vec(w),
                  pl.BlockSpec((w, w), lambda b: (0, 0)),
                  vec(w), vec(w), vec(w)],
        out_specs=pl.BlockSpec((seq, 2 * w), lambda b: (b, 0)),
        out_shape=jax.ShapeDtypeStruct((src.shape[0], 2 * w), BF16),
        scratch_shapes=[pltpu.VMEM((seq // MIX_CHUNK + 2, MIX_CHUNK + 2 * SCONV_PAD, w), F32),
                        pltpu.VMEM((seq // MIX_CHUNK + 2, MIX_CHUNK + 2 * CONF_PAD, w), F32),
                        pltpu.VMEM((seq, w), BF16)],
        compiler_params=_cparams(1),
        name="local_mixers",
    )(src, src, src, src, src, sconv_w, conf_dw, conf_db, ln_g, ln_b, pw, pb, ggs, ggc)


OUT_TN = 512


def _out_kernel(a_ref, m_ref, wa_ref, wm_ref, x_ref, gate_ref, o_ref):
    a = a_ref[...]
    mx = m_ref[...]
    for t in range(o_ref.shape[1] // OUT_TN):
        sl = slice(t * OUT_TN, (t + 1) * OUT_TN)
        acc = (jnp.dot(a, wa_ref[:, sl], preferred_element_type=F32)
               + jnp.dot(mx, wm_ref[:, sl], preferred_element_type=F32))
        o_ref[:, sl] = x_ref[:, sl] + gate_ref[0, :, sl] * acc


def _out_proj(att, mix, w, x2, mod3, mod_row, *, tm):
    m, d = x2.shape
    half = att.shape[1]
    return pl.pallas_call(
        _out_kernel,
        grid=(m // tm,),
        in_specs=[pl.BlockSpec((tm, half), lambda i: (i, 0)),
                  pl.BlockSpec((tm, half), lambda i: (i, 0)),
                  _const_spec((half, d), (0, 0)),
                  _const_spec((half, d), (1, 0)),
                  pl.BlockSpec((tm, d), lambda i: (i, 0)),
                  pl.BlockSpec((1, 1, d), lambda i: (mod_row(i), 0, 2))],
        out_specs=pl.BlockSpec((tm, d), lambda i: (i, 0)),
        out_shape=jax.ShapeDtypeStruct((m, d), F32),
        compiler_params=_cparams(1),
        name="out_proj",
    )(att, mix, w, w, x2, mod3)


MXU_WIDTH = 256


def _swiglu_accumulate(h, wg_ref, wu_ref, wd_refs, o_ref, assign=False):
    tm, tf = h.shape[0], wg_ref.shape[1]
    if tf % (2 * MXU_WIDTH) == 0:
        splits = [(slice(0, tm), slice(p * tf // 2, (p + 1) * tf // 2)) for p in range(2)]
    else:
        splits = [(slice(p * tm // 2, (p + 1) * tm // 2), slice(0, tf)) for p in range(2)]
    ups = [(jnp.dot(h[rows], wg_ref[:, cols], preferred_element_type=F32),
            jnp.dot(h[rows], wu_ref[:, cols], preferred_element_type=F32)) for rows, cols in splits]
    acts = [(gv * _sigmoid(gv) * uv).astype(BF16) for gv, uv in ups]
    same_rows = splits[0][0] == splits[1][0]
    col = 0
    for wd_ref in wd_refs:
        width = wd_ref.shape[1]
        downs = [jnp.dot(act, wd_ref[cols, :], preferred_element_type=F32) for act, (_, cols) in zip(acts, splits)]
        pieces = [(splits[0][0], downs[0] + downs[1])] if same_rows else list(zip((s[0] for s in splits), downs))
        for rows, part in pieces:
            if assign:
                o_ref[rows, col:col + width] = part
            else:
                o_ref[rows, col:col + width] += part
        col += width


def _ffn_kernel(x_ref, shift_ref, scale_ref, gate_ref, g_ref, wg_ref, wu_ref, wd_ref, *rest, n_cast):
    cast_in, o_ref, cast_out, h_ref = rest[:n_cast], rest[n_cast], rest[n_cast + 1:2 * n_cast + 1], rest[-1]
    j = pl.program_id(1)

    @pl.when(j == 0)
    def _():
        _cast_slabs(cast_in, cast_out)
        _norm_modulate_rows(x_ref, h_ref, g_ref[...], shift_ref[0], scale_ref[0], static=True)
        _swiglu_accumulate(h_ref[...], wg_ref, wu_ref, (wd_ref,), o_ref, assign=True)

    last = pl.num_programs(1) - 1

    @pl.when(jnp.logical_and(j > 0, j < last))
    def _():
        _cast_slabs(cast_in, cast_out)
        _swiglu_accumulate(h_ref[...], wg_ref, wu_ref, (wd_ref,), o_ref)

    @pl.when(j == last)
    def _():
        _cast_slabs(cast_in, cast_out)
        _swiglu_accumulate(h_ref[...], wg_ref, wu_ref, (wd_ref,), o_ref)
        o_ref[...] = x_ref[...] + gate_ref[0] * o_ref[...]


BF16_SUBLANE_TILE = 16


def _slab_rows(rows, n_lead, n_steps):
    for height in range(BF16_SUBLANE_TILE, rows + 1, BF16_SUBLANE_TILE):
        if rows % height == 0 and n_lead * (rows // height) <= n_steps:
            return height
    raise ValueError(f"no slab height for rows={rows}, n_lead={n_lead}, n_steps={n_steps}")


def _cast_plan(jobs, n_steps, step_of):
    arrays, in_specs, out_specs, out_shapes = [], [], [], []
    for arr, col_block, n_col_blocks, *only in jobs:
        _, rows, cols = arr.shape
        lead0, n_lead = (only[0], 1) if only else (0, arr.shape[0])
        width = cols // n_col_blocks
        height = _slab_rows(rows, n_lead, n_steps)
        per_lead = rows // height
        last = n_lead * per_lead - 1

        def index(*grid_idx, per_lead=per_lead, last=last, col=0, lead0=0):
            s = jnp.minimum(step_of(*grid_idx), last)
            return (lead0 + s // per_lead, s % per_lead, col)

        arrays.append(arr)
        in_specs.append(pl.BlockSpec((1, height, width), functools.partial(index, col=col_block, lead0=lead0)))
        out_specs.append(pl.BlockSpec((1, height, width), index))
        out_shapes.append(jax.ShapeDtypeStruct((n_lead, rows, width), BF16))
    return arrays, in_specs, out_specs, out_shapes


def _cast_slabs(cast_in, cast_out):
    for src, dst in zip(cast_in, cast_out):
        dst[...] = src[...].astype(dst.dtype)


def _dense_ffn(x2, mod3, mod_row, g, wg, wu, wd, *, tm, tf, cast=()):
    m, d = x2.shape
    f = wg.shape[2]
    nf = f // tf
    assert nf >= 2, "the first and the last hidden tile run different blocks"
    n_steps = (m // tm) * nf
    modspec = lambda k: pl.BlockSpec((1, 1, d), lambda i, j: (mod_row(i), 0, k))
    cast_arrays, cast_in, cast_out, cast_shapes = _cast_plan(cast, n_steps, lambda i, j: i * nf + j)
    outs = pl.pallas_call(
        functools.partial(_ffn_kernel, n_cast=len(cast)),
        grid=(m // tm, nf),
        in_specs=[pl.BlockSpec((tm, d), lambda i, j: (i, 0)),
                  modspec(3), modspec(4), modspec(5),
                  pl.BlockSpec((1, d), lambda i, j: (0, 0)),
                  pl.BlockSpec((None, d, tf), lambda i, j: (0, 0, j)),
                  pl.BlockSpec((None, d, tf), lambda i, j: (0, 0, j)),
                  pl.BlockSpec((None, tf, d), lambda i, j: (0, j, 0))] + cast_in,
        out_specs=[pl.BlockSpec((tm, d), lambda i, j: (i, 0))] + cast_out,
        out_shape=[jax.ShapeDtypeStruct((m, d), F32)] + cast_shapes,
        scratch_shapes=[pltpu.VMEM((tm, d), BF16)],
        compiler_params=_cparams(2),
        name="dense_ffn",
    )(x2, mod3, mod3, mod3, g, wg, wu, wd, *cast_arrays)
    return outs[0], tuple(outs[1:])


ROUTER_ROWS = 16


def _router_kernel(x_ref, shift_ref, scale_ref, g_ref, rw_ref, t_ref, idx_ref, gate_ref, cnt_ref,
                   tri_ref, base_ref):
    tm = x_ref.shape[0]

    @pl.when(pl.program_id(0) == 0)
    def _():
        r = lax.broadcasted_iota(jnp.int32, (tm, tm), 0)
        c = lax.broadcasted_iota(jnp.int32, (tm, tm), 1)
        tri_ref[...] = jnp.where(r < c, 1.0, 0.0).astype(BF16)
        base_ref[...] = jnp.zeros(base_ref.shape, F32)

    _norm_modulate_rows(x_ref, t_ref, g_ref[...], shift_ref[0], scale_ref[0], static=True)

    t = t_ref[...]
    t_hi = t.astype(BF16)
    t_lo = (t - t_hi.astype(F32)).astype(BF16)
    w = rw_ref[...]
    w_hi = w.astype(BF16)
    w_lo = (w - w_hi.astype(F32)).astype(BF16)
    dn = (((1,), (1,)), ((), ()))
    logits = (lax.dot_general(w_hi, t_hi, dn, preferred_element_type=F32)
              + lax.dot_general(w_lo, t_hi, dn, preferred_element_type=F32)
              + lax.dot_general(w_hi, t_lo, dn, preferred_element_type=F32))

    e = lax.broadcasted_iota(jnp.int32, (ROUTER_ROWS, tm), 0).astype(F32)
    neg = jnp.float32(-jnp.inf)
    lg = jnp.where(e < N_EXPERTS, logits, neg)
    m1 = jnp.max(lg, axis=0, keepdims=True)
    i1 = jnp.min(jnp.where(lg == m1, e, float(ROUTER_ROWS)), axis=0, keepdims=True)
    lg2 = jnp.where(e == i1, neg, lg)
    m2 = jnp.max(lg2, axis=0, keepdims=True)
    i2 = jnp.min(jnp.where(lg2 == m2, e, float(ROUTER_ROWS)), axis=0, keepdims=True)
    ex = jnp.exp(m2 - m1)
    den = 1.0 + ex
    gate_ref[0:1, :] = 1.0 / den
    gate_ref[1:2, :] = ex / den

    hit1 = e == i1
    hit2 = e == i2
    onehot = jnp.where(hit1 | hit2, 1.0, 0.0)
    prefix = jnp.dot(onehot.astype(BF16), tri_ref[...], preferred_element_type=F32) + base_ref[:, 0:1]
    r1 = jnp.sum(jnp.where(hit1, prefix, 0.0), axis=0, keepdims=True)
    r2 = jnp.sum(jnp.where(hit2, prefix, 0.0), axis=0, keepdims=True)
    idx_ref[0:1, :] = i1.astype(jnp.int32)
    idx_ref[1:2, :] = i2.astype(jnp.int32)
    idx_ref[2:3, :] = r1.astype(jnp.int32)
    idx_ref[3:4, :] = r2.astype(jnp.int32)
    base_ref[...] = base_ref[...] + jnp.sum(onehot, axis=1, keepdims=True)
    cnt_ref[...] = base_ref[...].astype(jnp.int32)


def _router(x2, mod3, mod_row, g, rw16, *, tm):
    m, d = x2.shape
    modspec = lambda k: pl.BlockSpec((1, 1, d), lambda i: (mod_row(i), 0, k))
    return pl.pallas_call(
        _router_kernel,
        grid=(m // tm,),
        in_specs=[pl.BlockSpec((tm, d), lambda i: (i, 0)),
                  modspec(3), modspec(4),
                  pl.BlockSpec((1, d), lambda i: (0, 0)),
                  pl.BlockSpec((ROUTER_ROWS, d), lambda i: (0, 0))],
        out_specs=[pl.BlockSpec((tm, d), lambda i: (i, 0)),
                   pl.BlockSpec((4, tm), lambda i: (0, i)),
                   pl.BlockSpec((2, tm), lambda i: (0, i)),
                   pl.BlockSpec((ROUTER_ROWS, 128), lambda i: (0, 0))],
        out_shape=[jax.ShapeDtypeStruct((m, d), F32),
                   jax.ShapeDtypeStruct((4, m), jnp.int32),
                   jax.ShapeDtypeStruct((2, m), F32),
                   jax.ShapeDtypeStruct((ROUTER_ROWS, 128), jnp.int32)],
        scratch_shapes=[pltpu.VMEM((tm, tm), BF16), pltpu.VMEM((ROUTER_ROWS, 128), F32)],
        compiler_params=_cparams(1),
        name="router",
    )(x2, mod3, mod3, g, rw16)


def _row_copy(src_ref, src_row, dst_ref, dst_row, sem):
    return pltpu.make_async_copy(src_ref.at[pl.ds(src_row, 1), :], dst_ref.at[pl.ds(dst_row, 1), :], sem)


DMA_ISSUE_UNROLL = 8


def _scatter_kernel(dest_ref, pad_ref, t_ref, buf_ref, zero_ref, sem, zsem, *, n_tok, n_tiles):
    tm = t_ref.shape[0]
    base = pl.program_id(0) * tm

    @pl.when(pl.program_id(0) == 0)
    def _():
        zero_ref[...] = jnp.zeros(zero_ref.shape, zero_ref.dtype)
        for e in range(N_EXPERTS):
            start, length = pad_ref[e], pad_ref[N_EXPERTS + e]

            def fill(r, carry, start=start):
                _row_copy(zero_ref, 0, buf_ref, start + r, zsem).start()
                return carry

            def drain(r, carry):
                _row_copy(zero_ref, 0, buf_ref, 0, zsem).wait()
                return carry

            lax.fori_loop(0, length, fill, 0)
            lax.fori_loop(0, length, drain, 0)

        def tile_copy(tile):
            row0 = pl.multiple_of(tile * EXPERT_TM, EXPERT_TM)
            return pltpu.make_async_copy(zero_ref, buf_ref.at[pl.ds(row0, EXPERT_TM), :], zsem)

        def fill_tile(tile, carry):
            tile_copy(tile).start()
            return carry

        def drain_tile(tile, carry):
            tile_copy(tile).wait()
            return carry

        lax.fori_loop(pad_ref[2 * N_EXPERTS], n_tiles, fill_tile, 0)
        lax.fori_loop(pad_ref[2 * N_EXPERTS], n_tiles, drain_tile, 0)

    def start(r, carry):
        for k in range(2):
            _row_copy(t_ref, r, buf_ref, dest_ref[k * n_tok + base + r], sem).start(priority=k)
        return carry

    lax.fori_loop(0, tm, start, 0, unroll=DMA_ISSUE_UNROLL)
    for k in range(2):
        pltpu.make_async_copy(t_ref, buf_ref.at[pl.ds(0, tm), :], sem).wait()


def _scatter_rows(dest, pad_info, t, *, tm, n_tiles):
    m, d = t.shape
    grid_spec = pltpu.PrefetchScalarGridSpec(
        num_scalar_prefetch=2,
        grid=(m // tm,),
        in_specs=[pl.BlockSpec((tm, d), lambda i, dest, pad: (i, 0))],
        out_specs=pl.BlockSpec(memory_space=pl.ANY),
        scratch_shapes=[pltpu.VMEM((EXPERT_TM, d), t.dtype),
                        pltpu.SemaphoreType.DMA(()), pltpu.SemaphoreType.DMA(())],
    )
    return pl.pallas_call(
        functools.partial(_scatter_kernel, n_tok=m, n_tiles=n_tiles),
        grid_spec=grid_spec,
        out_shape=jax.ShapeDtypeStruct((n_tiles * EXPERT_TM, d), t.dtype),
        compiler_params=_cparams(1),
        name="moe_scatter",
    )(dest, pad_info, t)


N_WEIGHT_SLOTS = 3


def _expert_kernel(te_ref, tv_ref, tx_ref, x_ref, *rest, n_wd, tf, nf):
    del tx_ref
    w_hbm = rest[:2 + n_wd]
    o_ref, xb_ref = rest[2 + n_wd], rest[3 + n_wd]
    w_buf = rest[4 + n_wd:6 + 2 * n_wd]
    sem = rest[-1]
    i = pl.program_id(0)
    n = pl.num_programs(0)

    def copies(tile, j, slot):
        e = te_ref[tile]
        cols = pl.ds(j * tf, tf)
        srcs = [w_hbm[0].at[e, :, cols], w_hbm[1].at[e, :, cols]] + [w.at[e, cols, :] for w in w_hbm[2:]]
        return [pltpu.make_async_copy(src, buf.at[slot], sem.at[slot, k])
                for k, (src, buf) in enumerate(zip(srcs, w_buf))]

    valid = tv_ref[i] == 1
    nxt = jnp.minimum(i + 1, n - 1)
    next_live = jnp.logical_and(i + 1 < n, tv_ref[nxt] == 1)

    @pl.when(jnp.logical_not(valid))
    def _():
        o_ref[...] = jnp.zeros(o_ref.shape, o_ref.dtype)

    @pl.when(jnp.logical_and(valid, i == 0))
    def _():
        for c in copies(i, 0, 0):
            c.start()

    @pl.when(valid)
    def _():
        xb_ref[...] = x_ref[...].astype(BF16)
        xb = xb_ref[...]
        for j in range(nf):
            slot = j % N_WEIGHT_SLOTS
            ahead = copies(i, j + 1, (j + 1) % N_WEIGHT_SLOTS) if j + 1 < nf else copies(nxt, 0, 0)
            for c in ahead:
                c.start()
            for c in copies(i, j, slot):
                c.wait()
            _swiglu_accumulate(xb, w_buf[0].at[slot], w_buf[1].at[slot], [b.at[slot] for b in w_buf[2:]],
                               o_ref, assign=j == 0)

    @pl.when(jnp.logical_and(valid, jnp.logical_not(next_live)))
    def _():
        for c in copies(nxt, 0, 0):
            c.wait()


def _experts(tile_e, tile_v, tile_x, buf, wg, wu, wds, *, tf):
    rows, d = buf.shape
    f = wg.shape[2]
    nf = f // tf
    assert nf % N_WEIGHT_SLOTS != 1, "slot 0 must be free while the last hidden tile computes"
    any_spec = pl.BlockSpec(memory_space=pl.ANY)
    grid_spec = pltpu.PrefetchScalarGridSpec(
        num_scalar_prefetch=3,
        grid=(rows // EXPERT_TM,),
        in_specs=[pl.BlockSpec((EXPERT_TM, d), lambda i, te, tv, tx: (tx[i], 0))] + [any_spec] * (2 + len(wds)),
        out_specs=pl.BlockSpec((EXPERT_TM, d), lambda i, te, tv, tx: (i, 0)),
        scratch_shapes=[pltpu.VMEM((EXPERT_TM, d), BF16),
                        pltpu.VMEM((N_WEIGHT_SLOTS, d, tf), BF16),
                        pltpu.VMEM((N_WEIGHT_SLOTS, d, tf), BF16)]
                       + [pltpu.VMEM((N_WEIGHT_SLOTS, tf, wd.shape[2]), BF16) for wd in wds]
                       + [pltpu.SemaphoreType.DMA((N_WEIGHT_SLOTS, 2 + len(wds)))],
    )
    return pl.pallas_call(
        functools.partial(_expert_kernel, n_wd=len(wds), tf=tf, nf=nf),
        grid_spec=grid_spec,
        out_shape=jax.ShapeDtypeStruct((rows, d), F32),
        compiler_params=_cparams(1),
        name="moe_experts",
    )(tile_e, tile_v, tile_x, buf, wg, wu, *wds)


def _combine_kernel(dest_ref, x_ref, gate5_ref, gates_ref, gf_ref, ybuf_ref, o_ref, rows_ref, sem, *, n_tok):
    tm = x_ref.shape[0]
    i = pl.program_id(0)

    def issue(tile, slot):
        base = tile * tm

        def start(r, carry):
            for k in range(2):
                _row_copy(ybuf_ref, dest_ref[k * n_tok + base + r], rows_ref.at[slot, k], r,
                          sem.at[slot]).start(priority=k)
            return carry

        lax.fori_loop(0, tm, start, 0, unroll=DMA_ISSUE_UNROLL)

    @pl.when(i == 0)
    def _():
        issue(0, 0)

    @pl.when(i + 1 < pl.num_programs(0))
    def _():
        issue(i + 1, (i + 1) % 2)

    slot = i % 2
    for k in range(2):
        pltpu.make_async_copy(ybuf_ref.at[pl.ds(0, tm), :], rows_ref.at[slot, k], sem.at[slot]).wait()

    eye = (lax.broadcasted_iota(jnp.int32, (tm, tm), 0) == lax.broadcasted_iota(jnp.int32, (tm, tm), 1))
    g0 = jnp.sum(jnp.where(eye, gates_ref[0:1, :], 0.0), axis=1, keepdims=True)
    g1 = jnp.sum(jnp.where(eye, gates_ref[1:2, :], 0.0), axis=1, keepdims=True)
    y = g0 * rows_ref[slot, 0] + g1 * rows_ref[slot, 1]
    xn = x_ref[...] + gate5_ref[0] * y
    ms = jnp.mean(xn * xn, axis=-1, keepdims=True)
    o_ref[...] = xn * lax.rsqrt(ms + EPS) * gf_ref[...]


def _combine(dest, x2, mod3, mod_row, gates, g_final, ybuf, *, tm):
    m, d = x2.shape
    grid_spec = pltpu.PrefetchScalarGridSpec(
        num_scalar_prefetch=1,
        grid=(m // tm,),
        in_specs=[pl.BlockSpec((tm, d), lambda i, dest: (i, 0)),
                  pl.BlockSpec((1, 1, d), lambda i, dest: (mod_row(i), 0, 5)),
                  pl.BlockSpec((2, tm), lambda i, dest: (0, i)),
                  pl.BlockSpec((1, d), lambda i, dest: (0, 0)),
                  pl.BlockSpec(memory_space=pl.ANY)],
        out_specs=pl.BlockSpec((tm, d), lambda i, dest: (i, 0)),
        scratch_shapes=[pltpu.VMEM((2, 2, tm, d), F32), pltpu.SemaphoreType.DMA((2,))],
    )
    return pl.pallas_call(
        functools.partial(_combine_kernel, n_tok=m),
        grid_spec=grid_spec,
        out_shape=jax.ShapeDtypeStruct((m, d), F32),
        compiler_params=_cparams(1),
        name="moe_combine",
    )(dest, x2, mod3, gates, g_final, ybuf)


def _rope_tables(seq):
    rows = seq // GRID_W
    row = jnp.repeat(jnp.arange(rows), GRID_W).astype(F32)
    col = jnp.tile(jnp.arange(GRID_W), rows).astype(F32)
    axis_dim = HEAD_DIM // 2
    inv_freq = ROPE_THETA ** (-jnp.arange(0, axis_dim, 2, dtype=F32) / axis_dim)
    ang_r = row[:, None] * inv_freq
    ang_c = col[:, None] * inv_freq
    cr, sr, cc, sc = jnp.cos(ang_r), jnp.sin(ang_r), jnp.cos(ang_c), jnp.sin(ang_c)
    return (jnp.concatenate([cr, cr, cc, cc], axis=-1),
            jnp.concatenate([-sr, sr, -sc, sc], axis=-1))


def kernel(x, c, ctx, c_ctx, w_ada, b_ada, g_mix, w_in, g_q, g_k, sconv_w, conf_dw, conf_db, conf_ln_g,
           conf_ln_b, conf_pw, conf_pb, g_group, w_o, g_ffn, dense_wg, dense_wu, dense_wd, router_w,
           moe_wg, moe_wu, moe_wd, g_final):
    b, s, d = x.shape
    n_ctx = ctx.shape[1]
    depth = w_ada.shape[0]
    assert b + 1 <= MOD_ROWS and depth == 2
    m_lat, m_ctx = b * s, b * n_ctx
    ctx_row = b

    cin = jnp.concatenate([c, c_ctx[None, :], jnp.zeros((MOD_ROWS - b - 1, d), F32)], axis=0)
    mod = _ada(cin, w_ada, b_ada)
    cos_t, sin_t = _rope_tables(s)
    row2 = lambda v: v.reshape(1, -1)

    lat_tm = ROW_TM
    lat_row_for = lambda tm: (lambda i: i // (s // tm))
    lat_row = lat_row_for(lat_tm)
    ctx_tm = ROW_TM
    ctx_mod_row = lambda i: ctx_row

    x2 = x.reshape(m_lat, d)
    xc2 = ctx.reshape(m_ctx, d)
    out = None
    moe_wd_b = ()
    next_proj_b = ()
    for l in range(depth):
        last = l == depth - 1
        mod3 = mod[l].reshape(MOD_ROWS, 1, 6 * d)
        if next_proj_b:
            w_in_b, w_o_b = (w.reshape(w.shape[1:]) for w in next_proj_b)
            next_proj_b = ()
        else:
            w_in_b, w_o_b = w_in[l].astype(BF16), w_o[l].astype(BF16)
        pw_b = conf_pw[l].astype(BF16)
        gg = g_group[l]
        gga, ggs, ggc = row2(gg[:ATTN_W]), row2(gg[ATTN_W:ATTN_W + SCONV_W]), row2(gg[ATTN_W + SCONV_W:])
        mixer_args = (sconv_w[l], conf_dw[l], row2(conf_db[l]), row2(conf_ln_g[l]), row2(conf_ln_b[l]),
                      pw_b, row2(conf_pb[l]), ggs, ggc)
        in_args = (row2(g_mix[l]), w_in_b, row2(g_q[l]), row2(g_k[l]), cos_t, sin_t)

        dense_here = l % 2 == 0
        k_dense = l // 2
        cast = (tuple((w[k_dense:k_dense + 1], 0, 1) for w in (dense_wg, dense_wu, dense_wd))
                if dense_here else ())
        k_moe = (l + 1) // 2 if dense_here else l // 2
        moe_wd_job = ((moe_wd[k_moe], l % 2, 2),) if k_moe < moe_wd.shape[0] else ()
        p_lat, dense_w_b = _in_proj(x2, mod3, lat_row, *in_args, tm=lat_tm, rope=True, j0=0, nj=8, seq=s,
                                    cast=cast)
        if last:
            p_ctx, _ = _in_proj(xc2, mod3, ctx_mod_row, *in_args, tm=ctx_tm, rope=False, j0=2, nj=1, seq=ctx_tm)
            kc_blk = 0
        else:
            p_ctx, _ = _in_proj(xc2, mod3, ctx_mod_row, *in_args, tm=ctx_tm, rope=False, j0=0, nj=8, seq=ctx_tm)
            kc_blk = ATTN_W // KV_W
        att, moe_wd_half = _attention(p_lat, p_ctx, p_lat, gga, n_batch=b, q_len=s, tq=ATTN_TQ, kc_blk=kc_blk,
                                      has_latent=True, n_ctx=n_ctx, seq=s, cast=moe_wd_job)
        moe_wd_b = moe_wd_b + moe_wd_half
        mix = _local_mixers(p_lat, *mixer_args, n_seq=b, seq=s)
        x2 = _out_proj(att, mix, w_o_b, x2, mod3, lat_row, tm=lat_tm)
        if not last:
            att_c, _ = _attention(p_ctx, p_ctx, p_ctx, gga, n_batch=b, q_len=n_ctx, tq=n_ctx, kc_blk=kc_blk,
                                  has_latent=False, n_ctx=n_ctx, seq=n_ctx)
            mix_c = _local_mixers(p_ctx, *mixer_args, n_seq=b, seq=n_ctx)
            xc2 = _out_proj(att_c, mix_c, w_o_b, xc2, mod3, ctx_mod_row, tm=ctx_tm)

        if l % 2 == 0:
            wg_b, wu_b, wd_b = dense_w_b
            ffn = functools.partial(_dense_ffn, g=row2(g_ffn[l]), wg=wg_b, wu=wu_b, wd=wd_b,
                                    tf=FFN_TF)
            routed_next = l + 1 < depth and (l + 1) % 2 == 1
            cast = ((moe_wg[(l + 1) // 2], 0, 1), (moe_wu[(l + 1) // 2], 0, 1)) if routed_next else ()
            x2, moe_up_b = ffn(x2, mod3, lat_row, tm=lat_tm, cast=cast)
            if not last:
                proj_jobs = ((w_in, 0, 1, l + 1), (w_o, 0, 1, l + 1))
                xc2, next_proj_b = ffn(xc2, mod3, ctx_mod_row, tm=ctx_tm, cast=proj_jobs)
        else:
            assert last, "routed layer is implemented for the final layer (latent tokens only)"
            rw16 = jnp.zeros((ROUTER_ROWS, d), F32).at[:N_EXPERTS].set(router_w[l // 2].T)
            t, idx, gates, cnt = _router(x2, mod3, lat_row, row2(g_ffn[l]), rw16, tm=lat_tm)
            counts = cnt[:N_EXPERTS, 0]
            padded = (counts + EXPERT_TM - 1) // EXPERT_TM * EXPERT_TM
            pends = jnp.cumsum(padded)
            pstarts = pends - padded
            slot_e = idx[0:2]
            slot_start = sum(jnp.where(slot_e == e, pstarts[e], 0) for e in range(N_EXPERTS))
            dest = (slot_start + idx[2:4]).reshape(-1).astype(jnp.int32)
            n_tiles = (2 * m_lat) // EXPERT_TM + N_EXPERTS
            tile_start = jnp.arange(n_tiles, dtype=jnp.int32) * EXPERT_TM
            tile_v = (tile_start < pends[-1]).astype(jnp.int32)
            last_tile = pends[-1] // EXPERT_TM - 1
            tile_x = jnp.minimum(jnp.arange(n_tiles, dtype=jnp.int32), last_tile).astype(jnp.int32)
            tile_e = jnp.minimum(jnp.sum((tile_x * EXPERT_TM)[:, None] >= pends[None, :], axis=1),
                                 N_EXPERTS - 1).astype(jnp.int32)
            pad_info = jnp.concatenate([pstarts + counts, padded - counts,
                                        (pends[-1:] // EXPERT_TM)]).astype(jnp.int32)
            buf = _scatter_rows(dest, pad_info, t, tm=lat_tm, n_tiles=n_tiles)
            ybuf = _experts(tile_e, tile_v, tile_x, buf, *moe_up_b, moe_wd_b, tf=FFN_TF)
            out = _combine(dest, x2, mod3, lat_row_for(COMBINE_TM), gates, row2(g_final), ybuf, tm=COMBINE_TM)
    return out.reshape(b, s, d)
```
